```python
import jax, jax.numpy as jnp
from jax import lax
import numpy as np

D_MODEL = 2048
BATCH = 4
SEQ = 4096
DEPTH = 2

MEM_LEN = 256
EPS = 1e-6

GLA_HEADS = 4
GLA_DK = 128
GLA_DV = 256
GLA_WIDTH = GLA_HEADS * GLA_DV
GLA_KWIDTH = GLA_HEADS * GLA_DK
GLA_CHUNK = 64
GK_RANK = 16
GK_NORMALIZER = 16.0

FOX_HEADS = 8
FOX_DH = 64
FOX_WIDTH = FOX_HEADS * FOX_DH
FOX_BLOCK = 128
FORGET_BIAS_INIT = 3.0

MEM_HEADS = 4
MEM_DH = 128
MEM_WIDTH = MEM_HEADS * MEM_DH

N_BRANCH = 3
MIX_WIDTH = GLA_WIDTH + FOX_WIDTH + MEM_WIDTH

IN_SIZES = (
    GLA_KWIDTH, GLA_KWIDTH, GLA_WIDTH, GLA_WIDTH, GK_RANK,
    FOX_WIDTH, FOX_WIDTH, FOX_WIDTH, FOX_HEADS, FOX_WIDTH,
    MEM_WIDTH, MEM_WIDTH,
    N_BRANCH * D_MODEL,
)
IN_COLS = int(sum(IN_SIZES))
IN_OFFSETS = tuple(int(v) for v in np.cumsum(IN_SIZES)[:-1])
BRANCH_ROWS = (0, GLA_WIDTH, GLA_WIDTH + FOX_WIDTH, MIX_WIDTH)

kernel_name = "hybrid_gla_fox_mem_gated_merge"


def rmsnorm(x, gain):
    xf = x.astype(jnp.float32)
    out = xf * lax.rsqrt(jnp.mean(xf * xf, axis=-1, keepdims=True) + EPS)
    return (out * gain.astype(jnp.float32)).astype(x.dtype)


def gla_chunked(q, k, v, g):
    B, S, H, DK = q.shape
    DV = v.shape[-1]
    C = GLA_CHUNK
    N = S // C

    def to_chunks(t):
        return t.astype(jnp.float32).reshape(B, N, C, H, -1).transpose(1, 0, 3, 2, 4)

    qc, kc, vc, gc = to_chunks(q * (DK ** -0.5)), to_chunks(k), to_chunks(v), to_chunks(g)
    causal = jnp.tril(jnp.ones((C, C), dtype=bool))

    def step(state, inp):
        qi, ki, vi, gi = inp
        b = jnp.cumsum(gi, axis=2)
        o_inter = jnp.einsum('bhcd,bhde->bhce', qi * jnp.exp(b), state)
        diff = b[:, :, :, None, :] - b[:, :, None, :, :]
        decay = jnp.exp(jnp.where(causal[None, None, :, :, None], diff, -jnp.inf))
        scores = jnp.einsum('bhid,bhjd,bhijd->bhij', qi, ki, decay)
        o_intra = jnp.einsum('bhij,bhje->bhie', scores, vi)
        b_last = b[:, :, -1:, :]
        k_dec = ki * jnp.exp(b_last - b)
        state = state * jnp.exp(b_last[:, :, 0, :, None]) + jnp.einsum('bhcd,bhce->bhde', k_dec, vi)
        return state, o_inter + o_intra

    state0 = jnp.zeros((B, H, DK, DV), jnp.float32)
    _, outs = lax.scan(step, state0, (qc, kc, vc, gc))
    return outs.transpose(1, 0, 3, 2, 4).reshape(B, S, H, DV)


def forgetting_attention(q, k, v, log_f):
    B, S, H, Dh = q.shape
    nb = S // FOX_BLOCK
    c = jnp.cumsum(log_f, axis=1).transpose(0, 2, 1)
    kh = k.transpose(0, 2, 1, 3)
    vh = v.transpose(0, 2, 1, 3)
    q_blocks = q.transpose(0, 2, 1, 3).reshape(B, H, nb, FOX_BLOCK, Dh).transpose(2, 0, 1, 3, 4)
    cq_blocks = c.reshape(B, H, nb, FOX_BLOCK).transpose(2, 0, 1, 3)
    key_pos = jnp.arange(S)
    scale = Dh ** -0.5

    def block(args):
        qb, cqb, idx = args
        s = jnp.einsum('bhqd,bhkd->bhqk', qb, kh).astype(jnp.float32) * scale
        s = s + cqb[..., None] - c[:, :, None, :]
        q_pos = idx * FOX_BLOCK + jnp.arange(FOX_BLOCK)
        s = jnp.where(key_pos[None, :] <= q_pos[:, None], s, -jnp.inf)
        p = jax.nn.softmax(s, axis=-1)
        return jnp.einsum('bhqk,bhkd->bhqd', p.astype(vh.dtype), vh)

    out = lax.map(block, (q_blocks, cq_blocks, jnp.arange(nb)))
    return out.transpose(1, 0, 3, 2, 4).reshape(B, S, H, Dh)


def memory_attention(q, mem_k, mem_v):
    s = jnp.einsum('bqhd,bkhd->bhqk', q, mem_k).astype(jnp.float32) * (q.shape[-1] ** -0.5)
    p = jax.nn.softmax(s, axis=-1)
    return jnp.einsum('bhqk,bkhd->bqhd', p.astype(mem_v.dtype), mem_v)


def setup_inputs(seed: int = 0) -> dict:
    key = jax.random.key(seed)
    ks = jax.random.split(key, 16)
    f32 = jnp.float32
    x = jax.random.normal(ks[0], (BATCH, SEQ, D_MODEL), f32)
    mem = jax.random.normal(ks[1], (BATCH, MEM_LEN, D_MODEL), f32)
    norm_gain = 1.0 + 0.02 * jax.random.normal(ks[2], (DEPTH, D_MODEL), f32)
    w_in = jax.random.normal(ks[3], (DEPTH, D_MODEL, IN_COLS), f32) * D_MODEL ** -0.5
    w_gk_up = jax.random.normal(ks[4], (DEPTH, GK_RANK, GLA_KWIDTH), f32) * GK_RANK ** -0.5
    b_gk = 0.1 * jax.random.normal(ks[5], (DEPTH, GLA_KWIDTH), f32)
    gla_norm_gain = 1.0 + 0.02 * jax.random.normal(ks[6], (DEPTH, GLA_DV), f32)
    b_f = FORGET_BIAS_INIT + 0.1 * jax.random.normal(ks[7], (DEPTH, FOX_HEADS), f32)
    mem_norm_gain = 1.0 + 0.02 * jax.random.normal(ks[8], (DEPTH, D_MODEL), f32)
    w_mem_kv = jax.random.normal(ks[9], (DEPTH, D_MODEL, 2 * MEM_WIDTH), f32) * D_MODEL ** -0.5
    kb = jax.random.split(ks[10], 3)
    w_branch = jnp.concatenate([
        jax.random.normal(kb[0], (DEPTH, GLA_WIDTH, D_MODEL), f32) * GLA_WIDTH ** -0.5,
        jax.random.normal(kb[1], (DEPTH, FOX_WIDTH, D_MODEL), f32) * FOX_WIDTH ** -0.5,
        jax.random.normal(kb[2], (DEPTH, MEM_WIDTH, D_MODEL), f32) * MEM_WIDTH ** -0.5,
    ], axis=1)
    w_out = jax.random.normal(ks[11], (DEPTH, D_MODEL, D_MODEL), f32) * D_MODEL ** -0.5
    final_gain = 1.0 + 0.02 * jax.random.normal(ks[12], (D_MODEL,), f32)
    return {"x": x, "mem": mem, "norm_gain": norm_gain, "w_in": w_in, "w_gk_up": w_gk_up,
            "b_gk": b_gk, "gla_norm_gain": gla_norm_gain, "b_f": b_f,
            "mem_norm_gain": mem_norm_gain, "w_mem_kv": w_mem_kv, "w_branch": w_branch,
            "w_out": w_out, "final_gain": final_gain}


def reference(x, mem, norm_gain, w_in, w_gk_up, b_gk, gla_norm_gain, b_f,
              mem_norm_gain, w_mem_kv, w_branch, w_out, final_gain):
    B, S, D = x.shape
    M = mem.shape[1]
    for l in range(DEPTH):
        h = rmsnorm(x, norm_gain[l])
        z = h @ w_in[l]
        (gq, gk_, gv, ggate, gdown, fq, fk, fv, flogit, fgate,
         mq, mgate, merge) = jnp.split(z, IN_OFFSETS, axis=-1)

        gk_log = jax.nn.log_sigmoid((gdown @ w_gk_up[l] + b_gk[l]).astype(jnp.float32)) / GK_NORMALIZER
        o_a = gla_chunked(gq.reshape(B, S, GLA_HEADS, GLA_DK), gk_.reshape(B, S, GLA_HEADS, GLA_DK),
                          gv.reshape(B, S, GLA_HEADS, GLA_DV), gk_log.reshape(B, S, GLA_HEADS, GLA_DK))
        o_a = rmsnorm(o_a, gla_norm_gain[l]).reshape(B, S, GLA_WIDTH).astype(x.dtype)
        o_a = o_a * jax.nn.silu(ggate)

        log_f = jax.nn.log_sigmoid((flogit + b_f[l]).astype(jnp.float32))
        o_b = forgetting_attention(fq.reshape(B, S, FOX_HEADS, FOX_DH), fk.reshape(B, S, FOX_HEADS, FOX_DH),
                                   fv.reshape(B, S, FOX_HEADS, FOX_DH), log_f)
        o_b = o_b.reshape(B, S, FOX_WIDTH) * jax.nn.silu(fgate)

        mkv = rmsnorm(mem, mem_norm_gain[l]) @ w_mem_kv[l]
        mk, mv = jnp.split(mkv, 2, axis=-1)
        o_c = memory_attention(mq.reshape(B, S, MEM_HEADS, MEM_DH), mk.reshape(B, M, MEM_HEADS, MEM_DH),
                               mv.reshape(B, M, MEM_HEADS, MEM_DH))
        o_c = o_c.reshape(B, S, MEM_WIDTH) * jax.nn.silu(mgate)

        gates = jax.nn.sigmoid(merge.reshape(B, S, N_BRANCH, D))
        wb = w_branch[l]
        y = (gates[:, :, 0] * (o_a @ wb[BRANCH_ROWS[0]:BRANCH_ROWS[1]])
             + gates[:, :, 1] * (o_b @ wb[BRANCH_ROWS[1]:BRANCH_ROWS[2]])
             + gates[:, :, 2] * (o_c @ wb[BRANCH_ROWS[2]:BRANCH_ROWS[3]]))
        x = x + y @ w_out[l]
    return rmsnorm(x, final_gain)
```

```python
import functools

import jax
import jax.numpy as jnp
from jax import lax
from jax.experimental import pallas as pl
from jax.experimental.pallas import tpu as pltpu

F32 = jnp.float32
BF16 = jnp.bfloat16

EPS = 1e-6

GLA_HEADS, GLA_DK, GLA_DV = 4, 128, 256
GLA_KWIDTH = GLA_HEADS * GLA_DK
GLA_WIDTH = GLA_HEADS * GLA_DV
GK_RANK = 16
GK_NORMALIZER = 16.0
FOX_HEADS, FOX_DH = 8, 64
FOX_WIDTH = FOX_HEADS * FOX_DH
MEM_HEADS, MEM_DH = 4, 128
MEM_WIDTH = MEM_HEADS * MEM_DH
N_BRANCH = 3

LANES = 128
VMEM_LIMIT_BYTES = 56 * 1024 * 1024

ZM_GQ = 0
ZM_GK = ZM_GQ + GLA_KWIDTH
ZM_GV = ZM_GK + GLA_KWIDTH
ZM_GGATE = ZM_GV + GLA_WIDTH
ZM_FQ = ZM_GGATE + GLA_WIDTH
ZM_FK = ZM_FQ + FOX_WIDTH
ZM_FV = ZM_FK + FOX_WIDTH
ZM_FGATE = ZM_FV + FOX_WIDTH
ZM_MQ = ZM_FGATE + FOX_WIDTH
ZM_MGATE = ZM_MQ + MEM_WIDTH
ZM_MERGE = ZM_MGATE + MEM_WIDTH
ZS_GDOWN = 0
ZS_FLOGIT = GK_RANK

GLA_CHUNK = 64
GLA_SUB = 16
FOX_EXTRA = 6


def _cparams(*sem):
    return pltpu.CompilerParams(dimension_semantics=sem, vmem_limit_bytes=VMEM_LIMIT_BYTES)


def _sigmoid(x):
    return 1.0 / (1.0 + jnp.exp(-x))


def _log_sigmoid(x):
    return jnp.minimum(x, 0.0) - jnp.log1p(jnp.exp(-jnp.abs(x)))


def _split3(x):
    a = x.astype(BF16)
    r = x - a.astype(F32)
    b = r.astype(BF16)
    c = (r - b.astype(F32)).astype(BF16)
    return a, b, c


def _tri_cumsum(tri_bf, x):
    a, b, c = _split3(x)
    dot = functools.partial(jnp.dot, preferred_element_type=F32)
    return dot(tri_bf, a) + dot(tri_bf, b) + dot(tri_bf, c)


def _inproj_kernel(x_ref, gain_ref, wm_ref, ws_ref, zm_ref, zs_ref, h_scr):
    @pl.when(pl.program_id(1) == 0)
    def _():
        x = x_ref[...]
        ms = jnp.mean(x * x, axis=-1, keepdims=True)
        h = (x * lax.rsqrt(ms + EPS) * gain_ref[...]).astype(BF16)
        h_scr[...] = h
        zs_ref[...] = jnp.dot(h, ws_ref[...], preferred_element_type=F32)

    zm_ref[...] = jnp.dot(h_scr[...], wm_ref[...], preferred_element_type=F32).astype(BF16)


def _inproj(x2, gain, w_main, w_small, *, tm, tn):
    T, D = x2.shape
    NM = w_main.shape[1]
    return pl.pallas_call(
        _inproj_kernel,
        grid=(T // tm, NM // tn),
        in_specs=[
            pl.BlockSpec((tm, D), lambda i, j: (i, 0)),
            pl.BlockSpec((1, D), lambda i, j: (0, 0)),
            pl.BlockSpec((D, tn), lambda i, j: (0, j)),
            pl.BlockSpec((D, LANES), lambda i, j: (0, 0)),
        ],
        out_specs=[
            pl.BlockSpec((tm, tn), lambda i, j: (i, j)),
            pl.BlockSpec((tm, LANES), lambda i, j: (i, 0)),
        ],
        out_shape=[
            jax.ShapeDtypeStruct((T, NM), BF16),
            jax.ShapeDtypeStruct((T, LANES), F32),
        ],
        scratch_shapes=[pltpu.VMEM((tm, D), BF16)],
        compiler_params=_cparams("parallel", "arbitrary"),
        name="inproj",
    )(x2, gain, w_main, w_small)


def _gla_kernel(q_ref, k_ref, v_ref, zs_ref, gate_ref, wup_ref, bgk_ref, gain_ref,
                o_ref, state_scr, g_scr, p_scr, *, nchunks):
    C, SUB = GLA_CHUNK, GLA_SUB
    NSUB = C // SUB
    DK, DV = GLA_DK, GLA_DV

    @pl.when(pl.program_id(2) == 0)
    def _():
        state_scr[...] = jnp.zeros_like(state_scr)

    gd = zs_ref[:, ZS_GDOWN:ZS_GDOWN + GK_RANK]
    logits = jnp.dot(gd, wup_ref[...], preferred_element_type=F32,
                     precision=lax.Precision.HIGHEST) + bgk_ref[...]
    g_scr[...] = _log_sigmoid(logits) * (1.0 / GK_NORMALIZER)

    row = lax.broadcasted_iota(jnp.int32, (C, C), 0)
    col = lax.broadcasted_iota(jnp.int32, (C, C), 1)
    tri = (col <= row).astype(BF16)
    same_sub = (row // SUB) == (col // SUB)
    causal = col <= row
    sel_r = lax.broadcasted_iota(jnp.int32, (SUB * DK, C), 0)
    sel_c = lax.broadcasted_iota(jnp.int32, (SUB * DK, C), 1)
    sel = ((sel_r // DK) == (sel_c % SUB)).astype(BF16)
    scale = DK ** -0.5
    gain = gain_ref[...]
    dot = functools.partial(jnp.dot, preferred_element_type=F32)

    def chunk(c, carry):
        r0 = pl.multiple_of(c * C, C)
        rows = pl.ds(r0, C)
        q = q_ref[rows, :].astype(F32) * scale
        k = k_ref[rows, :].astype(F32)
        v = v_ref[rows, :]
        b = _tri_cumsum(tri, g_scr[rows, :])
        b_last = b[C - 1:C, :]

        state = state_scr[...]
        o = dot((q * jnp.exp(b)).astype(BF16), state.astype(BF16))

        cross = []
        for i in range(NSUB):
            qi = q[i * SUB:(i + 1) * SUB, :]
            bi = b[i * SUB:(i + 1) * SUB, :]
            if i == 0:
                cross.append(jnp.zeros((SUB, C), F32))
                continue
            bref = b[i * SUB - 1:i * SUB, :]
            q_t = (qi * jnp.exp(bi - bref)).astype(BF16)
            k_t = (k * jnp.exp(jnp.minimum(bref - b, 0.0))).astype(BF16)
            cross.append(lax.dot_general(q_t, k_t, (((1,), (1,)), ((), ())),
                                         preferred_element_type=F32))
        a_cross = jnp.concatenate(cross, axis=0)

        k3 = k.reshape(NSUB, SUB, DK)
        b3 = b.reshape(NSUB, SUB, DK)
        for j in range(SUB):
            kj = jnp.broadcast_to(k3[:, j:j + 1, :], (NSUB, SUB, DK)).reshape(C, DK)
            bj = jnp.broadcast_to(b3[:, j:j + 1, :], (NSUB, SUB, DK)).reshape(C, DK)
            p = q * kj * jnp.exp(jnp.minimum(b - bj, 0.0))
            p_scr[:, j * DK:(j + 1) * DK] = p.astype(BF16)
        a_diag = dot(p_scr[...], sel)

        a = jnp.where(causal, jnp.where(same_sub, a_diag, a_cross), 0.0)
        o = o + dot(a.astype(BF16), v)

        k_dec = (k * jnp.exp(b_last - b)).astype(BF16)
        upd = lax.dot_general(k_dec, v, (((0,), (0,)), ((), ())),
                              preferred_element_type=F32)
        decay_col = jnp.transpose(jnp.broadcast_to(jnp.exp(b_last), (DK, DK)))[:, 0:1]
        state_scr[...] = state * decay_col + upd

        ms = jnp.mean(o * o, axis=-1, keepdims=True)
        on = o * lax.rsqrt(ms + EPS) * gain
        gt = gate_ref[rows, :].astype(F32)
        o_ref[rows, :] = (on * (gt * _sigmoid(gt))).astype(o_ref.dtype)
        return carry

    lax.fori_loop(0, nchunks, chunk, 0)


def _gla(z_main, z_small, w_up, b_gk, gain, *, batch, seq, ts):
    T = z_main.shape[0]
    nsb = seq // ts
    nchunks = ts // GLA_CHUNK
    rowmap = lambda b, h, s: b * nsb + s
    kblk = ZM_GK // GLA_DK
    vblk = ZM_GV // GLA_DV
    gblk = ZM_GGATE // GLA_DV
    return pl.pallas_call(
        functools.partial(_gla_kernel, nchunks=nchunks),
        grid=(batch, GLA_HEADS, nsb),
        in_specs=[
            pl.BlockSpec((ts, GLA_DK), lambda b, h, s: (rowmap(b, h, s), h)),
            pl.BlockSpec((ts, GLA_DK), lambda b, h, s: (rowmap(b, h, s), kblk + h)),
            pl.BlockSpec((ts, GLA_DV), lambda b, h, s: (rowmap(b, h, s), vblk + h)),
            pl.BlockSpec((ts, LANES), lambda b, h, s: (rowmap(b, h, s), 0)),
            pl.BlockSpec((ts, GLA_DV), lambda b, h, s: (rowmap(b, h, s), gblk + h)),
            pl.BlockSpec((GK_RANK, GLA_DK), lambda b, h, s: (0, h)),
            pl.BlockSpec((1, GLA_DK), lambda b, h, s: (0, h)),
            pl.BlockSpec((1, GLA_DV), lambda b, h, s: (0, 0)),
        ],
        out_specs=pl.BlockSpec((ts, GLA_DV), lambda b, h, s: (rowmap(b, h, s), h)),
        out_shape=jax.ShapeDtypeStruct((T, GLA_WIDTH), BF16),
        scratch_shapes=[
            pltpu.VMEM((GLA_DK, GLA_DV), F32),
            pltpu.VMEM((ts, GLA_DK), F32),
            pltpu.VMEM((GLA_CHUNK, GLA_SUB * GLA_DK), BF16),
        ],
        compiler_params=_cparams("parallel", "parallel", "arbitrary"),
        name="gla",
    )(z_main, z_main, z_main, z_small, z_main, w_up, b_gk, gain)


def _fox_prep_kernel(q_ref, k_ref, zs_ref, bf_ref, qa_ref, ka_ref, carry_scr, *, ts):
    @pl.when(pl.program_id(1) == 0)
    def _():
        carry_scr[...] = jnp.zeros_like(carry_scr)

    row = lax.broadcasted_iota(jnp.int32, (ts, ts), 0)
    col = lax.broadcasted_iota(jnp.int32, (ts, ts), 1)
    tri = (col <= row).astype(BF16)
    log_f = _log_sigmoid(zs_ref[...] + bf_ref[...])
    c_all = _tri_cumsum(tri, log_f) + carry_scr[...]
    carry_scr[...] = c_all[ts - 1:ts, :]

    lane = lax.broadcasted_iota(jnp.int32, (ts, LANES - FOX_DH), 1)
    qf = q_ref[...].astype(F32) * (FOX_DH ** -0.5)
    kf = k_ref[...].astype(F32)
    for h in range(FOX_HEADS):
        c = c_all[:, ZS_FLOGIT + h:ZS_FLOGIT + h + 1]
        c1, c2, c3 = (t.astype(F32) for t in _split3(c))
        ones = jnp.where(lane < FOX_EXTRA, 1.0, 0.0)
        cs = jnp.where(lane == 0, c1, jnp.where(lane == 1, c2, jnp.where(lane == 2, c3, 0.0)))
        ex_q = jnp.where(lane < 3, cs, ones)
        cs_k = jnp.where(lane == 3, -c1, jnp.where(lane == 4, -c2, jnp.where(lane == 5, -c3, 0.0)))
        ex_k = jnp.where(lane < 3, ones, cs_k)
        qa = jnp.concatenate([qf[:, h * FOX_DH:(h + 1) * FOX_DH], ex_q], axis=1)
        ka = jnp.concatenate([kf[:, h * FOX_DH:(h + 1) * FOX_DH], ex_k], axis=1)
        qa_ref[:, h * LANES:(h + 1) * LANES] = qa.astype(BF16)
        ka_ref[:, h * LANES:(h + 1) * LANES] = ka.astype(BF16)


def _fox_prep(z_main, z_small, bf_row, *, batch, seq, ts):
    T = z_main.shape[0]
    nsb = seq // ts
    qblk = ZM_FQ // FOX_WIDTH
    kblk = ZM_FK // FOX_WIDTH
    aug = jax.ShapeDtypeStruct((T, FOX_HEADS * LANES), BF16)
    return pl.pallas_call(
        functools.partial(_fox_prep_kernel, ts=ts),
        grid=(batch, nsb),
        in_specs=[
            pl.BlockSpec((ts, FOX_WIDTH), lambda b, s: (b * nsb + s, qblk)),
            pl.BlockSpec((ts, FOX_WIDTH), lambda b, s: (b * nsb + s, kblk)),
            pl.BlockSpec((ts, LANES), lambda b, s: (b * nsb + s, 0)),
            pl.BlockSpec((1, LANES), lambda b, s: (0, 0)),
        ],
        out_specs=[
            pl.BlockSpec((ts, FOX_HEADS * LANES), lambda b, s: (b * nsb + s, 0)),
            pl.BlockSpec((ts, FOX_HEADS * LANES), lambda b, s: (b * nsb + s, 0)),
        ],
        out_shape=[aug, aug],
        scratch_shapes=[pltpu.VMEM((1, LANES), F32)],
        compiler_params=_cparams("parallel", "arbitrary"),
        name="fox_prep",
    )(z_main, z_main, z_small, bf_row)


def _fox_attn_kernel(q_ref, k_ref, v_ref, gate_ref, o_ref, m_scr, l_scr, acc_scr, *, tq, tk):
    qi = pl.program_id(2)
    kj = pl.program_id(3)

    @pl.when(kj == 0)
    def _():
        m_scr[...] = jnp.full_like(m_scr, -jnp.inf)
        l_scr[...] = jnp.zeros_like(l_scr)
        acc_scr[...] = jnp.zeros_like(acc_scr)

    def step(masked):
        v = v_ref[...]
        lane = lax.broadcasted_iota(jnp.int32, v.shape, 1)
        if masked:
            q_pos = qi * tq + lax.broadcasted_iota(jnp.int32, (tq, tk), 0)
            k_pos = kj * tk + lax.broadcasted_iota(jnp.int32, (tq, tk), 1)
            keep = k_pos <= q_pos
        alphas = []
        pv = None
        for hh in range(2):
            q = q_ref[:, hh * LANES:(hh + 1) * LANES]
            k = k_ref[:, hh * LANES:(hh + 1) * LANES]
            s = lax.dot_general(q, k, (((1,), (1,)), ((), ())), preferred_element_type=F32)
            if masked:
                s = jnp.where(keep, s, -jnp.inf)
            m_prev = m_scr[hh]
            m_new = jnp.maximum(m_prev, jnp.max(s, axis=-1, keepdims=True))
            alpha = jnp.exp(m_prev - m_new)
            p = jnp.exp(s - m_new)
            l_scr[hh] = alpha * l_scr[hh] + jnp.sum(p, axis=-1, keepdims=True)
            m_scr[hh] = m_new
            in_head = (lane >= hh * FOX_DH) & (lane < (hh + 1) * FOX_DH)
            v_h = jnp.where(in_head, v, jnp.zeros_like(v))
            contrib = jnp.dot(p.astype(BF16), v_h, preferred_element_type=F32)
            pv = contrib if pv is None else pv + contrib
            alphas.append(alpha)
        out_lane = lax.broadcasted_iota(jnp.int32, (tq, LANES), 1)
        alpha_sel = jnp.where(out_lane < FOX_DH, alphas[0], alphas[1])
        acc_scr[...] = acc_scr[...] * alpha_sel + pv

    @pl.when(kj < qi)
    def _():
        step(False)

    @pl.when(kj == qi)
    def _():
        step(True)
        out_lane = lax.broadcasted_iota(jnp.int32, (tq, LANES), 1)
        l_sel = jnp.where(out_lane < FOX_DH, l_scr[0], l_scr[1])
        gt = gate_ref[...].astype(F32)
        o_ref[...] = (acc_scr[...] / l_sel * (gt * _sigmoid(gt))).astype(o_ref.dtype)


def _fox_attn(q_aug, k_aug, z_main, *, batch, seq, tq):
    T = z_main.shape[0]
    tk = tq
    nq = seq // tq
    npair = FOX_HEADS // 2
    vblk = ZM_FV // LANES
    gblk = ZM_FGATE // LANES
    return pl.pallas_call(
        functools.partial(_fox_attn_kernel, tq=tq, tk=tk),
        grid=(batch, npair, nq, nq),
        in_specs=[
            pl.BlockSpec((tq, 2 * LANES), lambda b, p, i, j: (b * nq + i, p)),
            pl.BlockSpec((tk, 2 * LANES), lambda b, p, i, j: (b * nq + jnp.minimum(i, j), p)),
            pl.BlockSpec((tk, LANES), lambda b, p, i, j: (b * nq + jnp.minimum(i, j), vblk + p)),
            pl.BlockSpec((tq, LANES), lambda b, p, i, j: (b * nq + i, gblk + p)),
        ],
        out_specs=pl.BlockSpec((tq, LANES), lambda b, p, i, j: (b * nq + i, p)),
        out_shape=jax.ShapeDtypeStruct((T, FOX_WIDTH), BF16),
        scratch_shapes=[
            pltpu.VMEM((2, tq, 1), F32),
            pltpu.VMEM((2, tq, 1), F32),
            pltpu.VMEM((tq, LANES), F32),
        ],
        compiler_params=_cparams("parallel", "parallel", "parallel", "arbitrary"),
        name="fox_attn",
    )(q_aug, k_aug, z_main, z_main)


def _memkv_kernel(mem_ref, gain_ref, w_ref, o_ref):
    x = mem_ref[...]
    ms = jnp.mean(x * x, axis=-1, keepdims=True)
    h = (x * lax.rsqrt(ms + EPS) * gain_ref[...]).astype(BF16)
    o_ref[...] = jnp.dot(h, w_ref[...], preferred_element_type=F32).astype(o_ref.dtype)


def _memkv(mem2, gain, w_kv, *, tm):
    R, D = mem2.shape
    N = w_kv.shape[1]
    return pl.pallas_call(
        _memkv_kernel,
        grid=(R // tm,),
        in_specs=[
            pl.BlockSpec((tm, D), lambda i: (i, 0)),
            pl.BlockSpec((1, D), lambda i: (0, 0)),
            pl.BlockSpec((D, N), lambda i: (0, 0)),
        ],
        out_specs=pl.BlockSpec((tm, N), lambda i: (i, 0)),
        out_shape=jax.ShapeDtypeStruct((R, N), BF16),
        compiler_params=_cparams("parallel"),
        name="memkv",
    )(mem2, gain, w_kv)


def _mem_attn_kernel(q_ref, gate_ref, mk_ref, mv_ref, o_ref):
    scale = MEM_DH ** -0.5
    outs = []
    for h in range(MEM_HEADS):
        sl = slice(h * MEM_DH, (h + 1) * MEM_DH)
        s = lax.dot_general(q_ref[:, sl], mk_ref[:, sl], (((1,), (1,)), ((), ())),
                            preferred_element_type=F32) * scale
        m = jnp.max(s, axis=-1, keepdims=True)
        p = jnp.exp(s - m)
        l = jnp.sum(p, axis=-1, keepdims=True)
        p = (p / l).astype(BF16)
        outs.append(jnp.dot(p, mv_ref[:, sl], preferred_element_type=F32))
    o = jnp.concatenate(outs, axis=1)
    gt = gate_ref[...].astype(F32)
    o_ref[...] = (o * (gt * _sigmoid(gt))).astype(o_ref.dtype)


def _mem_attn(z_main, mkv, *, batch, seq, mem_len, tm):
    T = z_main.shape[0]
    nsb = seq // tm
    qblk = ZM_MQ // MEM_WIDTH
    gblk = ZM_MGATE // MEM_WIDTH
    return pl.pallas_call(
        _mem_attn_kernel,
        grid=(batch, nsb),
        in_specs=[
            pl.BlockSpec((tm, MEM_WIDTH), lambda b, s: (b * nsb + s, qblk)),
            pl.BlockSpec((tm, MEM_WIDTH), lambda b, s: (b * nsb + s, gblk)),
            pl.BlockSpec((mem_len, MEM_WIDTH), lambda b, s: (b, 0)),
            pl.BlockSpec((mem_len, MEM_WIDTH), lambda b, s: (b, 1)),
        ],
        out_specs=pl.BlockSpec((tm, MEM_WIDTH), lambda b, s: (b * nsb + s, 0)),
        out_shape=jax.ShapeDtypeStruct((T, MEM_WIDTH), BF16),
        compiler_params=_cparams("parallel", "parallel"),
        name="mem_attn",
    )(z_main, z_main, mkv, mkv)


def _merge_kernel(oa_ref, ob_ref, oc_ref, g0_ref, g1_ref, g2_ref, wa_ref, wb_ref, wc_ref, y_ref):
    dot = functools.partial(jnp.dot, preferred_element_type=F32)
    y = _sigmoid(g0_ref[...].astype(F32)) * dot(oa_ref[...], wa_ref[...])
    y = y + _sigmoid(g1_ref[...].astype(F32)) * dot(ob_ref[...], wb_ref[...])
    y = y + _sigmoid(g2_ref[...].astype(F32)) * dot(oc_ref[...], wc_ref[...])
    y_ref[...] = y.astype(y_ref.dtype)


def _merge(o_a, o_b, o_c, z_main, w_branch, *, tm, tn):
    T = o_a.shape[0]
    D = w_branch.shape[1]
    gbase = ZM_MERGE // tn
    gstep = D // tn
    ra = GLA_WIDTH // FOX_WIDTH
    return pl.pallas_call(
        _merge_kernel,
        grid=(T // tm, D // tn),
        in_specs=[
            pl.BlockSpec((tm, GLA_WIDTH), lambda i, j: (i, 0)),
            pl.BlockSpec((tm, FOX_WIDTH), lambda i, j: (i, 0)),
            pl.BlockSpec((tm, MEM_WIDTH), lambda i, j: (i, 0)),
            pl.BlockSpec((tm, tn), lambda i, j: (i, gbase + j)),
            pl.BlockSpec((tm, tn), lambda i, j: (i, gbase + gstep + j)),
            pl.BlockSpec((tm, tn), lambda i, j: (i, gbase + 2 * gstep + j)),
            pl.BlockSpec((GLA_WIDTH, tn), lambda i, j: (0, j)),
            pl.BlockSpec((FOX_WIDTH, tn), lambda i, j: (ra, j)),
            pl.BlockSpec((MEM_WIDTH, tn), lambda i, j: (ra + 1, j)),
        ],
        out_specs=pl.BlockSpec((tm, tn), lambda i, j: (i, j)),
        out_shape=jax.ShapeDtypeStruct((T, D), BF16),
        compiler_params=_cparams("parallel", "arbitrary"),
        name="merge",
    )(o_a, o_b, o_c, z_main, z_main, z_main, w_branch, w_branch, w_branch)


def _outproj_kernel(x_ref, y_ref, w_ref, gain_ref, o_ref, *, final_norm):
    x = x_ref[...] + jnp.dot(y_ref[...], w_ref[...], preferred_element_type=F32)
    if final_norm:
        ms = jnp.mean(x * x, axis=-1, keepdims=True)
        x = x * lax.rsqrt(ms + EPS) * gain_ref[...]
    o_ref[...] = x


def _outproj(x2, y, w_out, final_gain, *, tm, final_norm):
    T, D = x2.shape
    return pl.pallas_call(
        functools.partial(_outproj_kernel, final_norm=final_norm),
        grid=(T // tm,),
        in_specs=[
            pl.BlockSpec((tm, D), lambda i: (i, 0)),
            pl.BlockSpec((tm, D), lambda i: (i, 0)),
            pl.BlockSpec((D, D), lambda i: (0, 0)),
            pl.BlockSpec((1, D), lambda i: (0, 0)),
        ],
        out_specs=pl.BlockSpec((tm, D), lambda i: (i, 0)),
        out_shape=jax.ShapeDtypeStruct((T, D), F32),
        compiler_params=_cparams("parallel"),
        name="outproj",
    )(x2, y, w_out, final_gain)


def _pack_w_in(w):
    d = w.shape[0]
    o_gdown = 2 * GLA_KWIDTH + 2 * GLA_WIDTH
    o_fq = o_gdown + GK_RANK
    o_flogit = o_fq + 3 * FOX_WIDTH
    o_fgate = o_flogit + FOX_HEADS
    w_main = jnp.concatenate([w[:, :o_gdown], w[:, o_fq:o_flogit], w[:, o_fgate:]], axis=1)
    pad = jnp.zeros((d, LANES - GK_RANK - FOX_HEADS), w.dtype)
    w_small = jnp.concatenate([w[:, o_gdown:o_fq], w[:, o_flogit:o_fgate], pad], axis=1)
    return w_main.astype(BF16), w_small.astype(BF16)


def _trunk(x, mem, norm_gain, w_in, w_gk_up, b_gk, gla_norm_gain, b_f, mem_norm_gain,
           w_mem_kv, w_branch, w_out, final_gain, *, tiles):
    B, S, D = x.shape
    M = mem.shape[1]
    depth = w_in.shape[0]
    x2 = x.reshape(B * S, D)
    mem2 = mem.reshape(B * M, D)
    for l in range(depth):
        w_main, w_small = _pack_w_in(w_in[l])
        z_main, z_small = _inproj(x2, norm_gain[l][None, :], w_main, w_small,
                                  tm=tiles["in_tm"], tn=tiles["in_tn"])
        o_a = _gla(z_main, z_small, w_gk_up[l], b_gk[l][None, :], gla_norm_gain[l][None, :],
                   batch=B, seq=S, ts=tiles["gla_ts"])
        bf_row = jnp.zeros((1, LANES), F32).at[0, ZS_FLOGIT:ZS_FLOGIT + FOX_HEADS].set(b_f[l])
        q_aug, k_aug = _fox_prep(z_main, z_small, bf_row, batch=B, seq=S, ts=tiles["prep_ts"])
        o_b = _fox_attn(q_aug, k_aug, z_main, batch=B, seq=S, tq=tiles["fox_tq"])
        mkv = _memkv(mem2, mem_norm_gain[l][None, :], w_mem_kv[l].astype(BF16), tm=tiles["memkv_tm"])
        o_c = _mem_attn(z_main, mkv, batch=B, seq=S, mem_len=M, tm=tiles["mem_tm"])
        y = _merge(o_a, o_b, o_c, z_main, w_branch[l].astype(BF16),
                   tm=tiles["merge_tm"], tn=tiles["merge_tn"])
        x2 = _outproj(x2, y, w_out[l].astype(BF16), final_gain[None, :],
                      tm=tiles["out_tm"], final_norm=(l == depth - 1))
    return x2.reshape(B, S, D)


_TILES = dict(in_tm=1024, in_tn=1024, gla_ts=512, prep_ts=512, fox_tq=512,
              memkv_tm=256, mem_tm=512, merge_tm=1024, merge_tn=512, out_tm=512)


def kernel(x, mem, norm_gain, w_in, w_gk_up, b_gk, gla_norm_gain, b_f, mem_norm_gain,
           w_mem_kv, w_branch, w_out, final_gain):
    return _trunk(x, mem, norm_gain, w_in, w_gk_up, b_gk, gla_norm_gain, b_f, mem_norm_gain,
                  w_mem_kv, w_branch, w_out, final_gain, tiles=_TILES)
```

```python
import functools

import jax
import jax.numpy as jnp
from jax import lax
from jax.experimental import pallas as pl
from jax.experimental.pallas import tpu as pltpu

F32 = jnp.float32
BF16 = jnp.bfloat16

EPS = 1e-6

GLA_HEADS, GLA_DK, GLA_DV = 4, 128, 256
GLA_KWIDTH = GLA_HEADS * GLA_DK
GLA_WIDTH = GLA_HEADS * GLA_DV
GK_RANK = 16
GK_NORMALIZER = 16.0
FOX_HEADS, FOX_DH = 8, 64
FOX_WIDTH = FOX_HEADS * FOX_DH
MEM_HEADS, MEM_DH = 4, 128
MEM_WIDTH = MEM_HEADS * MEM_DH
N_BRANCH = 3

LANES = 128
VMEM_LIMIT_BYTES = 56 * 1024 * 1024

ZM_GQ = 0
ZM_GK = ZM_GQ + GLA_KWIDTH
ZM_GV = ZM_GK + GLA_KWIDTH
ZM_GGATE = ZM_GV + GLA_WIDTH
ZM_FQ = ZM_GGATE + GLA_WIDTH
ZM_FK = ZM_FQ + FOX_WIDTH
ZM_FV = ZM_FK + FOX_WIDTH
ZM_FGATE = ZM_FV + FOX_WIDTH
ZM_MQ = ZM_FGATE + FOX_WIDTH
ZM_MGATE = ZM_MQ + MEM_WIDTH
ZM_MERGE = ZM_MGATE + MEM_WIDTH
GD_COPIES = 6
ZS_FLOGIT = GD_COPIES * GK_RANK

GLA_CHUNK = 64
GLA_SUB = 8
FOX_EXTRA = 6
LOG2E = 1.4426950408889634


def _cparams(*sem):
    return pltpu.CompilerParams(dimension_semantics=sem, vmem_limit_bytes=VMEM_LIMIT_BYTES)


def _sigmoid(x):
    return 1.0 / (1.0 + jnp.exp(-x))


def _log_sigmoid(x):
    return jnp.minimum(x, 0.0) - jnp.log(1.0 + jnp.exp(-jnp.abs(x)))


def _split3(x):
    a = x.astype(BF16)
    r = x - a.astype(F32)
    b = r.astype(BF16)
    c = (r - b.astype(F32)).astype(BF16)
    return a, b, c


def _tri_cumsum(tri_bf, x):
    a, b, c = _split3(x)
    dot = functools.partial(jnp.dot, preferred_element_type=F32)
    return dot(tri_bf, a) + dot(tri_bf, b) + dot(tri_bf, c)


def _inproj_kernel(x_ref, gain_ref, wm_ref, ws_ref, zm_ref, zs_ref, h_scr):
    @pl.when(pl.program_id(1) == 0)
    def _():
        x = x_ref[...]
        ms = jnp.mean(x * x, axis=-1, keepdims=True)
        h = (x * lax.rsqrt(ms + EPS) * gain_ref[...]).astype(BF16)
        h_scr[...] = h
        zs_ref[...] = jnp.dot(h, ws_ref[...], preferred_element_type=F32)

    zm_ref[...] = jnp.dot(h_scr[...], wm_ref[...], preferred_element_type=F32).astype(BF16)


def _inproj(x2, gain, w_main, w_small, *, tm, tn):
    T, D = x2.shape
    NM = w_main.shape[1]
    return pl.pallas_call(
        _inproj_kernel,
        grid=(T // tm, NM // tn),
        in_specs=[
            pl.BlockSpec((tm, D), lambda i, j: (i, 0)),
            pl.BlockSpec((1, D), lambda i, j: (0, 0)),
            pl.BlockSpec((D, tn), lambda i, j: (0, j)),
            pl.BlockSpec((D, LANES), lambda i, j: (0, 0)),
        ],
        out_specs=[
            pl.BlockSpec((tm, tn), lambda i, j: (i, j)),
            pl.BlockSpec((tm, LANES), lambda i, j: (i, 0)),
        ],
        out_shape=[
            jax.ShapeDtypeStruct((T, NM), BF16),
            jax.ShapeDtypeStruct((T, LANES), F32),
        ],
        scratch_shapes=[pltpu.VMEM((tm, D), BF16)],
        compiler_params=_cparams("parallel", "arbitrary"),
        name="inproj",
    )(x2, gain, w_main, w_small)


def _gla_kernel(q_ref, k_ref, v_ref, zs_ref, gate_ref, wup_ref, bgk_ref, gain_ref,
                o_ref, state_scr, b_scr, p_scr, *, nchunks, unroll):
    C, SUB = GLA_CHUNK, GLA_SUB
    NSUB = C // SUB
    NPAIR = SUB // 2
    DK, DV = GLA_DK, GLA_DV
    dot = functools.partial(jnp.dot, preferred_element_type=F32)

    @pl.when(pl.program_id(2) == 0)
    def _():
        state_scr[...] = jnp.zeros_like(state_scr)

    row = lax.broadcasted_iota(jnp.int32, (C, C), 0)
    col = lax.broadcasted_iota(jnp.int32, (C, C), 1)
    tri = (col <= row).astype(BF16)
    same_sub = (row // SUB) == (col // SUB)
    causal = col <= row

    z1, z2, z3 = _split3(zs_ref[...])
    zlane = lax.broadcasted_iota(jnp.int32, z1.shape, 1)
    lhs = jnp.where(zlane < 3 * GK_RANK, z1, jnp.where(zlane < 5 * GK_RANK, z2, z3))
    w1, w2, w3 = _split3(wup_ref[...])
    rhs = jnp.concatenate([w1, w2, w3, w1, w2, w1,
                           jnp.zeros((LANES - GD_COPIES * GK_RANK, DK), BF16)], axis=0)
    logits = dot(lhs, rhs) + bgk_ref[...]
    g = _log_sigmoid(logits) * (LOG2E / GK_NORMALIZER)
    for c in range(nchunks):
        b_scr[c * C:(c + 1) * C, :] = _tri_cumsum(tri, g[c * C:(c + 1) * C, :])

    cs_r = lax.broadcasted_iota(jnp.int32, (2 * DK, LANES), 0)
    cs_c = lax.broadcasted_iota(jnp.int32, (2 * DK, LANES), 1)
    chansum = ((cs_r // DK) == (cs_c % 2)).astype(BF16)
    pair_of_lane = (lax.broadcasted_iota(jnp.int32, (C, LANES), 1) % SUB) // 2
    scale = DK ** -0.5
    gain = gain_ref[...]

    def chunk(c, p_slot, state):
        r0 = pl.multiple_of(c * C, C)
        rows = pl.ds(r0, C)
        q = q_ref[rows, :].astype(F32) * scale
        k = k_ref[rows, :].astype(F32)
        v = v_ref[rows, :]
        b = b_scr[rows, :]
        b_last = b[C - 1:C, :]

        o = dot((q * jnp.exp2(b)).astype(BF16), state.astype(BF16))

        cross = [jnp.zeros((SUB, C), F32)]
        for i in range(1, NSUB):
            n = i * SUB
            bref = b[n - 1:n, :]
            q_t = (q[n:n + SUB, :] * jnp.exp2(b[n:n + SUB, :] - bref)).astype(BF16)
            k_t = (k[:n, :] * jnp.exp2(bref - b[:n, :])).astype(BF16)
            k_t = jnp.concatenate([k_t, jnp.zeros((C - n, DK), BF16)], axis=0)
            cross.append(lax.dot_general(q_t, k_t, (((1,), (1,)), ((), ())),
                                         preferred_element_type=F32))
        a_cross = jnp.concatenate(cross, axis=0)

        k3 = k.reshape(NSUB, SUB, DK)
        b3 = b.reshape(NSUB, SUB, DK)
        for j in range(SUB):
            kj = jnp.broadcast_to(k3[:, j:j + 1, :], (NSUB, SUB, DK)).reshape(C, DK)
            bj = jnp.broadcast_to(b3[:, j:j + 1, :], (NSUB, SUB, DK)).reshape(C, DK)
            p = q * kj * jnp.exp2(jnp.minimum(b - bj, 0.0))
            p_slot[(j // 2) * C:(j // 2 + 1) * C, (j % 2) * DK:(j % 2 + 1) * DK] = p.astype(BF16)
        d = dot(p_slot[...], chansum)
        a_diag = d[0:C, :]
        for t in range(1, NPAIR):
            a_diag = jnp.where(pair_of_lane == t, d[t * C:(t + 1) * C, :], a_diag)

        a = jnp.where(causal, jnp.where(same_sub, a_diag[:, :C], a_cross), 0.0)
        o = o + dot(a.astype(BF16), v)

        k_dec = (k * jnp.exp2(b_last - b)).astype(BF16)
        upd = lax.dot_general(k_dec, v, (((0,), (0,)), ((), ())),
                              preferred_element_type=F32)
        decay_col = jnp.transpose(jnp.broadcast_to(jnp.exp2(b_last), (DK, DK)))[:, 0:1]
        new_state = state * decay_col + upd

        ms = jnp.mean(o * o, axis=-1, keepdims=True)
        on = o * lax.rsqrt(ms + EPS) * gain
        gt = gate_ref[rows, :].astype(F32)
        o_ref[rows, :] = (on * (gt * _sigmoid(gt))).astype(o_ref.dtype)
        return new_state

    def group(gi, carry):
        state = state_scr[...]
        for u in range(unroll):
            state = chunk(gi * unroll + u, p_scr.at[u], state)
        state_scr[...] = state
        return carry

    lax.fori_loop(0, nchunks // unroll, group, 0)


def _gla(z_main, z_small, w_up, b_gk, gain, *, batch, seq, ts, unroll):
    T = z_main.shape[0]
    nsb = seq // ts
    nchunks = ts // GLA_CHUNK
    rowmap = lambda b, h, s: b * nsb + s
    kblk = ZM_GK // GLA_DK
    vblk = ZM_GV // GLA_DV
    gblk = ZM_GGATE // GLA_DV
    return pl.pallas_call(
        functools.partial(_gla_kernel, nchunks=nchunks, unroll=unroll),
        grid=(batch, GLA_HEADS, nsb),
        in_specs=[
            pl.BlockSpec((ts, GLA_DK), lambda b, h, s: (rowmap(b, h, s), h)),
            pl.BlockSpec((ts, GLA_DK), lambda b, h, s: (rowmap(b, h, s), kblk + h)),
            pl.BlockSpec((ts, GLA_DV), lambda b, h, s: (rowmap(b, h, s), vblk + h)),
            pl.BlockSpec((ts, LANES), lambda b, h, s: (rowmap(b, h, s), 0)),
            pl.BlockSpec((ts, GLA_DV), lambda b, h, s: (rowmap(b, h, s), gblk + h)),
            pl.BlockSpec((GK_RANK, GLA_DK), lambda b, h, s: (0, h)),
            pl.BlockSpec((1, GLA_DK), lambda b, h, s: (0, h)),
            pl.BlockSpec((1, GLA_DV), lambda b, h, s: (0, 0)),
        ],
        out_specs=pl.BlockSpec((ts, GLA_DV), lambda b, h, s: (rowmap(b, h, s), h)),
        out_shape=jax.ShapeDtypeStruct((T, GLA_WIDTH), BF16),
        scratch_shapes=[
            pltpu.VMEM((GLA_DK, GLA_DV), F32),
            pltpu.VMEM((ts, GLA_DK), F32),
            pltpu.VMEM((unroll, GLA_SUB // 2 * GLA_CHUNK, 2 * GLA_DK), BF16),
        ],
        compiler_params=_cparams("parallel", "parallel", "arbitrary"),
        name="gla",
    )(z_main, z_main, z_main, z_small, z_main, w_up, b_gk, gain)


def _fox_prep_kernel(q_ref, k_ref, v_ref, zs_ref, bf_ref, qa_ref, ka_ref, vt_ref, carry_scr, *, ts):
    @pl.when(pl.program_id(1) == 0)
    def _():
        carry_scr[...] = jnp.zeros_like(carry_scr)

    row = lax.broadcasted_iota(jnp.int32, (ts, ts), 0)
    col = lax.broadcasted_iota(jnp.int32, (ts, ts), 1)
    tri = (col <= row).astype(BF16)
    log_f = _log_sigmoid(zs_ref[...] + bf_ref[...])
    c_all = _tri_cumsum(tri, log_f) + carry_scr[...]
    carry_scr[...] = c_all[ts - 1:ts, :]

    vt_ref[...] = jnp.transpose(v_ref[...].astype(F32)).astype(BF16)

    lane = lax.broadcasted_iota(jnp.int32, (ts, LANES - FOX_DH), 1)
    qf = q_ref[...].astype(F32) * (FOX_DH ** -0.5 * LOG2E)
    kf = k_ref[...].astype(F32)
    for h in range(FOX_HEADS):
        c = c_all[:, ZS_FLOGIT + h:ZS_FLOGIT + h + 1] * LOG2E
        c1, c2, c3 = (t.astype(F32) for t in _split3(c))
        ones = jnp.where(lane < FOX_EXTRA, 1.0, 0.0)
        cs = jnp.where(lane == 0, c1, jnp.where(lane == 1, c2, jnp.where(lane == 2, c3, 0.0)))
        ex_q = jnp.where(lane < 3, cs, ones)
        cs_k = jnp.where(lane == 3, -c1, jnp.where(lane == 4, -c2, jnp.where(lane == 5, -c3, 0.0)))
        ex_k = jnp.where(lane < 3, ones, cs_k)
        qa = jnp.concatenate([qf[:, h * FOX_DH:(h + 1) * FOX_DH], ex_q], axis=1)
        ka = jnp.concatenate([kf[:, h * FOX_DH:(h + 1) * FOX_DH], ex_k], axis=1)
        qa_ref[:, h * LANES:(h + 1) * LANES] = qa.astype(BF16)
        ka_ref[:, h * LANES:(h + 1) * LANES] = ka.astype(BF16)


def _fox_prep(z_main, z_small, bf_row, *, batch, seq, ts):
    T = z_main.shape[0]
    nsb = seq // ts
    qblk = ZM_FQ // FOX_WIDTH
    kblk = ZM_FK // FOX_WIDTH
    vblk = ZM_FV // FOX_WIDTH
    aug = jax.ShapeDtypeStruct((T, FOX_HEADS * LANES), BF16)
    return pl.pallas_call(
        functools.partial(_fox_prep_kernel, ts=ts),
        grid=(batch, nsb),
        in_specs=[
            pl.BlockSpec((ts, FOX_WIDTH), lambda b, s: (b * nsb + s, qblk)),
            pl.BlockSpec((ts, FOX_WIDTH), lambda b, s: (b * nsb + s, kblk)),
            pl.BlockSpec((ts, FOX_WIDTH), lambda b, s: (b * nsb + s, vblk)),
            pl.BlockSpec((ts, LANES), lambda b, s: (b * nsb + s, 0)),
            pl.BlockSpec((1, LANES), lambda b, s: (0, 0)),
        ],
        out_specs=[
            pl.BlockSpec((ts, FOX_HEADS * LANES), lambda b, s: (b * nsb + s, 0)),
            pl.BlockSpec((ts, FOX_HEADS * LANES), lambda b, s: (b * nsb + s, 0)),
            pl.BlockSpec((FOX_WIDTH, ts), lambda b, s: (b, s)),
        ],
        out_shape=[aug, aug, jax.ShapeDtypeStruct((batch * FOX_WIDTH, seq), BF16)],
        scratch_shapes=[pltpu.VMEM((1, LANES), F32)],
        compiler_params=_cparams("parallel", "arbitrary"),
        name="fox_prep",
    )(z_main, z_main, z_main, z_small, bf_row)


def _fox_attn_kernel(q_ref, k_ref, vt_ref, gate_ref, o_ref, m_scr, l_scr, acc_scr, *, tq, tk):
    qi = pl.program_id(2)
    m_scr[...] = jnp.full_like(m_scr, -jnp.inf)
    l_scr[...] = jnp.zeros_like(l_scr)
    acc_scr[...] = jnp.zeros_like(acc_scr)

    def chunk(j, masked):
        k0 = pl.multiple_of(j * tk, tk)
        sts = []
        for hh in range(2):
            q = q_ref[:, hh * LANES:(hh + 1) * LANES]
            k = k_ref[pl.ds(k0, tk), hh * LANES:(hh + 1) * LANES]
            sts.append(lax.dot_general(k, q, (((1,), (1,)), ((), ())),
                                       preferred_element_type=F32))
        for hh in range(2):
            st = sts[hh]
            if masked:
                k_pos = lax.broadcasted_iota(jnp.int32, (tk, tq), 0)
                q_pos = lax.broadcasted_iota(jnp.int32, (tk, tq), 1)
                st = jnp.where(k_pos <= q_pos, st, -jnp.inf)
            m_prev = m_scr[hh]
            m_new = jnp.maximum(m_prev, jnp.max(st, axis=0, keepdims=True))
            alpha = jnp.exp2(m_prev - m_new)
            p = jnp.exp2(st - m_new)
            l_scr[hh] = alpha * l_scr[hh] + jnp.sum(p, axis=0, keepdims=True)
            m_scr[hh] = m_new
            vt = vt_ref[hh * FOX_DH:(hh + 1) * FOX_DH, pl.ds(k0, tk)]
            pv = jnp.dot(vt, p.astype(BF16), preferred_element_type=F32)
            acc_scr[hh] = acc_scr[hh] * alpha + pv

    def body(j, carry):
        chunk(j, False)
        return carry

    lax.fori_loop(0, qi, body, 0)
    chunk(qi, True)

    ot = jnp.concatenate([acc_scr[0] / l_scr[0], acc_scr[1] / l_scr[1]], axis=0)
    gt = gate_ref[...].astype(F32)
    o_ref[...] = (jnp.transpose(ot) * (gt * _sigmoid(gt))).astype(o_ref.dtype)


def _fox_attn(q_aug, k_aug, v_t, z_main, *, batch, seq, tq):
    T = z_main.shape[0]
    tk = tq
    nq = seq // tq
    npair = FOX_HEADS // 2
    gblk = ZM_FGATE // LANES
    return pl.pallas_call(
        functools.partial(_fox_attn_kernel, tq=tq, tk=tk),
        grid=(batch, npair, nq),
        in_specs=[
            pl.BlockSpec((tq, 2 * LANES), lambda b, p, i: (b * nq + i, p)),
            pl.BlockSpec((seq, 2 * LANES), lambda b, p, i: (b, p)),
            pl.BlockSpec((2 * FOX_DH, seq), lambda b, p, i: (b * npair + p, 0)),
            pl.BlockSpec((tq, LANES), lambda b, p, i: (b * nq + i, gblk + p)),
        ],
        out_specs=pl.BlockSpec((tq, LANES), lambda b, p, i: (b * nq + i, p)),
        out_shape=jax.ShapeDtypeStruct((T, FOX_WIDTH), BF16),
        scratch_shapes=[
            pltpu.VMEM((2, 1, tq), F32),
            pltpu.VMEM((2, 1, tq), F32),
            pltpu.VMEM((2, FOX_DH, tq), F32),
        ],
        compiler_params=_cparams("parallel", "parallel", "arbitrary"),
        name="fox_attn",
    )(q_aug, k_aug, v_t, z_main)


def _memkv_kernel(mem_ref, gain_ref, w_ref, o_ref):
    x = mem_ref[...]
    ms = jnp.mean(x * x, axis=-1, keepdims=True)
    h = (x * lax.rsqrt(ms + EPS) * gain_ref[...]).astype(BF16)
    o_ref[...] = jnp.dot(h, w_ref[...], preferred_element_type=F32).astype(o_ref.dtype)


def _memkv(mem2, gain, w_kv, *, tm):
    R, D = mem2.shape
    N = w_kv.shape[1]
    return pl.pallas_call(
        _memkv_kernel,
        grid=(R // tm,),
        in_specs=[
            pl.BlockSpec((tm, D), lambda i: (i, 0)),
            pl.BlockSpec((1, D), lambda i: (0, 0)),
            pl.BlockSpec((D, N), lambda i: (0, 0)),
        ],
        out_specs=pl.BlockSpec((tm, N), lambda i: (i, 0)),
        out_shape=jax.ShapeDtypeStruct((R, N), BF16),
        compiler_params=_cparams("parallel"),
        name="memkv",
    )(mem2, gain, w_kv)


def _mem_attn_kernel(q_ref, gate_ref, mk_ref, mv_ref, o_ref):
    scale = MEM_DH ** -0.5
    outs = []
    for h in range(MEM_HEADS):
        sl = slice(h * MEM_DH, (h + 1) * MEM_DH)
        s = lax.dot_general(q_ref[:, sl], mk_ref[:, sl], (((1,), (1,)), ((), ())),
                            preferred_element_type=F32) * scale
        m = jnp.max(s, axis=-1, keepdims=True)
        p = jnp.exp(s - m)
        l = jnp.sum(p, axis=-1, keepdims=True)
        p = (p / l).astype(BF16)
        outs.append(jnp.dot(p, mv_ref[:, sl], preferred_element_type=F32))
    o = jnp.concatenate(outs, axis=1)
    gt = gate_ref[...].astype(F32)
    o_ref[...] = (o * (gt * _sigmoid(gt))).astype(o_ref.dtype)


def _mem_attn(z_main, mkv, *, batch, seq, mem_len, tm):
    T = z_main.shape[0]
    nsb = seq // tm
    qblk = ZM_MQ // MEM_WIDTH
    gblk = ZM_MGATE // MEM_WIDTH
    return pl.pallas_call(
        _mem_attn_kernel,
        grid=(batch, nsb),
        in_specs=[
            pl.BlockSpec((tm, MEM_WIDTH), lambda b, s: (b * nsb + s, qblk)),
            pl.BlockSpec((tm, MEM_WIDTH), lambda b, s: (b * nsb + s, gblk)),
            pl.BlockSpec((mem_len, MEM_WIDTH), lambda b, s: (b, 0)),
            pl.BlockSpec((mem_len, MEM_WIDTH), lambda b, s: (b, 1)),
        ],
        out_specs=pl.BlockSpec((tm, MEM_WIDTH), lambda b, s: (b * nsb + s, 0)),
        out_shape=jax.ShapeDtypeStruct((T, MEM_WIDTH), BF16),
        compiler_params=_cparams("parallel", "parallel"),
        name="mem_attn",
    )(z_main, z_main, mkv, mkv)


def _merge_kernel(oa_ref, ob_ref, oc_ref, g0_ref, g1_ref, g2_ref, wa_ref, wb_ref, wc_ref, y_ref):
    dot = functools.partial(jnp.dot, preferred_element_type=F32)
    y = _sigmoid(g0_ref[...].astype(F32)) * dot(oa_ref[...], wa_ref[...])
    y = y + _sigmoid(g1_ref[...].astype(F32)) * dot(ob_ref[...], wb_ref[...])
    y = y + _sigmoid(g2_ref[...].astype(F32)) * dot(oc_ref[...], wc_ref[...])
    y_ref[...] = y.astype(y_ref.dtype)


def _merge(o_a, o_b, o_c, z_main, w_branch, *, tm, tn):
    T = o_a.shape[0]
    D = w_branch.shape[1]
    gbase = ZM_MERGE // tn
    gstep = D // tn
    ra = GLA_WIDTH // FOX_WIDTH
    return pl.pallas_call(
        _merge_kernel,
        grid=(T // tm, D // tn),
        in_specs=[
            pl.BlockSpec((tm, GLA_WIDTH), lambda i, j: (i, 0)),
            pl.BlockSpec((tm, FOX_WIDTH), lambda i, j: (i, 0)),
            pl.BlockSpec((tm, MEM_WIDTH), lambda i, j: (i, 0)),
            pl.BlockSpec((tm, tn), lambda i, j: (i, gbase + j)),
            pl.BlockSpec((tm, tn), lambda i, j: (i, gbase + gstep + j)),
            pl.BlockSpec((tm, tn), lambda i, j: (i, gbase + 2 * gstep + j)),
            pl.BlockSpec((GLA_WIDTH, tn), lambda i, j: (0, j)),
            pl.BlockSpec((FOX_WIDTH, tn), lambda i, j: (ra, j)),
            pl.BlockSpec((MEM_WIDTH, tn), lambda i, j: (ra + 1, j)),
        ],
        out_specs=pl.BlockSpec((tm, tn), lambda i, j: (i, j)),
        out_shape=jax.ShapeDtypeStruct((T, D), BF16),
        compiler_params=_cparams("parallel", "arbitrary"),
        name="merge",
    )(o_a, o_b, o_c, z_main, z_main, z_main, w_branch, w_branch, w_branch)


def _outproj_kernel(x_ref, y_ref, w_ref, gain_ref, o_ref, *, final_norm):
    x = x_ref[...] + jnp.dot(y_ref[...], w_ref[...], preferred_element_type=F32)
    if final_norm:
        ms = jnp.mean(x * x, axis=-1, keepdims=True)
        x = x * lax.rsqrt(ms + EPS) * gain_ref[...]
    o_ref[...] = x


def _outproj(x2, y, w_out, final_gain, *, tm, final_norm):
    T, D = x2.shape
    return pl.pallas_call(
        functools.partial(_outproj_kernel, final_norm=final_norm),
        grid=(T // tm,),
        in_specs=[
            pl.BlockSpec((tm, D), lambda i: (i, 0)),
            pl.BlockSpec((tm, D), lambda i: (i, 0)),
            pl.BlockSpec((D, D), lambda i: (0, 0)),
            pl.BlockSpec((1, D), lambda i: (0, 0)),
        ],
        out_specs=pl.BlockSpec((tm, D), lambda i: (i, 0)),
        out_shape=jax.ShapeDtypeStruct((T, D), F32),
        compiler_params=_cparams("parallel"),
        name="outproj",
    )(x2, y, w_out, final_gain)


def _pack_w_in(w):
    d = w.shape[0]
    o_gdown = 2 * GLA_KWIDTH + 2 * GLA_WIDTH
    o_fq = o_gdown + GK_RANK
    o_flogit = o_fq + 3 * FOX_WIDTH
    o_fgate = o_flogit + FOX_HEADS
    w_main = jnp.concatenate([w[:, :o_gdown], w[:, o_fq:o_flogit], w[:, o_fgate:]], axis=1)
    pad = jnp.zeros((d, LANES - GD_COPIES * GK_RANK - FOX_HEADS), w.dtype)
    w_small = jnp.concatenate([w[:, o_gdown:o_fq]] * GD_COPIES + [w[:, o_flogit:o_fgate], pad], axis=1)
    return w_main.astype(BF16), w_small.astype(BF16)


def _trunk(x, mem, norm_gain, w_in, w_gk_up, b_gk, gla_norm_gain, b_f, mem_norm_gain,
           w_mem_kv, w_branch, w_out, final_gain, *, tiles):
    B, S, D = x.shape
    M = mem.shape[1]
    depth = w_in.shape[0]
    x2 = x.reshape(B * S, D)
    mem2 = mem.reshape(B * M, D)
    for l in range(depth):
        w_main, w_small = _pack_w_in(w_in[l])
        z_main, z_small = _inproj(x2, norm_gain[l][None, :], w_main, w_small,
                                  tm=tiles["in_tm"], tn=tiles["in_tn"])
        o_a = _gla(z_main, z_small, w_gk_up[l], b_gk[l][None, :], gla_norm_gain[l][None, :],
                   batch=B, seq=S, ts=tiles["gla_ts"], unroll=tiles["gla_unroll"])
        bf_row = jnp.zeros((1, LANES), F32).at[0, ZS_FLOGIT:ZS_FLOGIT + FOX_HEADS].set(b_f[l])
        q_aug, k_aug, v_t = _fox_prep(z_main, z_small, bf_row, batch=B, seq=S, ts=tiles["prep_ts"])
        o_b = _fox_attn(q_aug, k_aug, v_t, z_main, batch=B, seq=S, tq=tiles["fox_tq"])
        mkv = _memkv(mem2, mem_norm_gain[l][None, :], w_mem_kv[l].astype(BF16), tm=tiles["memkv_tm"])
        o_c = _mem_attn(z_main, mkv, batch=B, seq=S, mem_len=M, tm=tiles["mem_tm"])
        y = _merge(o_a, o_b, o_c, z_main, w_branch[l].astype(BF16),
                   tm=tiles["merge_tm"], tn=tiles["merge_tn"])
        x2 = _outproj(x2, y, w_out[l].astype(BF16), final_gain[None, :],
                      tm=tiles["out_tm"], final_norm=(l == depth - 1))
    return x2.reshape(B, S, D)


_TILES = dict(in_tm=1024, in_tn=1024, gla_ts=512, gla_unroll=8, prep_ts=512, fox_tq=512,
              memkv_tm=256, mem_tm=512, merge_tm=1024, merge_tn=512, out_tm=512)


def kernel(x, mem, norm_gain, w_in, w_gk_up, b_gk, gla_norm_gain, b_f, mem_norm_gain,
           w_mem_kv, w_branch, w_out, final_gain):
    return _trunk(x, mem, norm_gain, w_in, w_gk_up, b_gk, gla_norm_gain, b_f, mem_norm_gain,
                  w_mem_kv, w_branch, w_out, final_gain, tiles=_TILES)
```

```python
import functools

import jax
import jax.numpy as jnp
from jax import lax
from jax.experimental import pallas as pl
from jax.experimental.pallas import tpu as pltpu

F32 = jnp.float32
BF16 = jnp.bfloat16

EPS = 1e-6

GLA_HEADS, GLA_DK, GLA_DV = 4, 128, 256
GLA_KWIDTH = GLA_HEADS * GLA_DK
GLA_WIDTH = GLA_HEADS * GLA_DV
GK_RANK = 16
GK_NORMALIZER = 16.0
FOX_HEADS, FOX_DH = 8, 64
FOX_WIDTH = FOX_HEADS * FOX_DH
MEM_HEADS, MEM_DH = 4, 128
MEM_WIDTH = MEM_HEADS * MEM_DH
N_BRANCH = 3

LANES = 128
VMEM_LIMIT_BYTES = 56 * 1024 * 1024

ZM_GQ = 0
ZM_GK = ZM_GQ + GLA_KWIDTH
ZM_GV = ZM_GK + GLA_KWIDTH
ZM_GGATE = ZM_GV + GLA_WIDTH
ZM_FQ = ZM_GGATE + GLA_WIDTH
ZM_FK = ZM_FQ + FOX_WIDTH
ZM_FV = ZM_FK + FOX_WIDTH
ZM_FGATE = ZM_FV + FOX_WIDTH
ZM_MQ = ZM_FGATE + FOX_WIDTH
ZM_MGATE = ZM_MQ + MEM_WIDTH
ZM_MERGE = ZM_MGATE + MEM_WIDTH
GD_COPIES = 6
ZS_FLOGIT = GD_COPIES * GK_RANK

GLA_CHUNK = 64
GLA_SUB = 8
FOX_EXTRA = 6
LOG2E = 1.4426950408889634


def _cparams(*sem):
    return pltpu.CompilerParams(dimension_semantics=sem, vmem_limit_bytes=VMEM_LIMIT_BYTES)


def _sigmoid(x):
    return 0.5 * jnp.tanh(0.5 * x) + 0.5


def _log_sigmoid(x):
    return jnp.minimum(x, 0.0) - jnp.log(1.0 + jnp.exp(-jnp.abs(x)))


def _split3(x):
    a = x.astype(BF16)
    r = x - a.astype(F32)
    b = r.astype(BF16)
    c = (r - b.astype(F32)).astype(BF16)
    return a, b, c


def _tri_cumsum(tri_bf, x):
    a, b, c = _split3(x)
    dot = functools.partial(jnp.dot, preferred_element_type=F32)
    return dot(tri_bf, a) + dot(tri_bf, b) + dot(tri_bf, c)


def _inproj_kernel(x_ref, gain_ref, wm_ref, ws_ref, zm_ref, zs_ref, h_scr):
    @pl.when(pl.program_id(1) == 0)
    def _():
        x = x_ref[...]
        ms = jnp.mean(x * x, axis=-1, keepdims=True)
        h = (x * lax.rsqrt(ms + EPS) * gain_ref[...]).astype(BF16)
        h_scr[...] = h
        zs_ref[...] = jnp.dot(h, ws_ref[...], preferred_element_type=F32)

    zm_ref[...] = jnp.dot(h_scr[...], wm_ref[...], preferred_element_type=F32).astype(BF16)


def _inproj(x2, gain, w_main, w_small, *, layer, tm, tn):
    T, D = x2.shape
    NM = w_main.shape[2]
    return pl.pallas_call(
        _inproj_kernel,
        grid=(T // tm, NM // tn),
        in_specs=[
            pl.BlockSpec((tm, D), lambda i, j: (i, 0)),
            pl.BlockSpec((1, D), lambda i, j: (0, 0)),
            pl.BlockSpec((None, D, tn), lambda i, j: (layer, 0, j)),
            pl.BlockSpec((None, D, LANES), lambda i, j: (layer, 0, 0)),
        ],
        out_specs=[
            pl.BlockSpec((tm, tn), lambda i, j: (i, j)),
            pl.BlockSpec((tm, LANES), lambda i, j: (i, 0)),
        ],
        out_shape=[
            jax.ShapeDtypeStruct((T, NM), BF16),
            jax.ShapeDtypeStruct((T, LANES), F32),
        ],
        scratch_shapes=[pltpu.VMEM((tm, D), BF16)],
        compiler_params=_cparams("parallel", "arbitrary"),
        name="inproj",
    )(x2, gain, w_main, w_small)


def _gla_kernel(q_ref, k_ref, v_ref, zs_ref, gate_ref, wup_ref, bgk_ref, gain_ref,
                o_ref, state_scr, b_scr, p_scr, *, nchunks, unroll):
    C, SUB = GLA_CHUNK, GLA_SUB
    NSUB = C // SUB
    NPAIR = SUB // 2
    DK, DV = GLA_DK, GLA_DV
    dot = functools.partial(jnp.dot, preferred_element_type=F32)

    @pl.when(pl.program_id(2) == 0)
    def _():
        state_scr[...] = jnp.zeros_like(state_scr)

    row = lax.broadcasted_iota(jnp.int32, (C, C), 0)
    col = lax.broadcasted_iota(jnp.int32, (C, C), 1)
    tri = (col <= row).astype(BF16)
    same_sub = (row // SUB) == (col // SUB)
    causal = col <= row

    z1, z2, z3 = _split3(zs_ref[...])
    zlane = lax.broadcasted_iota(jnp.int32, z1.shape, 1)
    lhs = jnp.where(zlane < 3 * GK_RANK, z1, jnp.where(zlane < 5 * GK_RANK, z2, z3))
    w1, w2, w3 = _split3(wup_ref[...])
    rhs = jnp.concatenate([w1, w2, w3, w1, w2, w1,
                           jnp.zeros((LANES - GD_COPIES * GK_RANK, DK), BF16)], axis=0)
    logits = dot(lhs, rhs) + bgk_ref[...]
    g = _log_sigmoid(logits) * (LOG2E / GK_NORMALIZER)
    for c in range(nchunks):
        b_scr[c * C:(c + 1) * C, :] = _tri_cumsum(tri, g[c * C:(c + 1) * C, :])

    cs_r = lax.broadcasted_iota(jnp.int32, (2 * DK, LANES), 0)
    cs_c = lax.broadcasted_iota(jnp.int32, (2 * DK, LANES), 1)
    chansum = ((cs_r // DK) == (cs_c % 2)).astype(BF16)
    pair_of_lane = (lax.broadcasted_iota(jnp.int32, (C, LANES), 1) % SUB) // 2
    scale = DK ** -0.5
    gain = gain_ref[...]

    def chunk(c, p_slot, state):
        r0 = pl.multiple_of(c * C, C)
        rows = pl.ds(r0, C)
        q = q_ref[rows, :].astype(F32) * scale
        k = k_ref[rows, :].astype(F32)
        v = v_ref[rows, :]
        b = b_scr[rows, :]
        b_last = b[C - 1:C, :]

        o = dot((q * jnp.exp2(b)).astype(BF16), state.astype(BF16))

        cross = [jnp.zeros((SUB, C), F32)]
        for i in range(1, NSUB):
            n = i * SUB
            bref = b[n - 1:n, :]
            q_t = (q[n:n + SUB, :] * jnp.exp2(b[n:n + SUB, :] - bref)).astype(BF16)
            k_t = (k[:n, :] * jnp.exp2(bref - b[:n, :])).astype(BF16)
            k_t = jnp.concatenate([k_t, jnp.zeros((C - n, DK), BF16)], axis=0)
            cross.append(lax.dot_general(q_t, k_t, (((1,), (1,)), ((), ())),
                                         preferred_element_type=F32))
        a_cross = jnp.concatenate(cross, axis=0)

        k3 = k.reshape(NSUB, SUB, DK)
        b3 = b.reshape(NSUB, SUB, DK)
        for j in range(SUB):
            kj = jnp.broadcast_to(k3[:, j:j + 1, :], (NSUB, SUB, DK)).reshape(C, DK)
            bj = jnp.broadcast_to(b3[:, j:j + 1, :], (NSUB, SUB, DK)).reshape(C, DK)
            p = q * kj * jnp.exp2(jnp.minimum(b - bj, 0.0))
            p_slot[(j // 2) * C:(j // 2 + 1) * C, (j % 2) * DK:(j % 2 + 1) * DK] = p.astype(BF16)
        d = dot(p_slot[...], chansum)
        a_diag = d[0:C, :]
        for t in range(1, NPAIR):
            a_diag = jnp.where(pair_of_lane == t, d[t * C:(t + 1) * C, :], a_diag)

        a = jnp.where(causal, jnp.where(same_sub, a_diag[:, :C], a_cross), 0.0)
        o = o + dot(a.astype(BF16), v)

        k_dec = (k * jnp.exp2(b_last - b)).astype(BF16)
        upd = lax.dot_general(k_dec, v, (((0,), (0,)), ((), ())),
                              preferred_element_type=F32)
        decay_col = jnp.transpose(jnp.broadcast_to(jnp.exp2(b_last), (DK, DK)))[:, 0:1]
        new_state = state * decay_col + upd

        ms = jnp.mean(o * o, axis=-1, keepdims=True)
        on = o * lax.rsqrt(ms + EPS) * gain
        gt = gate_ref[rows, :].astype(F32)
        o_ref[rows, :] = (on * (gt * _sigmoid(gt))).astype(o_ref.dtype)
        return new_state

    def group(gi, carry):
        state = state_scr[...]
        for u in range(unroll):
            state = chunk(gi * unroll + u, p_scr.at[u], state)
        state_scr[...] = state
        return carry

    lax.fori_loop(0, nchunks // unroll, group, 0)


def _gla(z_main, z_small, w_up, b_gk, gain, *, batch, seq, ts, unroll):
    T = z_main.shape[0]
    nsb = seq // ts
    nchunks = ts // GLA_CHUNK
    rowmap = lambda b, h, s: b * nsb + s
    kblk = ZM_GK // GLA_DK
    vblk = ZM_GV // GLA_DV
    gblk = ZM_GGATE // GLA_DV
    return pl.pallas_call(
        functools.partial(_gla_kernel, nchunks=nchunks, unroll=unroll),
        grid=(batch, GLA_HEADS, nsb),
        in_specs=[
            pl.BlockSpec((ts, GLA_DK), lambda b, h, s: (rowmap(b, h, s), h)),
            pl.BlockSpec((ts, GLA_DK), lambda b, h, s: (rowmap(b, h, s), kblk + h)),
            pl.BlockSpec((ts, GLA_DV), lambda b, h, s: (rowmap(b, h, s), vblk + h)),
            pl.BlockSpec((ts, LANES), lambda b, h, s: (rowmap(b, h, s), 0)),
            pl.BlockSpec((ts, GLA_DV), lambda b, h, s: (rowmap(b, h, s), gblk + h)),
            pl.BlockSpec((GK_RANK, GLA_DK), lambda b, h, s: (0, h)),
            pl.BlockSpec((1, GLA_DK), lambda b, h, s: (0, h)),
            pl.BlockSpec((1, GLA_DV), lambda b, h, s: (0, 0)),
        ],
        out_specs=pl.BlockSpec((ts, GLA_DV), lambda b, h, s: (rowmap(b, h, s), h)),
        out_shape=jax.ShapeDtypeStruct((T, GLA_WIDTH), BF16),
        scratch_shapes=[
            pltpu.VMEM((GLA_DK, GLA_DV), F32),
            pltpu.VMEM((ts, GLA_DK), F32),
            pltpu.VMEM((unroll, GLA_SUB // 2 * GLA_CHUNK, 2 * GLA_DK), BF16),
        ],
        compiler_params=_cparams("parallel", "parallel", "arbitrary"),
        name="gla",
    )(z_main, z_main, z_main, z_small, z_main, w_up, b_gk, gain)


def _fox_prep_kernel(q_ref, k_ref, v_ref, zs_ref, bf_ref, qa_ref, ka_ref, vt_ref, carry_scr, *, ts):
    @pl.when(pl.program_id(1) == 0)
    def _():
        carry_scr[...] = jnp.zeros_like(carry_scr)

    row = lax.broadcasted_iota(jnp.int32, (ts, ts), 0)
    col = lax.broadcasted_iota(jnp.int32, (ts, ts), 1)
    tri = (col <= row).astype(BF16)
    log_f = _log_sigmoid(zs_ref[...] + bf_ref[...])
    c_all = _tri_cumsum(tri, log_f) + carry_scr[...]
    carry_scr[...] = c_all[ts - 1:ts, :]

    vt_ref[...] = jnp.transpose(v_ref[...].astype(F32)).astype(BF16)

    lane = lax.broadcasted_iota(jnp.int32, (ts, LANES - FOX_DH), 1)
    qf = q_ref[...].astype(F32) * (FOX_DH ** -0.5 * LOG2E)
    kf = k_ref[...].astype(F32)
    for h in range(FOX_HEADS):
        c = c_all[:, ZS_FLOGIT + h:ZS_FLOGIT + h + 1] * LOG2E
        c1, c2, c3 = (t.astype(F32) for t in _split3(c))
        ones = jnp.where(lane < FOX_EXTRA, 1.0, 0.0)
        cs = jnp.where(lane == 0, c1, jnp.where(lane == 1, c2, jnp.where(lane == 2, c3, 0.0)))
        ex_q = jnp.where(lane < 3, cs, ones)
        cs_k = jnp.where(lane == 3, -c1, jnp.where(lane == 4, -c2, jnp.where(lane == 5, -c3, 0.0)))
        ex_k = jnp.where(lane < 3, ones, cs_k)
        qa = jnp.concatenate([qf[:, h * FOX_DH:(h + 1) * FOX_DH], ex_q], axis=1)
        ka = jnp.concatenate([kf[:, h * FOX_DH:(h + 1) * FOX_DH], ex_k], axis=1)
        qa_ref[:, h * LANES:(h + 1) * LANES] = qa.astype(BF16)
        ka_ref[:, h * LANES:(h + 1) * LANES] = ka.astype(BF16)


def _fox_prep(z_main, z_small, bf_row, *, batch, seq, ts):
    T = z_main.shape[0]
    nsb = seq // ts
    qblk = ZM_FQ // FOX_WIDTH
    kblk = ZM_FK // FOX_WIDTH
    vblk = ZM_FV // FOX_WIDTH
    aug = jax.ShapeDtypeStruct((T, FOX_HEADS * LANES), BF16)
    return pl.pallas_call(
        functools.partial(_fox_prep_kernel, ts=ts),
        grid=(batch, nsb),
        in_specs=[
            pl.BlockSpec((ts, FOX_WIDTH), lambda b, s: (b * nsb + s, qblk)),
            pl.BlockSpec((ts, FOX_WIDTH), lambda b, s: (b * nsb + s, kblk)),
            pl.BlockSpec((ts, FOX_WIDTH), lambda b, s: (b * nsb + s, vblk)),
            pl.BlockSpec((ts, LANES), lambda b, s: (b * nsb + s, 0)),
            pl.BlockSpec((1, LANES), lambda b, s: (0, 0)),
        ],
        out_specs=[
            pl.BlockSpec((ts, FOX_HEADS * LANES), lambda b, s: (b * nsb + s, 0)),
            pl.BlockSpec((ts, FOX_HEADS * LANES), lambda b, s: (b * nsb + s, 0)),
            pl.BlockSpec((FOX_WIDTH, ts), lambda b, s: (b, s)),
        ],
        out_shape=[aug, aug, jax.ShapeDtypeStruct((batch * FOX_WIDTH, seq), BF16)],
        scratch_shapes=[pltpu.VMEM((1, LANES), F32)],
        compiler_params=_cparams("parallel", "arbitrary"),
        name="fox_prep",
    )(z_main, z_main, z_main, z_small, bf_row)


def _fox_attn_kernel(q_ref, k_ref, vt_ref, gate_ref, o_ref, m_scr, l_scr, acc_scr, *, tq, tk):
    qi = pl.program_id(2)
    m_scr[...] = jnp.full_like(m_scr, -jnp.inf)
    l_scr[...] = jnp.zeros_like(l_scr)
    acc_scr[...] = jnp.zeros_like(acc_scr)

    def scores(j, hh):
        k0 = pl.multiple_of(j * tk, tk)
        q = q_ref[:, hh * LANES:(hh + 1) * LANES]
        k = k_ref[pl.ds(k0, tk), hh * LANES:(hh + 1) * LANES]
        return lax.dot_general(k, q, (((1,), (1,)), ((), ())),
                               preferred_element_type=F32)

    def absorb(j, hh, st, stats, masked):
        m_prev, l_prev, acc = stats
        if masked:
            k_pos = lax.broadcasted_iota(jnp.int32, (tk, tq), 0)
            q_pos = lax.broadcasted_iota(jnp.int32, (tk, tq), 1)
            st = jnp.where(k_pos <= q_pos, st, -jnp.inf)
        m_new = jnp.maximum(m_prev, jnp.max(st, axis=0, keepdims=True))
        alpha = jnp.exp2(m_prev - m_new)
        p = jnp.exp2(st - m_new)
        l_new = alpha * l_prev + jnp.sum(p, axis=0, keepdims=True)
        k0 = pl.multiple_of(j * tk, tk)
        vt = vt_ref[hh * FOX_DH:(hh + 1) * FOX_DH, pl.ds(k0, tk)]
        pv = jnp.dot(vt, p.astype(BF16), preferred_element_type=F32)
        return m_new, l_new, acc * alpha + pv

    def run(chunks, diag_last):
        sts = [[scores(j, hh) for hh in range(2)] for j in chunks]
        stats = [(m_scr[hh], l_scr[hh], acc_scr[hh]) for hh in range(2)]
        for ci, j in enumerate(chunks):
            for hh in range(2):
                stats[hh] = absorb(j, hh, sts[ci][hh], stats[hh],
                                   masked=diag_last and ci == len(chunks) - 1)
        for hh in range(2):
            m_scr[hh], l_scr[hh], acc_scr[hh] = stats[hh]

    def pair(jp, carry):
        run([2 * jp, 2 * jp + 1], False)
        return carry

    lax.fori_loop(0, qi // 2, pair, 0)

    @pl.when(qi % 2 == 1)
    def _():
        run([qi - 1, qi], True)

    @pl.when(qi % 2 == 0)
    def _():
        run([qi], True)

    ot = jnp.concatenate([acc_scr[0] / l_scr[0], acc_scr[1] / l_scr[1]], axis=0)
    gt = gate_ref[...].astype(F32)
    o_ref[...] = (jnp.transpose(ot) * (gt * _sigmoid(gt))).astype(o_ref.dtype)


def _fox_attn(q_aug, k_aug, v_t, z_main, *, batch, seq, tq):
    T = z_main.shape[0]
    tk = tq
    nq = seq // tq
    npair = FOX_HEADS // 2
    gblk = ZM_FGATE // LANES
    return pl.pallas_call(
        functools.partial(_fox_attn_kernel, tq=tq, tk=tk),
        grid=(batch, npair, nq),
        in_specs=[
            pl.BlockSpec((tq, 2 * LANES), lambda b, p, i: (b * nq + i, p)),
            pl.BlockSpec((seq, 2 * LANES), lambda b, p, i: (b, p)),
            pl.BlockSpec((2 * FOX_DH, seq), lambda b, p, i: (b * npair + p, 0)),
            pl.BlockSpec((tq, LANES), lambda b, p, i: (b * nq + i, gblk + p)),
        ],
        out_specs=pl.BlockSpec((tq, LANES), lambda b, p, i: (b * nq + i, p)),
        out_shape=jax.ShapeDtypeStruct((T, FOX_WIDTH), BF16),
        scratch_shapes=[
            pltpu.VMEM((2, 1, tq), F32),
            pltpu.VMEM((2, 1, tq), F32),
            pltpu.VMEM((2, FOX_DH, tq), F32),
        ],
        compiler_params=_cparams("parallel", "parallel", "arbitrary"),
        name="fox_attn",
    )(q_aug, k_aug, v_t, z_main)


def _memkv_kernel(mem_ref, gain_ref, w_ref, o_ref):
    x = mem_ref[...]
    ms = jnp.mean(x * x, axis=-1, keepdims=True)
    h = (x * lax.rsqrt(ms + EPS) * gain_ref[...]).astype(BF16)
    o_ref[...] = jnp.dot(h, w_ref[...], preferred_element_type=F32).astype(o_ref.dtype)


def _memkv(mem2, gain, w_kv, *, tm):
    R, D = mem2.shape
    N = w_kv.shape[1]
    return pl.pallas_call(
        _memkv_kernel,
        grid=(R // tm,),
        in_specs=[
            pl.BlockSpec((tm, D), lambda i: (i, 0)),
            pl.BlockSpec((1, D), lambda i: (0, 0)),
            pl.BlockSpec((D, N), lambda i: (0, 0)),
        ],
        out_specs=pl.BlockSpec((tm, N), lambda i: (i, 0)),
        out_shape=jax.ShapeDtypeStruct((R, N), BF16),
        compiler_params=_cparams("parallel"),
        name="memkv",
    )(mem2, gain, w_kv)


def _mem_attn_kernel(q_ref, gate_ref, mk_ref, mv_ref, o_ref):
    scale = MEM_DH ** -0.5
    outs = []
    for h in range(MEM_HEADS):
        sl = slice(h * MEM_DH, (h + 1) * MEM_DH)
        s = lax.dot_general(q_ref[:, sl], mk_ref[:, sl], (((1,), (1,)), ((), ())),
                            preferred_element_type=F32) * scale
        m = jnp.max(s, axis=-1, keepdims=True)
        p = jnp.exp(s - m)
        l = jnp.sum(p, axis=-1, keepdims=True)
        p = (p / l).astype(BF16)
        outs.append(jnp.dot(p, mv_ref[:, sl], preferred_element_type=F32))
    o = jnp.concatenate(outs, axis=1)
    gt = gate_ref[...].astype(F32)
    o_ref[...] = (o * (gt * _sigmoid(gt))).astype(o_ref.dtype)


def _mem_attn(z_main, mkv, *, batch, seq, mem_len, tm):
    T = z_main.shape[0]
    nsb = seq // tm
    qblk = ZM_MQ // MEM_WIDTH
    gblk = ZM_MGATE // MEM_WIDTH
    return pl.pallas_call(
        _mem_attn_kernel,
        grid=(batch, nsb),
        in_specs=[
            pl.BlockSpec((tm, MEM_WIDTH), lambda b, s: (b * nsb + s, qblk)),
            pl.BlockSpec((tm, MEM_WIDTH), lambda b, s: (b * nsb + s, gblk)),
            pl.BlockSpec((mem_len, MEM_WIDTH), lambda b, s: (b, 0)),
            pl.BlockSpec((mem_len, MEM_WIDTH), lambda b, s: (b, 1)),
        ],
        out_specs=pl.BlockSpec((tm, MEM_WIDTH), lambda b, s: (b * nsb + s, 0)),
        out_shape=jax.ShapeDtypeStruct((T, MEM_WIDTH), BF16),
        compiler_params=_cparams("parallel", "parallel"),
        name="mem_attn",
    )(z_main, z_main, mkv, mkv)


def _merge_kernel(oa_ref, ob_ref, oc_ref, g0_ref, g1_ref, g2_ref, wa_ref, wb_ref, wc_ref, y_ref):
    dot = functools.partial(jnp.dot, preferred_element_type=F32)
    y = _sigmoid(g0_ref[...].astype(F32)) * dot(oa_ref[...], wa_ref[...])
    y = y + _sigmoid(g1_ref[...].astype(F32)) * dot(ob_ref[...], wb_ref[...])
    y = y + _sigmoid(g2_ref[...].astype(F32)) * dot(oc_ref[...], wc_ref[...])
    y_ref[...] = y.astype(y_ref.dtype)


def _merge(o_a, o_b, o_c, z_main, w_branch, *, tm, tn):
    T = o_a.shape[0]
    D = w_branch.shape[1]
    gbase = ZM_MERGE // tn
    gstep = D // tn
    ra = GLA_WIDTH // FOX_WIDTH
    return pl.pallas_call(
        _merge_kernel,
        grid=(T // tm, D // tn),
        in_specs=[
            pl.BlockSpec((tm, GLA_WIDTH), lambda i, j: (i, 0)),
            pl.BlockSpec((tm, FOX_WIDTH), lambda i, j: (i, 0)),
            pl.BlockSpec((tm, MEM_WIDTH), lambda i, j: (i, 0)),
            pl.BlockSpec((tm, tn), lambda i, j: (i, gbase + j)),
            pl.BlockSpec((tm, tn), lambda i, j: (i, gbase + gstep + j)),
            pl.BlockSpec((tm, tn), lambda i, j: (i, gbase + 2 * gstep + j)),
            pl.BlockSpec((GLA_WIDTH, tn), lambda i, j: (0, j)),
            pl.BlockSpec((FOX_WIDTH, tn), lambda i, j: (ra, j)),
            pl.BlockSpec((MEM_WIDTH, tn), lambda i, j: (ra + 1, j)),
        ],
        out_specs=pl.BlockSpec((tm, tn), lambda i, j: (i, j)),
        out_shape=jax.ShapeDtypeStruct((T, D), BF16),
        compiler_params=_cparams("parallel", "arbitrary"),
        name="merge",
    )(o_a, o_b, o_c, z_main, z_main, z_main, w_branch, w_branch, w_branch)


def _outproj_kernel(x_ref, y_ref, w_ref, gain_ref, o_ref, *, final_norm):
    x = x_ref[...] + jnp.dot(y_ref[...], w_ref[...], preferred_element_type=F32)
    if final_norm:
        ms = jnp.mean(x * x, axis=-1, keepdims=True)
        x = x * lax.rsqrt(ms + EPS) * gain_ref[...]
    o_ref[...] = x


def _outproj(x2, y, w_out, final_gain, *, tm, final_norm):
    T, D = x2.shape
    return pl.pallas_call(
        functools.partial(_outproj_kernel, final_norm=final_norm),
        grid=(T // tm,),
        in_specs=[
            pl.BlockSpec((tm, D), lambda i: (i, 0)),
            pl.BlockSpec((tm, D), lambda i: (i, 0)),
            pl.BlockSpec((D, D), lambda i: (0, 0)),
            pl.BlockSpec((1, D), lambda i: (0, 0)),
        ],
        out_specs=pl.BlockSpec((tm, D), lambda i: (i, 0)),
        out_shape=jax.ShapeDtypeStruct((T, D), F32),
        compiler_params=_cparams("parallel"),
        name="outproj",
    )(x2, y, w_out, final_gain)


W_GDOWN = 2 * GLA_KWIDTH + 2 * GLA_WIDTH
W_FQ = W_GDOWN + GK_RANK
W_FLOGIT = W_FQ + 3 * FOX_WIDTH
W_FGATE = W_FLOGIT + FOX_HEADS
PACK_COLS = 512


def _pack_kernel(w_ref, wm_ref, ws_ref):
    n_main = wm_ref.shape[1]
    for c0 in range(0, n_main, PACK_COLS):
        src = c0 if c0 < W_GDOWN else (c0 + GK_RANK if c0 + GK_RANK < W_FLOGIT else c0 + GK_RANK + FOX_HEADS)
        wm_ref[:, c0:c0 + PACK_COLS] = w_ref[:, src:src + PACK_COLS].astype(BF16)
    rows = w_ref.shape[0]
    gd = w_ref[:, W_GDOWN:W_FQ]
    fl = w_ref[:, W_FLOGIT:W_FGATE]
    pad = jnp.zeros((rows, LANES - GD_COPIES * GK_RANK - FOX_HEADS), F32)
    ws_ref[...] = jnp.concatenate([gd] * GD_COPIES + [fl, pad], axis=1).astype(BF16)


def _pack_w_in(w_in, *, tr):
    depth, d, in_cols = w_in.shape
    n_main = in_cols - GK_RANK - FOX_HEADS
    return pl.pallas_call(
        _pack_kernel,
        grid=(depth, d // tr),
        in_specs=[pl.BlockSpec((None, tr, in_cols), lambda l, i: (l, i, 0))],
        out_specs=[
            pl.BlockSpec((None, tr, n_main), lambda l, i: (l, i, 0)),
            pl.BlockSpec((None, tr, LANES), lambda l, i: (l, i, 0)),
        ],
        out_shape=[
            jax.ShapeDtypeStruct((depth, d, n_main), BF16),
            jax.ShapeDtypeStruct((depth, d, LANES), BF16),
        ],
        compiler_params=_cparams("parallel", "parallel"),
        name="pack_w_in",
    )(w_in)


def _trunk(x, mem, norm_gain, w_in, w_gk_up, b_gk, gla_norm_gain, b_f, mem_norm_gain,
           w_mem_kv, w_branch, w_out, final_gain, *, tiles):
    B, S, D = x.shape
    M = mem.shape[1]
    depth = w_in.shape[0]
    x2 = x.reshape(B * S, D)
    mem2 = mem.reshape(B * M, D)
    w_main, w_small = _pack_w_in(w_in, tr=tiles["pack_tr"])
    for l in range(depth):
        z_main, z_small = _inproj(x2, norm_gain[l][None, :], w_main, w_small,
                                  layer=l, tm=tiles["in_tm"], tn=tiles["in_tn"])
        o_a = _gla(z_main, z_small, w_gk_up[l], b_gk[l][None, :], gla_norm_gain[l][None, :],
                   batch=B, seq=S, ts=tiles["gla_ts"], unroll=tiles["gla_unroll"])
        bf_row = jnp.zeros((1, LANES), F32).at[0, ZS_FLOGIT:ZS_FLOGIT + FOX_HEADS].set(b_f[l])
        q_aug, k_aug, v_t = _fox_prep(z_main, z_small, bf_row, batch=B, seq=S, ts=tiles["prep_ts"])
        o_b = _fox_attn(q_aug, k_aug, v_t, z_main, batch=B, seq=S, tq=tiles["fox_tq"])
        mkv = _memkv(mem2, mem_norm_gain[l][None, :], w_mem_kv[l].astype(BF16), tm=tiles["memkv_tm"])
        o_c = _mem_attn(z_main, mkv, batch=B, seq=S, mem_len=M, tm=tiles["mem_tm"])
        y = _merge(o_a, o_b, o_c, z_main, w_branch[l].astype(BF16),
                   tm=tiles["merge_tm"], tn=tiles["merge_tn"])
        x2 = _outproj(x2, y, w_out[l].astype(BF16), final_gain[None, :],
                      tm=tiles["out_tm"], final_norm=(l == depth - 1))
    return x2.reshape(B, S, D)


_TILES = dict(pack_tr=128, in_tm=1024, in_tn=1024, gla_ts=512, gla_unroll=8, prep_ts=512, fox_tq=512,
              memkv_tm=256, mem_tm=512, merge_tm=1024, merge_tn=512, out_tm=512)


def kernel(x, mem, norm_gain, w_in, w_gk_up, b_gk, gla_norm_gain, b_f, mem_norm_gain,
           w_mem_kv, w_branch, w_out, final_gain):
    return _trunk(x, mem, norm_gain, w_in, w_gk_up, b_gk, gla_norm_gain, b_f, mem_norm_gain,
                  w_mem_kv, w_branch, w_out, final_gain, tiles=_TILES)
```

```python
import functools

import jax
import jax.numpy as jnp
from jax import lax
from jax.experimental import pallas as pl
from jax.experimental.pallas import tpu as pltpu

F32 = jnp.float32
BF16 = jnp.bfloat16

EPS = 1e-6

GLA_HEADS, GLA_DK, GLA_DV = 4, 128, 256
GLA_KWIDTH = GLA_HEADS * GLA_DK
GLA_WIDTH = GLA_HEADS * GLA_DV
GK_RANK = 16
GK_NORMALIZER = 16.0
FOX_HEADS, FOX_DH = 8, 64
FOX_WIDTH = FOX_HEADS * FOX_DH
MEM_HEADS, MEM_DH = 4, 128
MEM_WIDTH = MEM_HEADS * MEM_DH
N_BRANCH = 3

LANES = 128
VMEM_LIMIT_BYTES = 56 * 1024 * 1024

ZM_GQ = 0
ZM_GK = ZM_GQ + GLA_KWIDTH
ZM_GV = ZM_GK + GLA_KWIDTH
ZM_GGATE = ZM_GV + GLA_WIDTH
ZM_FQ = ZM_GGATE + GLA_WIDTH
ZM_FK = ZM_FQ + FOX_WIDTH
ZM_FV = ZM_FK + FOX_WIDTH
ZM_FGATE = ZM_FV + FOX_WIDTH
ZM_MQ = ZM_FGATE + FOX_WIDTH
ZM_MGATE = ZM_MQ + MEM_WIDTH
ZM_MERGE = ZM_MGATE + MEM_WIDTH
GD_COPIES = 6
ZS_FLOGIT = GD_COPIES * GK_RANK

GLA_CHUNK = 64
GLA_SUB = 8
FOX_EXTRA = 6
LOG2E = 1.4426950408889634


def _cparams(*sem):
    return pltpu.CompilerParams(dimension_semantics=sem, vmem_limit_bytes=VMEM_LIMIT_BYTES)


def _sigmoid(x):
    return 0.5 * jnp.tanh(0.5 * x) + 0.5


def _log_sigmoid(x):
    return jnp.minimum(x, 0.0) - jnp.log(1.0 + jnp.exp(-jnp.abs(x)))


def _split3(x):
    a = x.astype(BF16)
    r = x - a.astype(F32)
    b = r.astype(BF16)
    c = (r - b.astype(F32)).astype(BF16)
    return a, b, c


def _dot_nt(a, b):
    return lax.dot_general(a, b, (((1,), (1,)), ((), ())), preferred_element_type=F32)


def _tri_cumsum(tri_bf, x):
    a, b, c = _split3(x)
    dot = functools.partial(jnp.dot, preferred_element_type=F32)
    return dot(tri_bf, a) + dot(tri_bf, b) + dot(tri_bf, c)


def _inproj_kernel(x_ref, gain_ref, wm_ref, ws_ref, zm_ref, zs_ref, h_scr):
    @pl.when(pl.program_id(1) == 0)
    def _():
        x = x_ref[...]
        ms = jnp.mean(x * x, axis=-1, keepdims=True)
        h = (x * lax.rsqrt(ms + EPS) * gain_ref[...]).astype(BF16)
        h_scr[...] = h
        zs_ref[...] = _dot_nt(h, ws_ref[...])

    zm_ref[...] = _dot_nt(h_scr[...], wm_ref[...]).astype(BF16)


def _inproj(x2, gain, w_main_t, w_small_t, *, layer, tm, tn):
    T, D = x2.shape
    NM = w_main_t.shape[1]
    return pl.pallas_call(
        _inproj_kernel,
        grid=(T // tm, NM // tn),
        in_specs=[
            pl.BlockSpec((tm, D), lambda i, j: (i, 0)),
            pl.BlockSpec((1, D), lambda i, j: (0, 0)),
            pl.BlockSpec((None, tn, D), lambda i, j: (layer, j, 0)),
            pl.BlockSpec((None, LANES, D), lambda i, j: (layer, 0, 0)),
        ],
        out_specs=[
            pl.BlockSpec((tm, tn), lambda i, j: (i, j)),
            pl.BlockSpec((tm, LANES), lambda i, j: (i, 0)),
        ],
        out_shape=[
            jax.ShapeDtypeStruct((T, NM), BF16),
            jax.ShapeDtypeStruct((T, LANES), F32),
        ],
        scratch_shapes=[pltpu.VMEM((tm, D), BF16)],
        compiler_params=_cparams("parallel", "arbitrary"),
        name="inproj",
    )(x2, gain, w_main_t, w_small_t)


def _gla_kernel(q_ref, k_ref, v_ref, zs_ref, gate_ref, wup_ref, bgk_ref, gain_ref,
                o_ref, state_scr, b_scr, p_scr, *, nchunks, unroll):
    C, SUB = GLA_CHUNK, GLA_SUB
    NSUB = C // SUB
    NPAIR = SUB // 2
    DK, DV = GLA_DK, GLA_DV
    dot = functools.partial(jnp.dot, preferred_element_type=F32)

    @pl.when(pl.program_id(2) == 0)
    def _():
        state_scr[...] = jnp.zeros_like(state_scr)

    row = lax.broadcasted_iota(jnp.int32, (C, C), 0)
    col = lax.broadcasted_iota(jnp.int32, (C, C), 1)
    tri = (col <= row).astype(BF16)
    same_sub = (row // SUB) == (col // SUB)
    causal = col <= row

    z1, z2, z3 = _split3(zs_ref[...])
    zlane = lax.broadcasted_iota(jnp.int32, z1.shape, 1)
    lhs = jnp.where(zlane < 3 * GK_RANK, z1, jnp.where(zlane < 5 * GK_RANK, z2, z3))
    w1, w2, w3 = _split3(wup_ref[...])
    rhs = jnp.concatenate([w1, w2, w3, w1, w2, w1,
                           jnp.zeros((LANES - GD_COPIES * GK_RANK, DK), BF16)], axis=0)
    logits = dot(lhs, rhs) + bgk_ref[...]
    g = _log_sigmoid(logits) * (LOG2E / GK_NORMALIZER)
    for c in range(nchunks):
        b_scr[c * C:(c + 1) * C, :] = _tri_cumsum(tri, g[c * C:(c + 1) * C, :])

    cs_r = lax.broadcasted_iota(jnp.int32, (2 * DK, LANES), 0)
    cs_c = lax.broadcasted_iota(jnp.int32, (2 * DK, LANES), 1)
    chansum = ((cs_r // DK) == (cs_c % 2)).astype(BF16)
    pair_of_lane = (lax.broadcasted_iota(jnp.int32, (C, LANES), 1) % SUB) // 2
    scale = DK ** -0.5
    gain = gain_ref[...]

    def chunk(c, p_slot, state):
        r0 = pl.multiple_of(c * C, C)
        rows = pl.ds(r0, C)
        q = q_ref[rows, :].astype(F32) * scale
        k = k_ref[rows, :].astype(F32)
        v = v_ref[rows, :]
        b = b_scr[rows, :]
        b_last = b[C - 1:C, :]

        o = dot((q * jnp.exp2(b)).astype(BF16), state.astype(BF16))

        cross = [jnp.zeros((SUB, C), F32)]
        for i in range(1, NSUB):
            n = i * SUB
            bref = b[n - 1:n, :]
            q_t = (q[n:n + SUB, :] * jnp.exp2(b[n:n + SUB, :] - bref)).astype(BF16)
            k_t = (k[:n, :] * jnp.exp2(bref - b[:n, :])).astype(BF16)
            k_t = jnp.concatenate([k_t, jnp.zeros((C - n, DK), BF16)], axis=0)
            cross.append(lax.dot_general(q_t, k_t, (((1,), (1,)), ((), ())),
                                         preferred_element_type=F32))
        a_cross = jnp.concatenate(cross, axis=0)

        k3 = k.reshape(NSUB, SUB, DK)
        b3 = b.reshape(NSUB, SUB, DK)
        for j in range(SUB):
            kj = jnp.broadcast_to(k3[:, j:j + 1, :], (NSUB, SUB, DK)).reshape(C, DK)
            bj = jnp.broadcast_to(b3[:, j:j + 1, :], (NSUB, SUB, DK)).reshape(C, DK)
            p = q * kj * jnp.exp2(jnp.minimum(b - bj, 0.0))
            p_slot[(j // 2) * C:(j // 2 + 1) * C, (j % 2) * DK:(j % 2 + 1) * DK] = p.astype(BF16)
        d = dot(p_slot[...], chansum)
        a_diag = d[0:C, :]
        for t in range(1, NPAIR):
            a_diag = jnp.where(pair_of_lane == t, d[t * C:(t + 1) * C, :], a_diag)

        a = jnp.where(causal, jnp.where(same_sub, a_diag[:, :C], a_cross), 0.0)
        o = o + dot(a.astype(BF16), v)

        k_dec = (k * jnp.exp2(b_last - b)).astype(BF16)
        upd = lax.dot_general(k_dec, v, (((0,), (0,)), ((), ())),
                              preferred_element_type=F32)
        decay_col = jnp.transpose(jnp.broadcast_to(jnp.exp2(b_last), (DK, DK)))[:, 0:1]
        new_state = state * decay_col + upd

        ms = jnp.mean(o * o, axis=-1, keepdims=True)
        on = o * lax.rsqrt(ms + EPS) * gain
        gt = gate_ref[rows, :].astype(F32)
        o_ref[rows, :] = (on * (gt * _sigmoid(gt))).astype(o_ref.dtype)
        return new_state

    def group(gi, carry):
        state = state_scr[...]
        for u in range(unroll):
            state = chunk(gi * unroll + u, p_scr.at[u], state)
        state_scr[...] = state
        return carry

    lax.fori_loop(0, nchunks // unroll, group, 0)


def _gla(z_main, z_small, w_up, b_gk, gain, *, batch, seq, ts, unroll):
    T = z_main.shape[0]
    nsb = seq // ts
    nchunks = ts // GLA_CHUNK
    rowmap = lambda b, h, s: b * nsb + s
    kblk = ZM_GK // GLA_DK
    vblk = ZM_GV // GLA_DV
    gblk = ZM_GGATE // GLA_DV
    return pl.pallas_call(
        functools.partial(_gla_kernel, nchunks=nchunks, unroll=unroll),
        grid=(batch, GLA_HEADS, nsb),
        in_specs=[
            pl.BlockSpec((ts, GLA_DK), lambda b, h, s: (rowmap(b, h, s), h)),
            pl.BlockSpec((ts, GLA_DK), lambda b, h, s: (rowmap(b, h, s), kblk + h)),
            pl.BlockSpec((ts, GLA_DV), lambda b, h, s: (rowmap(b, h, s), vblk + h)),
            pl.BlockSpec((ts, LANES), lambda b, h, s: (rowmap(b, h, s), 0)),
            pl.BlockSpec((ts, GLA_DV), lambda b, h, s: (rowmap(b, h, s), gblk + h)),
            pl.BlockSpec((GK_RANK, GLA_DK), lambda b, h, s: (0, h)),
            pl.BlockSpec((1, GLA_DK), lambda b, h, s: (0, h)),
            pl.BlockSpec((1, GLA_DV), lambda b, h, s: (0, 0)),
        ],
        out_specs=pl.BlockSpec((ts, GLA_DV), lambda b, h, s: (rowmap(b, h, s), h)),
        out_shape=jax.ShapeDtypeStruct((T, GLA_WIDTH), BF16),
        scratch_shapes=[
            pltpu.VMEM((GLA_DK, GLA_DV), F32),
            pltpu.VMEM((ts, GLA_DK), F32),
            pltpu.VMEM((unroll, GLA_SUB // 2 * GLA_CHUNK, 2 * GLA_DK), BF16),
        ],
        compiler_params=_cparams("parallel", "parallel", "arbitrary"),
        name="gla",
    )(z_main, z_main, z_main, z_small, z_main, w_up, b_gk, gain)


def _fox_prep_kernel(q_ref, k_ref, v_ref, zs_ref, bf_ref, qa_ref, ka_ref, vt_ref, carry_scr, *, ts):
    @pl.when(pl.program_id(1) == 0)
    def _():
        carry_scr[...] = jnp.zeros_like(carry_scr)

    row = lax.broadcasted_iota(jnp.int32, (ts, ts), 0)
    col = lax.broadcasted_iota(jnp.int32, (ts, ts), 1)
    tri = (col <= row).astype(BF16)
    log_f = _log_sigmoid(zs_ref[...] + bf_ref[...])
    c_all = _tri_cumsum(tri, log_f) + carry_scr[...]
    carry_scr[...] = c_all[ts - 1:ts, :]

    vt_ref[...] = jnp.transpose(v_ref[...].astype(F32)).astype(BF16)

    lane = lax.broadcasted_iota(jnp.int32, (ts, LANES - FOX_DH), 1)
    qf = q_ref[...].astype(F32) * (FOX_DH ** -0.5 * LOG2E)
    kf = k_ref[...].astype(F32)
    for h in range(FOX_HEADS):
        c = c_all[:, ZS_FLOGIT + h:ZS_FLOGIT + h + 1] * LOG2E
        c1, c2, c3 = (t.astype(F32) for t in _split3(c))
        ones = jnp.where(lane < FOX_EXTRA, 1.0, 0.0)
        cs = jnp.where(lane == 0, c1, jnp.where(lane == 1, c2, jnp.where(lane == 2, c3, 0.0)))
        ex_q = jnp.where(lane < 3, cs, ones)
        cs_k = jnp.where(lane == 3, -c1, jnp.where(lane == 4, -c2, jnp.where(lane == 5, -c3, 0.0)))
        ex_k = jnp.where(lane < 3, ones, cs_k)
        qa = jnp.concatenate([qf[:, h * FOX_DH:(h + 1) * FOX_DH], ex_q], axis=1)
        ka = jnp.concatenate([kf[:, h * FOX_DH:(h + 1) * FOX_DH], ex_k], axis=1)
        qa_ref[:, h * LANES:(h + 1) * LANES] = qa.astype(BF16)
        ka_ref[:, h * LANES:(h + 1) * LANES] = ka.astype(BF16)


def _fox_prep(z_main, z_small, bf_row, *, batch, seq, ts):
    T = z_main.shape[0]
    nsb = seq // ts
    qblk = ZM_FQ // FOX_WIDTH
    kblk = ZM_FK // FOX_WIDTH
    vblk = ZM_FV // FOX_WIDTH
    aug = jax.ShapeDtypeStruct((T, FOX_HEADS * LANES), BF16)
    return pl.pallas_call(
        functools.partial(_fox_prep_kernel, ts=ts),
        grid=(batch, nsb),
        in_specs=[
            pl.BlockSpec((ts, FOX_WIDTH), lambda b, s: (b * nsb + s, qblk)),
            pl.BlockSpec((ts, FOX_WIDTH), lambda b, s: (b * nsb + s, kblk)),
            pl.BlockSpec((ts, FOX_WIDTH), lambda b, s: (b * nsb + s, vblk)),
            pl.BlockSpec((ts, LANES), lambda b, s: (b * nsb + s, 0)),
            pl.BlockSpec((1, LANES), lambda b, s: (0, 0)),
        ],
        out_specs=[
            pl.BlockSpec((ts, FOX_HEADS * LANES), lambda b, s: (b * nsb + s, 0)),
            pl.BlockSpec((ts, FOX_HEADS * LANES), lambda b, s: (b * nsb + s, 0)),
            pl.BlockSpec((FOX_WIDTH, ts), lambda b, s: (b, s)),
        ],
        out_shape=[aug, aug, jax.ShapeDtypeStruct((batch * FOX_WIDTH, seq), BF16)],
        scratch_shapes=[pltpu.VMEM((1, LANES), F32)],
        compiler_params=_cparams("parallel", "arbitrary"),
        name="fox_prep",
    )(z_main, z_main, z_main, z_small, bf_row)


def _fox_attn_kernel(q_ref, k_ref, vt_ref, gate_ref, o_ref, m_scr, l_scr, acc_scr, *, tq, tk):
    qi = pl.program_id(2)
    m_scr[...] = jnp.full_like(m_scr, -jnp.inf)
    l_scr[...] = jnp.zeros_like(l_scr)
    acc_scr[...] = jnp.zeros_like(acc_scr)

    def scores(j, hh):
        k0 = pl.multiple_of(j * tk, tk)
        q = q_ref[:, hh * LANES:(hh + 1) * LANES]
        k = k_ref[pl.ds(k0, tk), hh * LANES:(hh + 1) * LANES]
        return lax.dot_general(k, q, (((1,), (1,)), ((), ())),
                               preferred_element_type=F32)

    def absorb(j, hh, st, stats, masked):
        m_prev, l_prev, acc = stats
        if masked:
            k_pos = lax.broadcasted_iota(jnp.int32, (tk, tq), 0)
            q_pos = lax.broadcasted_iota(jnp.int32, (tk, tq), 1)
            st = jnp.where(k_pos <= q_pos, st, -jnp.inf)
        m_new = jnp.maximum(m_prev, jnp.max(st, axis=0, keepdims=True))
        alpha = jnp.exp2(m_prev - m_new)
        p = jnp.exp2(st - m_new)
        l_new = alpha * l_prev + jnp.sum(p, axis=0, keepdims=True)
        k0 = pl.multiple_of(j * tk, tk)
        vt = vt_ref[hh * FOX_DH:(hh + 1) * FOX_DH, pl.ds(k0, tk)]
        pv = jnp.dot(vt, p.astype(BF16), preferred_element_type=F32)
        return m_new, l_new, acc * alpha + pv

    def run(chunks, diag_last):
        sts = [[scores(j, hh) for hh in range(2)] for j in chunks]
        stats = [(m_scr[hh], l_scr[hh], acc_scr[hh]) for hh in range(2)]
        for ci, j in enumerate(chunks):
            for hh in range(2):
                stats[hh] = absorb(j, hh, sts[ci][hh], stats[hh],
                                   masked=diag_last and ci == len(chunks) - 1)
        for hh in range(2):
            m_scr[hh], l_scr[hh], acc_scr[hh] = stats[hh]

    def pair(jp, carry):
        run([2 * jp, 2 * jp + 1], False)
        return carry

    lax.fori_loop(0, qi // 2, pair, 0)

    @pl.when(qi % 2 == 1)
    def _():
        run([qi - 1, qi], True)

    @pl.when(qi % 2 == 0)
    def _():
        run([qi], True)

    ot = jnp.concatenate([acc_scr[0] / l_scr[0], acc_scr[1] / l_scr[1]], axis=0)
    gt = gate_ref[...].astype(F32)
    o_ref[...] = (jnp.transpose(ot) * (gt * _sigmoid(gt))).astype(o_ref.dtype)


def _fox_attn(q_aug, k_aug, v_t, z_main, *, batch, seq, tq):
    T = z_main.shape[0]
    tk = tq
    nq = seq // tq
    npair = FOX_HEADS // 2
    gblk = ZM_FGATE // LANES
    return pl.pallas_call(
        functools.partial(_fox_attn_kernel, tq=tq, tk=tk),
        grid=(batch, npair, nq),
        in_specs=[
            pl.BlockSpec((tq, 2 * LANES), lambda b, p, i: (b * nq + i, p)),
            pl.BlockSpec((seq, 2 * LANES), lambda b, p, i: (b, p)),
            pl.BlockSpec((2 * FOX_DH, seq), lambda b, p, i: (b * npair + p, 0)),
            pl.BlockSpec((tq, LANES), lambda b, p, i: (b * nq + i, gblk + p)),
        ],
        out_specs=pl.BlockSpec((tq, LANES), lambda b, p, i: (b * nq + i, p)),
        out_shape=jax.ShapeDtypeStruct((T, FOX_WIDTH), BF16),
        scratch_shapes=[
            pltpu.VMEM((2, 1, tq), F32),
            pltpu.VMEM((2, 1, tq), F32),
            pltpu.VMEM((2, FOX_DH, tq), F32),
        ],
        compiler_params=_cparams("parallel", "parallel", "arbitrary"),
        name="fox_attn",
    )(q_aug, k_aug, v_t, z_main)


def _memkv_kernel(mem_ref, gain_ref, w_ref, o_ref):
    x = mem_ref[...]
    ms = jnp.mean(x * x, axis=-1, keepdims=True)
    h = (x * lax.rsqrt(ms + EPS) * gain_ref[...]).astype(BF16)
    o_ref[...] = jnp.dot(h, w_ref[...], preferred_element_type=F32).astype(o_ref.dtype)


def _memkv(mem2, gain, w_kv, *, tm):
    R, D = mem2.shape
    N = w_kv.shape[1]
    return pl.pallas_call(
        _memkv_kernel,
        grid=(R // tm,),
        in_specs=[
            pl.BlockSpec((tm, D), lambda i: (i, 0)),
            pl.BlockSpec((1, D), lambda i: (0, 0)),
            pl.BlockSpec((D, N), lambda i: (0, 0)),
        ],
        out_specs=pl.BlockSpec((tm, N), lambda i: (i, 0)),
        out_shape=jax.ShapeDtypeStruct((R, N), BF16),
        compiler_params=_cparams("parallel"),
        name="memkv",
    )(mem2, gain, w_kv)


def _mem_attn_kernel(q_ref, gate_ref, mk_ref, mv_ref, o_ref):
    scale = MEM_DH ** -0.5
    outs = []
    for h in range(MEM_HEADS):
        sl = slice(h * MEM_DH, (h + 1) * MEM_DH)
        s = lax.dot_general(q_ref[:, sl], mk_ref[:, sl], (((1,), (1,)), ((), ())),
                            preferred_element_type=F32) * scale
        m = jnp.max(s, axis=-1, keepdims=True)
        p = jnp.exp(s - m)
        l = jnp.sum(p, axis=-1, keepdims=True)
        p = (p / l).astype(BF16)
        outs.append(jnp.dot(p, mv_ref[:, sl], preferred_element_type=F32))
    o = jnp.concatenate(outs, axis=1)
    gt = gate_ref[...].astype(F32)
    o_ref[...] = (o * (gt * _sigmoid(gt))).astype(o_ref.dtype)


def _mem_attn(z_main, mkv, *, batch, seq, mem_len, tm):
    T = z_main.shape[0]
    nsb = seq // tm
    qblk = ZM_MQ // MEM_WIDTH
    gblk = ZM_MGATE // MEM_WIDTH
    return pl.pallas_call(
        _mem_attn_kernel,
        grid=(batch, nsb),
        in_specs=[
            pl.BlockSpec((tm, MEM_WIDTH), lambda b, s: (b * nsb + s, qblk)),
            pl.BlockSpec((tm, MEM_WIDTH), lambda b, s: (b * nsb + s, gblk)),
            pl.BlockSpec((mem_len, MEM_WIDTH), lambda b, s: (b, 0)),
            pl.BlockSpec((mem_len, MEM_WIDTH), lambda b, s: (b, 1)),
        ],
        out_specs=pl.BlockSpec((tm, MEM_WIDTH), lambda b, s: (b * nsb + s, 0)),
        out_shape=jax.ShapeDtypeStruct((T, MEM_WIDTH), BF16),
        compiler_params=_cparams("parallel", "parallel"),
        name="mem_attn",
    )(z_main, z_main, mkv, mkv)


def _merge_kernel(oa_ref, ob_ref, oc_ref, g0_ref, g1_ref, g2_ref, wa_ref, wb_ref, wc_ref, y_ref):
    dot = functools.partial(jnp.dot, preferred_element_type=F32)
    y = _sigmoid(g0_ref[...].astype(F32)) * dot(oa_ref[...], wa_ref[...])
    y = y + _sigmoid(g1_ref[...].astype(F32)) * dot(ob_ref[...], wb_ref[...])
    y = y + _sigmoid(g2_ref[...].astype(F32)) * dot(oc_ref[...], wc_ref[...])
    y_ref[...] = y.astype(y_ref.dtype)


def _merge(o_a, o_b, o_c, z_main, w_branch, *, tm, tn):
    T = o_a.shape[0]
    D = w_branch.shape[1]
    gbase = ZM_MERGE // tn
    gstep = D // tn
    ra = GLA_WIDTH // FOX_WIDTH
    return pl.pallas_call(
        _merge_kernel,
        grid=(T // tm, D // tn),
        in_specs=[
            pl.BlockSpec((tm, GLA_WIDTH), lambda i, j: (i, 0)),
            pl.BlockSpec((tm, FOX_WIDTH), lambda i, j: (i, 0)),
            pl.BlockSpec((tm, MEM_WIDTH), lambda i, j: (i, 0)),
            pl.BlockSpec((tm, tn), lambda i, j: (i, gbase + j)),
            pl.BlockSpec((tm, tn), lambda i, j: (i, gbase + gstep + j)),
            pl.BlockSpec((tm, tn), lambda i, j: (i, gbase + 2 * gstep + j)),
            pl.BlockSpec((GLA_WIDTH, tn), lambda i, j: (0, j)),
            pl.BlockSpec((FOX_WIDTH, tn), lambda i, j: (ra, j)),
            pl.BlockSpec((MEM_WIDTH, tn), lambda i, j: (ra + 1, j)),
        ],
        out_specs=pl.BlockSpec((tm, tn), lambda i, j: (i, j)),
        out_shape=jax.ShapeDtypeStruct((T, D), BF16),
        compiler_params=_cparams("parallel", "arbitrary"),
        name="merge",
    )(o_a, o_b, o_c, z_main, z_main, z_main, w_branch, w_branch, w_branch)


def _outproj_kernel(x_ref, y_ref, w_ref, gain_ref, o_ref, *, final_norm):
    x = x_ref[...] + jnp.dot(y_ref[...], w_ref[...], preferred_element_type=F32)
    if final_norm:
        ms = jnp.mean(x * x, axis=-1, keepdims=True)
        x = x * lax.rsqrt(ms + EPS) * gain_ref[...]
    o_ref[...] = x


def _outproj(x2, y, w_out, final_gain, *, tm, final_norm):
    T, D = x2.shape
    return pl.pallas_call(
        functools.partial(_outproj_kernel, final_norm=final_norm),
        grid=(T // tm,),
        in_specs=[
            pl.BlockSpec((tm, D), lambda i: (i, 0)),
            pl.BlockSpec((tm, D), lambda i: (i, 0)),
            pl.BlockSpec((D, D), lambda i: (0, 0)),
            pl.BlockSpec((1, D), lambda i: (0, 0)),
        ],
        out_specs=pl.BlockSpec((tm, D), lambda i: (i, 0)),
        out_shape=jax.ShapeDtypeStruct((T, D), F32),
        compiler_params=_cparams("parallel"),
        name="outproj",
    )(x2, y, w_out, final_gain)


W_GDOWN = 2 * GLA_KWIDTH + 2 * GLA_WIDTH
W_FQ = W_GDOWN + GK_RANK
W_FLOGIT = W_FQ + 3 * FOX_WIDTH
W_FGATE = W_FLOGIT + FOX_HEADS


def _pack_src_row(r):
    return (r + jnp.where(r >= W_GDOWN, GK_RANK, 0)
            + jnp.where(r + GK_RANK >= W_FLOGIT, FOX_HEADS, 0))


def _pack_copy(wt_hbm, buf, sem, step, *, tr, nblk):
    layer = step // nblk
    row = pl.multiple_of(_pack_src_row((step % nblk) * tr), 8)
    slot = step % 2
    return pltpu.make_async_copy(wt_hbm.at[layer, pl.ds(row, tr), :], buf.at[slot], sem.at[slot])


def _pack_kernel(wt_hbm, wm_ref, buf, sem, *, tr, nblk):
    s = pl.program_id(0)
    copy = functools.partial(_pack_copy, wt_hbm, buf, sem, tr=tr, nblk=nblk)

    @pl.when(s == 0)
    def _():
        copy(s).start()

    @pl.when(s + 1 < pl.num_programs(0))
    def _():
        copy(s + 1).start()

    copy(s).wait()
    wm_ref[...] = buf[s % 2].astype(BF16)


def _pack_w_in(w_in, *, tr):
    depth, d, in_cols = w_in.shape
    n_main = in_cols - GK_RANK - FOX_HEADS
    nblk = n_main // tr
    wt = jnp.swapaxes(w_in, 1, 2)
    w_main_t = pl.pallas_call(
        functools.partial(_pack_kernel, tr=tr, nblk=nblk),
        grid=(depth * nblk,),
        in_specs=[pl.BlockSpec(memory_space=pl.ANY)],
        out_specs=pl.BlockSpec((None, tr, d), lambda s: (s // nblk, s % nblk, 0)),
        out_shape=jax.ShapeDtypeStruct((depth, n_main, d), BF16),
        scratch_shapes=[pltpu.VMEM((2, tr, d), F32), pltpu.SemaphoreType.DMA((2,))],
        compiler_params=_cparams("arbitrary"),
        name="pack_w_in",
    )(wt)
    pad = jnp.zeros((depth, LANES - GD_COPIES * GK_RANK - FOX_HEADS, d), w_in.dtype)
    w_small_t = jnp.concatenate([wt[:, W_GDOWN:W_FQ, :]] * GD_COPIES
                                + [wt[:, W_FLOGIT:W_FGATE, :], pad], axis=1).astype(BF16)
    return w_main_t, w_small_t


def _trunk(x, mem, norm_gain, w_in, w_gk_up, b_gk, gla_norm_gain, b_f, mem_norm_gain,
           w_mem_kv, w_branch, w_out, final_gain, *, tiles):
    B, S, D = x.shape
    M = mem.shape[1]
    depth = w_in.shape[0]
    x2 = x.reshape(B * S, D)
    mem2 = mem.reshape(B * M, D)
    w_main, w_small = _pack_w_in(w_in, tr=tiles["pack_tr"])
    for l in range(depth):
        z_main, z_small = _inproj(x2, norm_gain[l][None, :], w_main, w_small,
                                  layer=l, tm=tiles["in_tm"], tn=tiles["in_tn"])
        o_a = _gla(z_main, z_small, w_gk_up[l], b_gk[l][None, :], gla_norm_gain[l][None, :],
                   batch=B, seq=S, ts=tiles["gla_ts"], unroll=tiles["gla_unroll"])
        bf_row = jnp.zeros((1, LANES), F32).at[0, ZS_FLOGIT:ZS_FLOGIT + FOX_HEADS].set(b_f[l])
        q_aug, k_aug, v_t = _fox_prep(z_main, z_small, bf_row, batch=B, seq=S, ts=tiles["prep_ts"])
        o_b = _fox_attn(q_aug, k_aug, v_t, z_main, batch=B, seq=S, tq=tiles["fox_tq"])
        mkv = _memkv(mem2, mem_norm_gain[l][None, :], w_mem_kv[l].astype(BF16), tm=tiles["memkv_tm"])
        o_c = _mem_attn(z_main, mkv, batch=B, seq=S, mem_len=M, tm=tiles["mem_tm"])
        y = _merge(o_a, o_b, o_c, z_main, w_branch[l].astype(BF16),
                   tm=tiles["merge_tm"], tn=tiles["merge_tn"])
        x2 = _outproj(x2, y, w_out[l].astype(BF16), final_gain[None, :],
                      tm=tiles["out_tm"], final_norm=(l == depth - 1))
    return x2.reshape(B, S, D)


_TILES = dict(pack_tr=256, in_tm=1024, in_tn=1536, gla_ts=512, gla_unroll=8, prep_ts=512, fox_tq=512,
              memkv_tm=256, mem_tm=512, merge_tm=1024, merge_tn=512, out_tm=512)


def kernel(x, mem, norm_gain, w_in, w_gk_up, b_gk, gla_norm_gain, b_f, mem_norm_gain,
           w_mem_kv, w_branch, w_out, final_gain):
    return _trunk(x, mem, norm_gain, w_in, w_gk_up, b_gk, gla_norm_gain, b_f, mem_norm_gain,
                  w_mem_kv, w_branch, w_out, final_gain, tiles=_TILES)
```

```python
import functools

import jax
import jax.numpy as jnp
import numpy as np
from jax import lax
from jax.experimental import pallas as pl
from jax.experimental.pallas import tpu as pltpu

F32 = jnp.float32
BF16 = jnp.bfloat16

EPS = 1e-6

GLA_HEADS, GLA_DK, GLA_DV = 4, 128, 256
GLA_KWIDTH = GLA_HEADS * GLA_DK
GLA_WIDTH = GLA_HEADS * GLA_DV
GK_RANK = 16
GK_NORMALIZER = 16.0
FOX_HEADS, FOX_DH = 8, 64
FOX_WIDTH = FOX_HEADS * FOX_DH
MEM_HEADS, MEM_DH = 4, 128
MEM_WIDTH = MEM_HEADS * MEM_DH
N_BRANCH = 3

LANES = 128
VMEM_LIMIT_BYTES = 56 * 1024 * 1024

ZM_GQ = 0
ZM_GK = ZM_GQ + GLA_KWIDTH
ZM_GV = ZM_GK + GLA_KWIDTH
ZM_GGATE = ZM_GV + GLA_WIDTH
ZM_FQ = ZM_GGATE + GLA_WIDTH
ZM_FK = ZM_FQ + FOX_WIDTH
ZM_FV = ZM_FK + FOX_WIDTH
ZM_FGATE = ZM_FV + FOX_WIDTH
ZM_MQ = ZM_FGATE + FOX_WIDTH
ZM_MGATE = ZM_MQ + MEM_WIDTH
ZM_MERGE = ZM_MGATE + MEM_WIDTH
GD_COPIES = 6
ZS_FLOGIT = GD_COPIES * GK_RANK

GLA_CHUNK = 64
GLA_SUB = 8
FOX_EXTRA = 6
FOX_VROWS = 80
LOG2E = 1.4426950408889634


def _cparams(*sem):
    return pltpu.CompilerParams(dimension_semantics=sem, vmem_limit_bytes=VMEM_LIMIT_BYTES)


def _sigmoid(x):
    return 0.5 * jnp.tanh(0.5 * x) + 0.5


def _log_sigmoid(x):
    return jnp.minimum(x, 0.0) - jnp.log(1.0 + jnp.exp(-jnp.abs(x)))


def _split3(x):
    a = x.astype(BF16)
    r = x - a.astype(F32)
    b = r.astype(BF16)
    c = (r - b.astype(F32)).astype(BF16)
    return a, b, c


def _dot_nt(a, b):
    return lax.dot_general(a, b, (((1,), (1,)), ((), ())), preferred_element_type=F32)


def _tri_cumsum(tri_bf, x):
    a, b, c = _split3(x)
    dot = functools.partial(jnp.dot, preferred_element_type=F32)
    return dot(tri_bf, a) + dot(tri_bf, b) + dot(tri_bf, c)


def _inproj_kernel(x_ref, gain_ref, wm_ref, ws_ref, zm_ref, zs_ref, h_scr):
    @pl.when(pl.program_id(1) == 0)
    def _():
        x = x_ref[...]
        ms = jnp.mean(x * x, axis=-1, keepdims=True)
        h = (x * lax.rsqrt(ms + EPS) * gain_ref[...]).astype(BF16)
        h_scr[...] = h
        zs_ref[...] = _dot_nt(h, ws_ref[...])

    zm_ref[...] = _dot_nt(h_scr[...], wm_ref[...]).astype(BF16)


def _inproj(x2, gain, w_main_t, w_small_t, *, layer, tm, tn):
    T, D = x2.shape
    NM = w_main_t.shape[1]
    return pl.pallas_call(
        _inproj_kernel,
        grid=(T // tm, NM // tn),
        in_specs=[
            pl.BlockSpec((tm, D), lambda i, j: (i, 0)),
            pl.BlockSpec((1, D), lambda i, j: (0, 0)),
            pl.BlockSpec((None, tn, D), lambda i, j: (layer, j, 0)),
            pl.BlockSpec((None, LANES, D), lambda i, j: (layer, 0, 0)),
        ],
        out_specs=[
            pl.BlockSpec((tm, tn), lambda i, j: (i, j)),
            pl.BlockSpec((tm, LANES), lambda i, j: (i, 0)),
        ],
        out_shape=[
            jax.ShapeDtypeStruct((T, NM), BF16),
            jax.ShapeDtypeStruct((T, LANES), F32),
        ],
        scratch_shapes=[pltpu.VMEM((tm, D), BF16)],
        compiler_params=_cparams("parallel", "arbitrary"),
        name="inproj",
    )(x2, gain, w_main_t, w_small_t)


def _gla_kernel(q_ref, k_ref, v_ref, zs_ref, gate_ref, wup_ref, bgk_ref, gain_ref,
                o_ref, state_scr, b_scr, p_scr, *, nchunks, unroll):
    C, SUB = GLA_CHUNK, GLA_SUB
    NSUB = C // SUB
    NPAIR = SUB // 2
    DK, DV = GLA_DK, GLA_DV
    dot = functools.partial(jnp.dot, preferred_element_type=F32)

    @pl.when(pl.program_id(2) == 0)
    def _():
        state_scr[...] = jnp.zeros_like(state_scr)

    row = lax.broadcasted_iota(jnp.int32, (C, C), 0)
    col = lax.broadcasted_iota(jnp.int32, (C, C), 1)
    tri = (col <= row).astype(BF16)
    same_sub = (row // SUB) == (col // SUB)
    causal = col <= row

    z1, z2, z3 = _split3(zs_ref[...])
    zlane = lax.broadcasted_iota(jnp.int32, z1.shape, 1)
    lhs = jnp.where(zlane < 3 * GK_RANK, z1, jnp.where(zlane < 5 * GK_RANK, z2, z3))
    w1, w2, w3 = _split3(wup_ref[...])
    rhs = jnp.concatenate([w1, w2, w3, w1, w2, w1,
                           jnp.zeros((LANES - GD_COPIES * GK_RANK, DK), BF16)], axis=0)
    logits = dot(lhs, rhs) + bgk_ref[...]
    g = _log_sigmoid(logits) * (LOG2E / GK_NORMALIZER)
    for c in range(nchunks):
        b_scr[c * C:(c + 1) * C, :] = _tri_cumsum(tri, g[c * C:(c + 1) * C, :])

    cs_r = lax.broadcasted_iota(jnp.int32, (2 * DK, LANES), 0)
    cs_c = lax.broadcasted_iota(jnp.int32, (2 * DK, LANES), 1)
    chansum = ((cs_r // DK) == (cs_c % 2)).astype(BF16)
    pair_of_lane = (lax.broadcasted_iota(jnp.int32, (C, LANES), 1) % SUB) // 2
    scale = DK ** -0.5
    gain = gain_ref[...]

    def chunk(c, p_slot, state):
        r0 = pl.multiple_of(c * C, C)
        rows = pl.ds(r0, C)
        q = q_ref[rows, :].astype(F32) * scale
        k = k_ref[rows, :].astype(F32)
        v = v_ref[rows, :]
        b = b_scr[rows, :]
        b_last = b[C - 1:C, :]

        o = dot((q * jnp.exp2(b)).astype(BF16), state.astype(BF16))

        cross = [jnp.zeros((SUB, C), F32)]
        for i in range(1, NSUB):
            n = i * SUB
            bref = b[n - 1:n, :]
            q_t = (q[n:n + SUB, :] * jnp.exp2(b[n:n + SUB, :] - bref)).astype(BF16)
            k_t = (k[:n, :] * jnp.exp2(bref - b[:n, :])).astype(BF16)
            k_t = jnp.concatenate([k_t, jnp.zeros((C - n, DK), BF16)], axis=0)
            cross.append(lax.dot_general(q_t, k_t, (((1,), (1,)), ((), ())),
                                         preferred_element_type=F32))
        a_cross = jnp.concatenate(cross, axis=0)

        k3 = k.reshape(NSUB, SUB, DK)
        b3 = b.reshape(NSUB, SUB, DK)
        for j in range(SUB):
            kj = jnp.broadcast_to(k3[:, j:j + 1, :], (NSUB, SUB, DK)).reshape(C, DK)
            bj = jnp.broadcast_to(b3[:, j:j + 1, :], (NSUB, SUB, DK)).reshape(C, DK)
            p = q * kj * jnp.exp2(jnp.minimum(b - bj, 0.0))
            p_slot[(j // 2) * C:(j // 2 + 1) * C, (j % 2) * DK:(j % 2 + 1) * DK] = p.astype(BF16)
        d = dot(p_slot[...], chansum)
        a_diag = d[0:C, :]
        for t in range(1, NPAIR):
            a_diag = jnp.where(pair_of_lane == t, d[t * C:(t + 1) * C, :], a_diag)

        a = jnp.where(causal, jnp.where(same_sub, a_diag[:, :C], a_cross), 0.0)
        o = o + dot(a.astype(BF16), v)

        k_dec = (k * jnp.exp2(b_last - b)).astype(BF16)
        upd = lax.dot_general(k_dec, v, (((0,), (0,)), ((), ())),
                              preferred_element_type=F32)
        decay_col = jnp.transpose(jnp.broadcast_to(jnp.exp2(b_last), (DK, DK)))[:, 0:1]
        new_state = state * decay_col + upd

        ms = jnp.mean(o * o, axis=-1, keepdims=True)
        on = o * lax.rsqrt(ms + EPS) * gain
        gt = gate_ref[rows, :].astype(F32)
        o_ref[rows, :] = (on * (gt * _sigmoid(gt))).astype(o_ref.dtype)
        return new_state

    def group(gi, carry):
        state = state_scr[...]
        for u in range(unroll):
            state = chunk(gi * unroll + u, p_scr.at[u], state)
        state_scr[...] = state
        return carry

    lax.fori_loop(0, nchunks // unroll, group, 0)


def _gla(z_main, z_small, w_up, b_gk, gain, *, batch, seq, ts, unroll):
    T = z_main.shape[0]
    nsb = seq // ts
    nchunks = ts // GLA_CHUNK
    rowmap = lambda b, h, s: b * nsb + s
    kblk = ZM_GK // GLA_DK
    vblk = ZM_GV // GLA_DV
    gblk = ZM_GGATE // GLA_DV
    return pl.pallas_call(
        functools.partial(_gla_kernel, nchunks=nchunks, unroll=unroll),
        grid=(batch, GLA_HEADS, nsb),
        in_specs=[
            pl.BlockSpec((ts, GLA_DK), lambda b, h, s: (rowmap(b, h, s), h)),
            pl.BlockSpec((ts, GLA_DK), lambda b, h, s: (rowmap(b, h, s), kblk + h)),
            pl.BlockSpec((ts, GLA_DV), lambda b, h, s: (rowmap(b, h, s), vblk + h)),
            pl.BlockSpec((ts, LANES), lambda b, h, s: (rowmap(b, h, s), 0)),
            pl.BlockSpec((ts, GLA_DV), lambda b, h, s: (rowmap(b, h, s), gblk + h)),
            pl.BlockSpec((GK_RANK, GLA_DK), lambda b, h, s: (0, h)),
            pl.BlockSpec((1, GLA_DK), lambda b, h, s: (0, h)),
            pl.BlockSpec((1, GLA_DV), lambda b, h, s: (0, 0)),
        ],
        out_specs=pl.BlockSpec((ts, GLA_DV), lambda b, h, s: (rowmap(b, h, s), h)),
        out_shape=jax.ShapeDtypeStruct((T, GLA_WIDTH), BF16),
        scratch_shapes=[
            pltpu.VMEM((GLA_DK, GLA_DV), F32),
            pltpu.VMEM((ts, GLA_DK), F32),
            pltpu.VMEM((unroll, GLA_SUB // 2 * GLA_CHUNK, 2 * GLA_DK), BF16),
        ],
        compiler_params=_cparams("parallel", "parallel", "arbitrary"),
        name="gla",
    )(z_main, z_main, z_main, z_small, z_main, w_up, b_gk, gain)


def _fox_extra_tables():
    npair = FOX_HEADS // 2
    pq = np.zeros((3 * LANES, npair * LANES), np.float32)
    pk = np.zeros((3 * LANES, npair * LANES), np.float32)
    cq = np.zeros((1, npair * LANES), np.float32)
    ck = np.zeros((1, npair * LANES), np.float32)
    for h in range(FOX_HEADS):
        base = (h // 2) * LANES + (h % 2) * FOX_EXTRA
        for t in range(3):
            pq[t * LANES + ZS_FLOGIT + h, base + t] = 1.0
            cq[0, base + 3 + t] = 1.0
            ck[0, base + t] = 1.0
            pk[t * LANES + ZS_FLOGIT + h, base + 3 + t] = -1.0
    return jnp.asarray(pq, BF16), jnp.asarray(pk, BF16), jnp.asarray(cq), jnp.asarray(ck)


def _fox_prep_kernel(v_ref, zs_ref, bf_ref, pq_ref, pk_ref, cq_ref, ck_ref,
                     qx_ref, kx_ref, vt_ref, carry_scr, *, ts):
    @pl.when(pl.program_id(1) == 0)
    def _():
        carry_scr[...] = jnp.zeros_like(carry_scr)

    dot = functools.partial(jnp.dot, preferred_element_type=F32)
    row = lax.broadcasted_iota(jnp.int32, (ts, ts), 0)
    col = lax.broadcasted_iota(jnp.int32, (ts, ts), 1)
    tri = (col <= row).astype(BF16)
    log_f = _log_sigmoid(zs_ref[...] + bf_ref[...])
    c_all = _tri_cumsum(tri, log_f) + carry_scr[...]
    carry_scr[...] = c_all[ts - 1:ts, :]

    cs = jnp.concatenate(_split3(c_all * LOG2E), axis=1)
    qx_ref[...] = (dot(cs, pq_ref[...]) + cq_ref[...]).astype(BF16)
    kx_ref[...] = (dot(cs, pk_ref[...]) + ck_ref[...]).astype(BF16)

    v_t = jnp.transpose(v_ref[...].astype(F32))
    pad_rows = FOX_VROWS - FOX_DH
    ones_blk = jnp.where(lax.broadcasted_iota(jnp.int32, (pad_rows, ts), 0) == 0, 1.0, 0.0)
    pieces = []
    for h in range(FOX_HEADS):
        pieces += [v_t[h * FOX_DH:(h + 1) * FOX_DH, :], ones_blk]
    vt_ref[...] = jnp.concatenate(pieces, axis=0).astype(BF16)


def _fox_prep(z_main, z_small, bf_row, *, batch, seq, ts):
    T = z_main.shape[0]
    nsb = seq // ts
    vblk = ZM_FV // FOX_WIDTH
    npair = FOX_HEADS // 2
    extra = jax.ShapeDtypeStruct((T, npair * LANES), BF16)
    pq, pk, cq, ck = _fox_extra_tables()
    const = lambda b, s: (0, 0)
    return pl.pallas_call(
        functools.partial(_fox_prep_kernel, ts=ts),
        grid=(batch, nsb),
        in_specs=[
            pl.BlockSpec((ts, FOX_WIDTH), lambda b, s: (b * nsb + s, vblk)),
            pl.BlockSpec((ts, LANES), lambda b, s: (b * nsb + s, 0)),
            pl.BlockSpec((1, LANES), const),
            pl.BlockSpec(pq.shape, const),
            pl.BlockSpec(pk.shape, const),
            pl.BlockSpec(cq.shape, const),
            pl.BlockSpec(ck.shape, const),
        ],
        out_specs=[
            pl.BlockSpec((ts, npair * LANES), lambda b, s: (b * nsb + s, 0)),
            pl.BlockSpec((ts, npair * LANES), lambda b, s: (b * nsb + s, 0)),
            pl.BlockSpec((FOX_HEADS * FOX_VROWS, ts), lambda b, s: (b, s)),
        ],
        out_shape=[extra, extra,
                   jax.ShapeDtypeStruct((batch * FOX_HEADS * FOX_VROWS, seq), BF16)],
        scratch_shapes=[pltpu.VMEM((1, LANES), F32)],
        compiler_params=_cparams("parallel", "arbitrary"),
        name="fox_prep",
    )(z_main, z_small, bf_row, pq, pk, cq, ck)


def _fox_attn_kernel(q_ref, qx_ref, k_ref, kx_ref, vt_ref, gate_ref, o_ref, m_scr, acc_scr, *, tq, tk):
    qi = pl.program_id(2)
    m_scr[...] = jnp.full_like(m_scr, -jnp.inf)
    acc_scr[...] = jnp.zeros_like(acc_scr)

    q_scaled = (q_ref[...].astype(F32) * (FOX_DH ** -0.5 * LOG2E)).astype(BF16)
    q2 = jnp.concatenate([q_scaled, qx_ref[...]], axis=1)
    lane = lax.broadcasted_iota(jnp.int32, (1, 2 * LANES), 1)
    head_lanes = [((lane >= hh * FOX_DH) & (lane < (hh + 1) * FOX_DH))
                  | ((lane >= LANES + hh * FOX_EXTRA) & (lane < LANES + (hh + 1) * FOX_EXTRA))
                  for hh in range(2)]

    def scores(j, hh):
        k0 = pl.multiple_of(j * tk, tk)
        k2 = jnp.concatenate([k_ref[pl.ds(k0, tk), :], kx_ref[pl.ds(k0, tk), :]], axis=1)
        k_h = jnp.where(head_lanes[hh], k2, jnp.zeros_like(k2))
        return _dot_nt(k_h, q2)

    def absorb(j, hh, st, stats, masked):
        m_prev, acc = stats
        if masked:
            k_pos = lax.broadcasted_iota(jnp.int32, (tk, tq), 0)
            q_pos = lax.broadcasted_iota(jnp.int32, (tk, tq), 1)
            st = jnp.where(k_pos <= q_pos, st, -jnp.inf)
        m_new = jnp.maximum(m_prev, jnp.max(st, axis=0, keepdims=True))
        alpha = jnp.exp2(m_prev - m_new)
        p = jnp.exp2(st - m_new).astype(BF16)
        k0 = pl.multiple_of(j * tk, tk)
        vt = vt_ref[hh * FOX_VROWS:(hh + 1) * FOX_VROWS, pl.ds(k0, tk)]
        pv = jnp.dot(vt, p, preferred_element_type=F32)
        return m_new, acc * alpha + pv

    def run(chunks, diag_last):
        sts = [[scores(j, hh) for hh in range(2)] for j in chunks]
        stats = [(m_scr[hh], acc_scr[hh]) for hh in range(2)]
        for ci, j in enumerate(chunks):
            for hh in range(2):
                stats[hh] = absorb(j, hh, sts[ci][hh], stats[hh],
                                   masked=diag_last and ci == len(chunks) - 1)
        for hh in range(2):
            m_scr[hh], acc_scr[hh] = stats[hh]

    def pair(jp, carry):
        run([2 * jp, 2 * jp + 1], False)
        return carry

    lax.fori_loop(0, qi // 2, pair, 0)

    @pl.when(qi % 2 == 1)
    def _():
        run([qi - 1, qi], True)

    @pl.when(qi % 2 == 0)
    def _():
        run([qi], True)

    outs = []
    for hh in range(2):
        acc = acc_scr[hh]
        outs.append(acc[:FOX_DH, :] / acc[FOX_DH:FOX_DH + 1, :])
    ot = jnp.concatenate(outs, axis=0)
    gt = gate_ref[...].astype(F32)
    o_ref[...] = (jnp.transpose(ot) * (gt * _sigmoid(gt))).astype(o_ref.dtype)


def _fox_attn(z_main, q_extra, k_extra, v_t, *, batch, seq, tq):
    T = z_main.shape[0]
    tk = tq
    nq = seq // tq
    npair = FOX_HEADS // 2
    qblk = ZM_FQ // LANES
    kblk = ZM_FK // LANES
    gblk = ZM_FGATE // LANES
    return pl.pallas_call(
        functools.partial(_fox_attn_kernel, tq=tq, tk=tk),
        grid=(batch, npair, nq),
        in_specs=[
            pl.BlockSpec((tq, LANES), lambda b, p, i: (b * nq + i, qblk + p)),
            pl.BlockSpec((tq, LANES), lambda b, p, i: (b * nq + i, p)),
            pl.BlockSpec((seq, LANES), lambda b, p, i: (b, kblk + p)),
            pl.BlockSpec((seq, LANES), lambda b, p, i: (b, p)),
            pl.BlockSpec((2 * FOX_VROWS, seq), lambda b, p, i: (b * npair + p, 0)),
            pl.BlockSpec((tq, LANES), lambda b, p, i: (b * nq + i, gblk + p)),
        ],
        out_specs=pl.BlockSpec((tq, LANES), lambda b, p, i: (b * nq + i, p)),
        out_shape=jax.ShapeDtypeStruct((T, FOX_WIDTH), BF16),
        scratch_shapes=[
            pltpu.VMEM((2, 1, tq), F32),
            pltpu.VMEM((2, FOX_VROWS, tq), F32),
        ],
        compiler_params=_cparams("parallel", "parallel", "arbitrary"),
        name="fox_attn",
    )(z_main, q_extra, z_main, k_extra, v_t, z_main)


def _memkv_kernel(mem_ref, gain_ref, w_ref, o_ref):
    x = mem_ref[...]
    ms = jnp.mean(x * x, axis=-1, keepdims=True)
    h = (x * lax.rsqrt(ms + EPS) * gain_ref[...]).astype(BF16)
    o_ref[...] = jnp.dot(h, w_ref[...], preferred_element_type=F32).astype(o_ref.dtype)


def _memkv(mem2, gain, w_kv, *, tm):
    R, D = mem2.shape
    N = w_kv.shape[1]
    return pl.pallas_call(
        _memkv_kernel,
        grid=(R // tm,),
        in_specs=[
            pl.BlockSpec((tm, D), lambda i: (i, 0)),
            pl.BlockSpec((1, D), lambda i: (0, 0)),
            pl.BlockSpec((D, N), lambda i: (0, 0)),
        ],
        out_specs=pl.BlockSpec((tm, N), lambda i: (i, 0)),
        out_shape=jax.ShapeDtypeStruct((R, N), BF16),
        compiler_params=_cparams("parallel"),
        name="memkv",
    )(mem2, gain, w_kv)


def _mem_attn_kernel(q_ref, gate_ref, mk_ref, mv_ref, o_ref):
    scale = MEM_DH ** -0.5
    outs = []
    for h in range(MEM_HEADS):
        sl = slice(h * MEM_DH, (h + 1) * MEM_DH)
        s = lax.dot_general(q_ref[:, sl], mk_ref[:, sl], (((1,), (1,)), ((), ())),
                            preferred_element_type=F32) * scale
        m = jnp.max(s, axis=-1, keepdims=True)
        p = jnp.exp(s - m)
        l = jnp.sum(p, axis=-1, keepdims=True)
        p = (p / l).astype(BF16)
        outs.append(jnp.dot(p, mv_ref[:, sl], preferred_element_type=F32))
    o = jnp.concatenate(outs, axis=1)
    gt = gate_ref[...].astype(F32)
    o_ref[...] = (o * (gt * _sigmoid(gt))).astype(o_ref.dtype)


def _mem_attn(z_main, mkv, *, batch, seq, mem_len, tm):
    T = z_main.shape[0]
    nsb = seq // tm
    qblk = ZM_MQ // MEM_WIDTH
    gblk = ZM_MGATE // MEM_WIDTH
    return pl.pallas_call(
        _mem_attn_kernel,
        grid=(batch, nsb),
        in_specs=[
            pl.BlockSpec((tm, MEM_WIDTH), lambda b, s: (b * nsb + s, qblk)),
            pl.BlockSpec((tm, MEM_WIDTH), lambda b, s: (b * nsb + s, gblk)),
            pl.BlockSpec((mem_len, MEM_WIDTH), lambda b, s: (b, 0)),
            pl.BlockSpec((mem_len, MEM_WIDTH), lambda b, s: (b, 1)),
        ],
        out_specs=pl.BlockSpec((tm, MEM_WIDTH), lambda b, s: (b * nsb + s, 0)),
        out_shape=jax.ShapeDtypeStruct((T, MEM_WIDTH), BF16),
        compiler_params=_cparams("parallel", "parallel"),
        name="mem_attn",
    )(z_main, z_main, mkv, mkv)


def _merge_kernel(oa_ref, ob_ref, oc_ref, g0_ref, g1_ref, g2_ref, wa_ref, wb_ref, wc_ref, y_ref):
    dot = functools.partial(jnp.dot, preferred_element_type=F32)
    y = _sigmoid(g0_ref[...].astype(F32)) * dot(oa_ref[...], wa_ref[...])
    y = y + _sigmoid(g1_ref[...].astype(F32)) * dot(ob_ref[...], wb_ref[...])
    y = y + _sigmoid(g2_ref[...].astype(F32)) * dot(oc_ref[...], wc_ref[...])
    y_ref[...] = y.astype(y_ref.dtype)


def _merge(o_a, o_b, o_c, z_main, w_branch, *, tm, tn):
    T = o_a.shape[0]
    D = w_branch.shape[1]
    gbase = ZM_MERGE // tn
    gstep = D // tn
    ra = GLA_WIDTH // FOX_WIDTH
    return pl.pallas_call(
        _merge_kernel,
        grid=(T // tm, D // tn),
        in_specs=[
            pl.BlockSpec((tm, GLA_WIDTH), lambda i, j: (i, 0)),
            pl.BlockSpec((tm, FOX_WIDTH), lambda i, j: (i, 0)),
            pl.BlockSpec((tm, MEM_WIDTH), lambda i, j: (i, 0)),
            pl.BlockSpec((tm, tn), lambda i, j: (i, gbase + j)),
            pl.BlockSpec((tm, tn), lambda i, j: (i, gbase + gstep + j)),
            pl.BlockSpec((tm, tn), lambda i, j: (i, gbase + 2 * gstep + j)),
            pl.BlockSpec((GLA_WIDTH, tn), lambda i, j: (0, j)),
            pl.BlockSpec((FOX_WIDTH, tn), lambda i, j: (ra, j)),
            pl.BlockSpec((MEM_WIDTH, tn), lambda i, j: (ra + 1, j)),
        ],
        out_specs=pl.BlockSpec((tm, tn), lambda i, j: (i, j)),
        out_shape=jax.ShapeDtypeStruct((T, D), BF16),
        compiler_params=_cparams("parallel", "arbitrary"),
        name="merge",
    )(o_a, o_b, o_c, z_main, z_main, z_main, w_branch, w_branch, w_branch)


def _outproj_kernel(x_ref, y_ref, w_ref, gain_ref, o_ref, *, final_norm):
    x = x_ref[...] + jnp.dot(y_ref[...], w_ref[...], preferred_element_type=F32)
    if final_norm:
        ms = jnp.mean(x * x, axis=-1, keepdims=True)
        x = x * lax.rsqrt(ms + EPS) * gain_ref[...]
    o_ref[...] = x


def _outproj(x2, y, w_out, final_gain, *, tm, final_norm):
    T, D = x2.shape
    return pl.pallas_call(
        functools.partial(_outproj_kernel, final_norm=final_norm),
        grid=(T // tm,),
        in_specs=[
            pl.BlockSpec((tm, D), lambda i: (i, 0)),
            pl.BlockSpec((tm, D), lambda i: (i, 0)),
            pl.BlockSpec((D, D), lambda i: (0, 0)),
            pl.BlockSpec((1, D), lambda i: (0, 0)),
        ],
        out_specs=pl.BlockSpec((tm, D), lambda i: (i, 0)),
        out_shape=jax.ShapeDtypeStruct((T, D), F32),
        compiler_params=_cparams("parallel"),
        name="outproj",
    )(x2, y, w_out, final_gain)


W_GDOWN = 2 * GLA_KWIDTH + 2 * GLA_WIDTH
W_FQ = W_GDOWN + GK_RANK
W_FLOGIT = W_FQ + 3 * FOX_WIDTH
W_FGATE = W_FLOGIT + FOX_HEADS


def _pack_src_row(r):
    return (r + jnp.where(r >= W_GDOWN, GK_RANK, 0)
            + jnp.where(r + GK_RANK >= W_FLOGIT, FOX_HEADS, 0))


def _pack_copy(wt_hbm, buf, sem, step, *, tr, nblk):
    layer = step // nblk
    row = pl.multiple_of(_pack_src_row((step % nblk) * tr), 8)
    slot = step % 2
    return pltpu.make_async_copy(wt_hbm.at[layer, pl.ds(row, tr), :], buf.at[slot], sem.at[slot])


def _pack_kernel(wt_hbm, wm_ref, buf, sem, *, tr, nblk):
    s = pl.program_id(0)
    copy = functools.partial(_pack_copy, wt_hbm, buf, sem, tr=tr, nblk=nblk)

    @pl.when(s == 0)
    def _():
        copy(s).start()

    @pl.when(s + 1 < pl.num_programs(0))
    def _():
        copy(s + 1).start()

    copy(s).wait()
    wm_ref[...] = buf[s % 2].astype(BF16)


def _pack_w_in(w_in, *, tr):
    depth, d, in_cols = w_in.shape
    n_main = in_cols - GK_RANK - FOX_HEADS
    nblk = n_main // tr
    wt = jnp.swapaxes(w_in, 1, 2)
    w_main_t = pl.pallas_call(
        functools.partial(_pack_kernel, tr=tr, nblk=nblk),
        grid=(depth * nblk,),
        in_specs=[pl.BlockSpec(memory_space=pl.ANY)],
        out_specs=pl.BlockSpec((None, tr, d), lambda s: (s // nblk, s % nblk, 0)),
        out_shape=jax.ShapeDtypeStruct((depth, n_main, d), BF16),
        scratch_shapes=[pltpu.VMEM((2, tr, d), F32), pltpu.SemaphoreType.DMA((2,))],
        compiler_params=_cparams("arbitrary"),
        name="pack_w_in",
    )(wt)
    pad = jnp.zeros((depth, LANES - GD_COPIES * GK_RANK - FOX_HEADS, d), w_in.dtype)
    w_small_t = jnp.concatenate([wt[:, W_GDOWN:W_FQ, :]] * GD_COPIES
                                + [wt[:, W_FLOGIT:W_FGATE, :], pad], axis=1).astype(BF16)
    return w_main_t, w_small_t


def _trunk(x, mem, norm_gain, w_in, w_gk_up, b_gk, gla_norm_gain, b_f, mem_norm_gain,
           w_mem_kv, w_branch, w_out, final_gain, *, tiles):
    B, S, D = x.shape
    M = mem.shape[1]
    depth = w_in.shape[0]
    x2 = x.reshape(B * S, D)
    mem2 = mem.reshape(B * M, D)
    w_main, w_small = _pack_w_in(w_in, tr=tiles["pack_tr"])
    for l in range(depth):
        z_main, z_small = _inproj(x2, norm_gain[l][None, :], w_main, w_small,
                                  layer=l, tm=tiles["in_tm"], tn=tiles["in_tn"])
        o_a = _gla(z_main, z_small, w_gk_up[l], b_gk[l][None, :], gla_norm_gain[l][None, :],
                   batch=B, seq=S, ts=tiles["gla_ts"], unroll=tiles["gla_unroll"])
        bf_row = jnp.zeros((1, LANES), F32).at[0, ZS_FLOGIT:ZS_FLOGIT + FOX_HEADS].set(b_f[l])
        q_extra, k_extra, v_t = _fox_prep(z_main, z_small, bf_row, batch=B, seq=S, ts=tiles["prep_ts"])
        o_b = _fox_attn(z_main, q_extra, k_extra, v_t, batch=B, seq=S, tq=tiles["fox_tq"])
        mkv = _memkv(mem2, mem_norm_gain[l][None, :], w_mem_kv[l].astype(BF16), tm=tiles["memkv_tm"])
        o_c = _mem_attn(z_main, mkv, batch=B, seq=S, mem_len=M, tm=tiles["mem_tm"])
        y = _merge(o_a, o_b, o_c, z_main, w_branch[l].astype(BF16),
                   tm=tiles["merge_tm"], tn=tiles["merge_tn"])
        x2 = _outproj(x2, y, w_out[l].astype(BF16), final_gain[None, :],
                      tm=tiles["out_tm"], final_norm=(l == depth - 1))
    return x2.reshape(B, S, D)


_TILES = dict(pack_tr=256, in_tm=1024, in_tn=1536, gla_ts=512, gla_unroll=8, prep_ts=512, fox_tq=512,
              memkv_tm=256, mem_tm=512, merge_tm=1024, merge_tn=512, out_tm=512)


def kernel(x, mem, norm_gain, w_in, w_gk_up, b_gk, gla_norm_gain, b_f, mem_norm_gain,
           w_mem_kv, w_branch, w_out, final_gain):
    return _trunk(x, mem, norm_gain, w_in, w_gk_up, b_gk, gla_norm_gain, b_f, mem_norm_gain,
                  w_mem_kv, w_branch, w_out, final_gain, tiles=_TILES)
```

```python
import functools

import jax
import jax.numpy as jnp
import numpy as np
from jax import lax
from jax.experimental import pallas as pl
from jax.experimental.pallas import tpu as pltpu

F32 = jnp.float32
BF16 = jnp.bfloat16

EPS = 1e-6

GLA_HEADS, GLA_DK, GLA_DV = 4, 128, 256
GLA_KWIDTH = GLA_HEADS * GLA_DK
GLA_WIDTH = GLA_HEADS * GLA_DV
GK_RANK = 16
GK_NORMALIZER = 16.0
FOX_HEADS, FOX_DH = 8, 64
FOX_WIDTH = FOX_HEADS * FOX_DH
MEM_HEADS, MEM_DH = 4, 128
MEM_WIDTH = MEM_HEADS * MEM_DH
N_BRANCH = 3

LANES = 128
VMEM_LIMIT_BYTES = 56 * 1024 * 1024

ZM_GQ = 0
ZM_GK = ZM_GQ + GLA_KWIDTH
ZM_GV = ZM_GK + GLA_KWIDTH
ZM_GGATE = ZM_GV + GLA_WIDTH
ZM_FQ = ZM_GGATE + GLA_WIDTH
ZM_FK = ZM_FQ + FOX_WIDTH
ZM_FV = ZM_FK + FOX_WIDTH
ZM_FGATE = ZM_FV + FOX_WIDTH
ZM_MQ = ZM_FGATE + FOX_WIDTH
ZM_MGATE = ZM_MQ + MEM_WIDTH
ZM_MERGE = ZM_MGATE + MEM_WIDTH
GD_COPIES = 6
ZS_FLOGIT = GD_COPIES * GK_RANK

GLA_CHUNK = 64
GLA_SUB = 8
FOX_EXTRA = 6
FOX_VROWS = 80
LOG2E = 1.4426950408889634


def _cparams(*sem):
    return pltpu.CompilerParams(dimension_semantics=sem, vmem_limit_bytes=VMEM_LIMIT_BYTES)


def _sigmoid(x):
    return 0.5 * jnp.tanh(0.5 * x) + 0.5


def _log_sigmoid(x):
    return jnp.minimum(x, 0.0) - jnp.log(1.0 + jnp.exp(-jnp.abs(x)))


def _split3(x):
    a = x.astype(BF16)
    r = x - a.astype(F32)
    b = r.astype(BF16)
    c = (r - b.astype(F32)).astype(BF16)
    return a, b, c


def _dot_nt(a, b):
    return lax.dot_general(a, b, (((1,), (1,)), ((), ())), preferred_element_type=F32)


def _tri_cumsum(tri_bf, x):
    a, b, c = _split3(x)
    dot = functools.partial(jnp.dot, preferred_element_type=F32)
    return dot(tri_bf, a) + dot(tri_bf, b) + dot(tri_bf, c)


def _inproj_kernel(x_ref, gain_ref, wm_ref, ws_ref, zm_ref, zs_ref, h_scr):
    @pl.when(pl.program_id(1) == 0)
    def _():
        x = x_ref[...]
        ms = jnp.mean(x * x, axis=-1, keepdims=True)
        h = (x * lax.rsqrt(ms + EPS) * gain_ref[...]).astype(BF16)
        h_scr[...] = h
        zs_ref[...] = _dot_nt(h, ws_ref[...])

    zm_ref[...] = _dot_nt(h_scr[...], wm_ref[...]).astype(BF16)


def _inproj(x2, gain, w_main_t, w_small_t, *, layer, tm, tn):
    T, D = x2.shape
    NM = w_main_t.shape[1]
    return pl.pallas_call(
        _inproj_kernel,
        grid=(T // tm, NM // tn),
        in_specs=[
            pl.BlockSpec((tm, D), lambda i, j: (i, 0)),
            pl.BlockSpec((1, D), lambda i, j: (0, 0)),
            pl.BlockSpec((None, tn, D), lambda i, j: (layer, j, 0)),
            pl.BlockSpec((None, LANES, D), lambda i, j: (layer, 0, 0)),
        ],
        out_specs=[
            pl.BlockSpec((tm, tn), lambda i, j: (i, j)),
            pl.BlockSpec((tm, LANES), lambda i, j: (i, 0)),
        ],
        out_shape=[
            jax.ShapeDtypeStruct((T, NM), BF16),
            jax.ShapeDtypeStruct((T, LANES), F32),
        ],
        scratch_shapes=[pltpu.VMEM((tm, D), BF16)],
        compiler_params=_cparams("parallel", "arbitrary"),
        name="inproj",
    )(x2, gain, w_main_t, w_small_t)


def _gla_kernel(q_ref, k_ref, v_ref, zs_ref, gate_ref, wup_ref, bgk_ref, gain_ref,
                o_ref, state_scr, b_scr, p_scr, *, nchunks, unroll):
    C, SUB = GLA_CHUNK, GLA_SUB
    NSUB = C // SUB
    NPAIR = SUB // 2
    DK, DV = GLA_DK, GLA_DV
    dot = functools.partial(jnp.dot, preferred_element_type=F32)

    @pl.when(pl.program_id(2) == 0)
    def _():
        state_scr[...] = jnp.zeros_like(state_scr)

    row = lax.broadcasted_iota(jnp.int32, (C, C), 0)
    col = lax.broadcasted_iota(jnp.int32, (C, C), 1)
    tri = (col <= row).astype(BF16)
    same_sub = (row // SUB) == (col // SUB)
    causal = col <= row

    z1, z2, z3 = _split3(zs_ref[...])
    zlane = lax.broadcasted_iota(jnp.int32, z1.shape, 1)
    lhs = jnp.where(zlane < 3 * GK_RANK, z1, jnp.where(zlane < 5 * GK_RANK, z2, z3))
    w1, w2, w3 = _split3(wup_ref[...])
    rhs = jnp.concatenate([w1, w2, w3, w1, w2, w1,
                           jnp.zeros((LANES - GD_COPIES * GK_RANK, DK), BF16)], axis=0)
    logits = dot(lhs, rhs) + bgk_ref[...]
    g = _log_sigmoid(logits) * (LOG2E / GK_NORMALIZER)
    for c in range(nchunks):
        b_scr[c * C:(c + 1) * C, :] = _tri_cumsum(tri, g[c * C:(c + 1) * C, :])

    cs_r = lax.broadcasted_iota(jnp.int32, (2 * DK, LANES), 0)
    cs_c = lax.broadcasted_iota(jnp.int32, (2 * DK, LANES), 1)
    chansum = ((cs_r // DK) == (cs_c % 2)).astype(BF16)
    pair_of_lane = (lax.broadcasted_iota(jnp.int32, (C, LANES), 1) % SUB) // 2
    scale = DK ** -0.5
    gain = gain_ref[...]

    def chunk(c, p_slot, state):
        r0 = pl.multiple_of(c * C, C)
        rows = pl.ds(r0, C)
        q = q_ref[rows, :].astype(F32) * scale
        k = k_ref[rows, :].astype(F32)
        v = v_ref[rows, :]
        b = b_scr[rows, :]
        b_last = b[C - 1:C, :]

        o = dot((q * jnp.exp2(b)).astype(BF16), state.astype(BF16))

        cross = [jnp.zeros((SUB, C), F32)]
        for i in range(1, NSUB):
            n = i * SUB
            bref = b[n - 1:n, :]
            q_t = (q[n:n + SUB, :] * jnp.exp2(b[n:n + SUB, :] - bref)).astype(BF16)
            k_t = (k[:n, :] * jnp.exp2(bref - b[:n, :])).astype(BF16)
            k_t = jnp.concatenate([k_t, jnp.zeros((C - n, DK), BF16)], axis=0)
            cross.append(lax.dot_general(q_t, k_t, (((1,), (1,)), ((), ())),
                                         preferred_element_type=F32))
        a_cross = jnp.concatenate(cross, axis=0)

        k3 = k.reshape(NSUB, SUB, DK)
        b3 = b.reshape(NSUB, SUB, DK)
        for j in range(SUB):
            kj = jnp.broadcast_to(k3[:, j:j + 1, :], (NSUB, SUB, DK)).reshape(C, DK)
            bj = jnp.broadcast_to(b3[:, j:j + 1, :], (NSUB, SUB, DK)).reshape(C, DK)
            p = q * kj * jnp.exp2(jnp.minimum(b - bj, 0.0))
            p_slot[(j // 2) * C:(j // 2 + 1) * C, (j % 2) * DK:(j % 2 + 1) * DK] = p.astype(BF16)
        d = dot(p_slot[...], chansum)
        a_diag = d[0:C, :]
        for t in range(1, NPAIR):
            a_diag = jnp.where(pair_of_lane == t, d[t * C:(t + 1) * C, :], a_diag)

        a = jnp.where(causal, jnp.where(same_sub, a_diag[:, :C], a_cross), 0.0)
        o = o + dot(a.astype(BF16), v)

        k_dec = (k * jnp.exp2(b_last - b)).astype(BF16)
        upd = lax.dot_general(k_dec, v, (((0,), (0,)), ((), ())),
                              preferred_element_type=F32)
        decay_col = jnp.transpose(jnp.broadcast_to(jnp.exp2(b_last), (DK, DK)))[:, 0:1]
        new_state = state * decay_col + upd

        ms = jnp.mean(o * o, axis=-1, keepdims=True)
        on = o * lax.rsqrt(ms + EPS) * gain
        gt = gate_ref[rows, :].astype(F32)
        o_ref[rows, :] = (on * (gt * _sigmoid(gt))).astype(o_ref.dtype)
        return new_state

    def group(gi, carry):
        state = state_scr[...]
        for u in range(unroll):
            state = chunk(gi * unroll + u, p_scr.at[u], state)
        state_scr[...] = state
        return carry

    lax.fori_loop(0, nchunks // unroll, group, 0)


def _gla(z_main, z_small, w_up, b_gk, gain, *, batch, seq, ts, unroll):
    T = z_main.shape[0]
    nsb = seq // ts
    nchunks = ts // GLA_CHUNK
    rowmap = lambda b, h, s: b * nsb + s
    kblk = ZM_GK // GLA_DK
    vblk = ZM_GV // GLA_DV
    gblk = ZM_GGATE // GLA_DV
    return pl.pallas_call(
        functools.partial(_gla_kernel, nchunks=nchunks, unroll=unroll),
        grid=(batch, GLA_HEADS, nsb),
        in_specs=[
            pl.BlockSpec((ts, GLA_DK), lambda b, h, s: (rowmap(b, h, s), h)),
            pl.BlockSpec((ts, GLA_DK), lambda b, h, s: (rowmap(b, h, s), kblk + h)),
            pl.BlockSpec((ts, GLA_DV), lambda b, h, s: (rowmap(b, h, s), vblk + h)),
            pl.BlockSpec((ts, LANES), lambda b, h, s: (rowmap(b, h, s), 0)),
            pl.BlockSpec((ts, GLA_DV), lambda b, h, s: (rowmap(b, h, s), gblk + h)),
            pl.BlockSpec((GK_RANK, GLA_DK), lambda b, h, s: (0, h)),
            pl.BlockSpec((1, GLA_DK), lambda b, h, s: (0, h)),
            pl.BlockSpec((1, GLA_DV), lambda b, h, s: (0, 0)),
        ],
        out_specs=pl.BlockSpec((ts, GLA_DV), lambda b, h, s: (rowmap(b, h, s), h)),
        out_shape=jax.ShapeDtypeStruct((T, GLA_WIDTH), BF16),
        scratch_shapes=[
            pltpu.VMEM((GLA_DK, GLA_DV), F32),
            pltpu.VMEM((ts, GLA_DK), F32),
            pltpu.VMEM((unroll, GLA_SUB // 2 * GLA_CHUNK, 2 * GLA_DK), BF16),
        ],
        compiler_params=_cparams("parallel", "parallel", "arbitrary"),
        name="gla",
    )(z_main, z_main, z_main, z_small, z_main, w_up, b_gk, gain)


def _fox_extra_tables():
    npair = FOX_HEADS // 2
    pq = np.zeros((3 * LANES, npair * LANES), np.float32)
    pk = np.zeros((3 * LANES, npair * LANES), np.float32)
    cq = np.zeros((1, npair * LANES), np.float32)
    ck = np.zeros((1, npair * LANES), np.float32)
    for h in range(FOX_HEADS):
        base = (h // 2) * LANES + (h % 2) * FOX_EXTRA
        for t in range(3):
            pq[t * LANES + ZS_FLOGIT + h, base + t] = 1.0
            cq[0, base + 3 + t] = 1.0
            ck[0, base + t] = 1.0
            pk[t * LANES + ZS_FLOGIT + h, base + 3 + t] = -1.0
    return jnp.asarray(pq, BF16), jnp.asarray(pk, BF16), jnp.asarray(cq), jnp.asarray(ck)


def _fox_prep_kernel(v_ref, zs_ref, bf_ref, pq_ref, pk_ref, cq_ref, ck_ref,
                     qx_ref, kx_ref, vt_ref, carry_scr, *, ts):
    @pl.when(pl.program_id(1) == 0)
    def _():
        carry_scr[...] = jnp.zeros_like(carry_scr)

    dot = functools.partial(jnp.dot, preferred_element_type=F32)
    row = lax.broadcasted_iota(jnp.int32, (ts, ts), 0)
    col = lax.broadcasted_iota(jnp.int32, (ts, ts), 1)
    tri = (col <= row).astype(BF16)
    log_f = _log_sigmoid(zs_ref[...] + bf_ref[...])
    c_all = _tri_cumsum(tri, log_f) + carry_scr[...]
    carry_scr[...] = c_all[ts - 1:ts, :]

    cs = jnp.concatenate(_split3(c_all * LOG2E), axis=1)
    qx_ref[...] = (dot(cs, pq_ref[...]) + cq_ref[...]).astype(BF16)
    kx_ref[...] = (dot(cs, pk_ref[...]) + ck_ref[...]).astype(BF16)

    v_t = jnp.transpose(v_ref[...].astype(F32))
    pad_rows = FOX_VROWS - FOX_DH
    ones_blk = jnp.where(lax.broadcasted_iota(jnp.int32, (pad_rows, ts), 0) == 0, 1.0, 0.0)
    pieces = []
    for h in range(FOX_HEADS):
        pieces += [v_t[h * FOX_DH:(h + 1) * FOX_DH, :], ones_blk]
    vt_ref[...] = jnp.concatenate(pieces, axis=0).astype(BF16)


def _fox_prep(z_main, z_small, bf_row, *, batch, seq, ts):
    T = z_main.shape[0]
    nsb = seq // ts
    vblk = ZM_FV // FOX_WIDTH
    npair = FOX_HEADS // 2
    extra = jax.ShapeDtypeStruct((T, npair * LANES), BF16)
    pq, pk, cq, ck = _fox_extra_tables()
    const = lambda b, s: (0, 0)
    return pl.pallas_call(
        functools.partial(_fox_prep_kernel, ts=ts),
        grid=(batch, nsb),
        in_specs=[
            pl.BlockSpec((ts, FOX_WIDTH), lambda b, s: (b * nsb + s, vblk)),
            pl.BlockSpec((ts, LANES), lambda b, s: (b * nsb + s, 0)),
            pl.BlockSpec((1, LANES), const),
            pl.BlockSpec(pq.shape, const),
            pl.BlockSpec(pk.shape, const),
            pl.BlockSpec(cq.shape, const),
            pl.BlockSpec(ck.shape, const),
        ],
        out_specs=[
            pl.BlockSpec((ts, npair * LANES), lambda b, s: (b * nsb + s, 0)),
            pl.BlockSpec((ts, npair * LANES), lambda b, s: (b * nsb + s, 0)),
            pl.BlockSpec((FOX_HEADS * FOX_VROWS, ts), lambda b, s: (b, s)),
        ],
        out_shape=[extra, extra,
                   jax.ShapeDtypeStruct((batch * FOX_HEADS * FOX_VROWS, seq), BF16)],
        scratch_shapes=[pltpu.VMEM((1, LANES), F32)],
        compiler_params=_cparams("parallel", "arbitrary"),
        name="fox_prep",
    )(z_main, z_small, bf_row, pq, pk, cq, ck)


def _fox_attn_kernel(q_ref, qx_ref, k_ref, kx_ref, vt_ref, gate_ref, o_ref, m_scr, acc_scr, *, tq, tk):
    qi = pl.program_id(2)
    m_scr[...] = jnp.full_like(m_scr, -jnp.inf)
    acc_scr[...] = jnp.zeros_like(acc_scr)

    q_scaled = (q_ref[...].astype(F32) * (FOX_DH ** -0.5 * LOG2E)).astype(BF16)
    q2 = jnp.concatenate([q_scaled, qx_ref[...]], axis=1)
    lane = lax.broadcasted_iota(jnp.int32, (1, 2 * LANES), 1)
    head_lanes = [((lane >= hh * FOX_DH) & (lane < (hh + 1) * FOX_DH))
                  | ((lane >= LANES + hh * FOX_EXTRA) & (lane < LANES + (hh + 1) * FOX_EXTRA))
                  for hh in range(2)]

    def scores(j, hh):
        k0 = pl.multiple_of(j * tk, tk)
        k2 = jnp.concatenate([k_ref[pl.ds(k0, tk), :], kx_ref[pl.ds(k0, tk), :]], axis=1)
        k_h = jnp.where(head_lanes[hh], k2, jnp.zeros_like(k2))
        return _dot_nt(k_h, q2)

    def absorb(j, hh, st, stats, masked):
        m_prev, acc = stats
        if masked:
            k_pos = lax.broadcasted_iota(jnp.int32, (tk, tq), 0)
            q_pos = lax.broadcasted_iota(jnp.int32, (tk, tq), 1)
            st = jnp.where(k_pos <= q_pos, st, -jnp.inf)
        m_new = jnp.maximum(m_prev, jnp.max(st, axis=0, keepdims=True))
        alpha = jnp.exp2(m_prev - m_new)
        p = jnp.exp2(st - m_new).astype(BF16)
        k0 = pl.multiple_of(j * tk, tk)
        vt = vt_ref[hh * FOX_VROWS:(hh + 1) * FOX_VROWS, pl.ds(k0, tk)]
        pv = jnp.dot(vt, p, preferred_element_type=F32)
        return m_new, acc * alpha + pv

    def run(chunks, diag_last):
        sts = [[scores(j, hh) for hh in range(2)] for j in chunks]
        stats = [(m_scr[hh], acc_scr[hh]) for hh in range(2)]
        for ci, j in enumerate(chunks):
            for hh in range(2):
                stats[hh] = absorb(j, hh, sts[ci][hh], stats[hh],
                                   masked=diag_last and ci == len(chunks) - 1)
        for hh in range(2):
            m_scr[hh], acc_scr[hh] = stats[hh]

    def pair(jp, carry):
        run([2 * jp, 2 * jp + 1], False)
        return carry

    lax.fori_loop(0, qi // 2, pair, 0)

    @pl.when(qi % 2 == 1)
    def _():
        run([qi - 1, qi], True)

    @pl.when(qi % 2 == 0)
    def _():
        run([qi], True)

    outs = []
    for hh in range(2):
        acc = acc_scr[hh]
        outs.append(acc[:FOX_DH, :] / acc[FOX_DH:FOX_DH + 1, :])
    ot = jnp.concatenate(outs, axis=0)
    gt = gate_ref[...].astype(F32)
    o_ref[...] = (jnp.transpose(ot) * (gt * _sigmoid(gt))).astype(o_ref.dtype)


def _fox_attn(z_main, q_extra, k_extra, v_t, *, batch, seq, tq):
    T = z_main.shape[0]
    tk = tq
    nq = seq // tq
    npair = FOX_HEADS // 2
    qblk = ZM_FQ // LANES
    kblk = ZM_FK // LANES
    gblk = ZM_FGATE // LANES
    return pl.pallas_call(
        functools.partial(_fox_attn_kernel, tq=tq, tk=tk),
        grid=(batch, npair, nq),
        in_specs=[
            pl.BlockSpec((tq, LANES), lambda b, p, i: (b * nq + i, qblk + p)),
            pl.BlockSpec((tq, LANES), lambda b, p, i: (b * nq + i, p)),
            pl.BlockSpec((seq, LANES), lambda b, p, i: (b, kblk + p)),
            pl.BlockSpec((seq, LANES), lambda b, p, i: (b, p)),
            pl.BlockSpec((2 * FOX_VROWS, seq), lambda b, p, i: (b * npair + p, 0)),
            pl.BlockSpec((tq, LANES), lambda b, p, i: (b * nq + i, gblk + p)),
        ],
        out_specs=pl.BlockSpec((tq, LANES), lambda b, p, i: (b * nq + i, p)),
        out_shape=jax.ShapeDtypeStruct((T, FOX_WIDTH), BF16),
        scratch_shapes=[
            pltpu.VMEM((2, 1, tq), F32),
            pltpu.VMEM((2, FOX_VROWS, tq), F32),
        ],
        compiler_params=_cparams("parallel", "parallel", "arbitrary"),
        name="fox_attn",
    )(z_main, q_extra, z_main, k_extra, v_t, z_main)


def _memkv_kernel(mem_ref, gain_ref, w_ref, o_ref):
    x = mem_ref[...]
    ms = jnp.mean(x * x, axis=-1, keepdims=True)
    h = (x * lax.rsqrt(ms + EPS) * gain_ref[...]).astype(BF16)
    o_ref[...] = jnp.dot(h, w_ref[...], preferred_element_type=F32).astype(o_ref.dtype)


def _memkv(mem2, gain, w_kv, *, tm):
    R, D = mem2.shape
    N = w_kv.shape[1]
    return pl.pallas_call(
        _memkv_kernel,
        grid=(R // tm,),
        in_specs=[
            pl.BlockSpec((tm, D), lambda i: (i, 0)),
            pl.BlockSpec((1, D), lambda i: (0, 0)),
            pl.BlockSpec((D, N), lambda i: (0, 0)),
        ],
        out_specs=pl.BlockSpec((tm, N), lambda i: (i, 0)),
        out_shape=jax.ShapeDtypeStruct((R, N), BF16),
        compiler_params=_cparams("parallel"),
        name="memkv",
    )(mem2, gain, w_kv)


def _mem_attn_kernel(q_ref, gate_ref, mk_ref, mv_ref, o_ref):
    scale = MEM_DH ** -0.5
    outs = []
    for h in range(MEM_HEADS):
        sl = slice(h * MEM_DH, (h + 1) * MEM_DH)
        s = lax.dot_general(q_ref[:, sl], mk_ref[:, sl], (((1,), (1,)), ((), ())),
                            preferred_element_type=F32) * scale
        m = jnp.max(s, axis=-1, keepdims=True)
        p = jnp.exp(s - m)
        l = jnp.sum(p, axis=-1, keepdims=True)
        p = (p / l).astype(BF16)
        outs.append(jnp.dot(p, mv_ref[:, sl], preferred_element_type=F32))
    o = jnp.concatenate(outs, axis=1)
    gt = gate_ref[...].astype(F32)
    o_ref[...] = (o * (gt * _sigmoid(gt))).astype(o_ref.dtype)


def _mem_attn(z_main, mkv, *, batch, seq, mem_len, tm):
    T = z_main.shape[0]
    nsb = seq // tm
    qblk = ZM_MQ // MEM_WIDTH
    gblk = ZM_MGATE // MEM_WIDTH
    return pl.pallas_call(
        _mem_attn_kernel,
        grid=(batch, nsb),
        in_specs=[
            pl.BlockSpec((tm, MEM_WIDTH), lambda b, s: (b * nsb + s, qblk)),
            pl.BlockSpec((tm, MEM_WIDTH), lambda b, s: (b * nsb + s, gblk)),
            pl.BlockSpec((mem_len, MEM_WIDTH), lambda b, s: (b, 0)),
            pl.BlockSpec((mem_len, MEM_WIDTH), lambda b, s: (b, 1)),
        ],
        out_specs=pl.BlockSpec((tm, MEM_WIDTH), lambda b, s: (b * nsb + s, 0)),
        out_shape=jax.ShapeDtypeStruct((T, MEM_WIDTH), BF16),
        compiler_params=_cparams("parallel", "parallel"),
        name="mem_attn",
    )(z_main, z_main, mkv, mkv)


def _merge_out_kernel(oa_ref, ob_ref, oc_ref, g0_ref, g1_ref, g2_ref, x_ref, gain_ref,
                      wbr_hbm, wout_hbm, o_ref, wbr_vmem, wout_vmem, sem, *, layer, final_norm):
    @pl.when(pl.program_id(0) == 0)
    def _():
        copies = [pltpu.make_async_copy(wbr_hbm.at[layer], wbr_vmem, sem.at[0]),
                  pltpu.make_async_copy(wout_hbm.at[layer], wout_vmem, sem.at[1])]
        for cp in copies:
            cp.start()
        for cp in copies:
            cp.wait()

    dot = functools.partial(jnp.dot, preferred_element_type=F32)
    r_b, r_c = GLA_WIDTH, GLA_WIDTH + FOX_WIDTH
    y = _sigmoid(g0_ref[...].astype(F32)) * dot(oa_ref[...], wbr_vmem[0:r_b, :])
    y = y + _sigmoid(g1_ref[...].astype(F32)) * dot(ob_ref[...], wbr_vmem[r_b:r_c, :])
    y = y + _sigmoid(g2_ref[...].astype(F32)) * dot(oc_ref[...], wbr_vmem[r_c:, :])
    x = x_ref[...] + dot(y.astype(BF16), wout_vmem[...])
    if final_norm:
        ms = jnp.mean(x * x, axis=-1, keepdims=True)
        x = x * lax.rsqrt(ms + EPS) * gain_ref[...]
    o_ref[...] = x


def _merge_out(o_a, o_b, o_c, z_main, x2, w_branch, w_out, final_gain, *, layer, tm, final_norm):
    T, D = x2.shape
    gblk = ZM_MERGE // D
    return pl.pallas_call(
        functools.partial(_merge_out_kernel, layer=layer, final_norm=final_norm),
        grid=(T // tm,),
        in_specs=[
            pl.BlockSpec((tm, GLA_WIDTH), lambda i: (i, 0)),
            pl.BlockSpec((tm, FOX_WIDTH), lambda i: (i, 0)),
            pl.BlockSpec((tm, MEM_WIDTH), lambda i: (i, 0)),
            pl.BlockSpec((tm, D), lambda i: (i, gblk)),
            pl.BlockSpec((tm, D), lambda i: (i, gblk + 1)),
            pl.BlockSpec((tm, D), lambda i: (i, gblk + 2)),
            pl.BlockSpec((tm, D), lambda i: (i, 0)),
            pl.BlockSpec((1, D), lambda i: (0, 0)),
            pl.BlockSpec(memory_space=pl.ANY),
            pl.BlockSpec(memory_space=pl.ANY),
        ],
        out_specs=pl.BlockSpec((tm, D), lambda i: (i, 0)),
        out_shape=jax.ShapeDtypeStruct((T, D), F32),
        scratch_shapes=[
            pltpu.VMEM(w_branch.shape[1:], BF16),
            pltpu.VMEM(w_out.shape[1:], BF16),
            pltpu.SemaphoreType.DMA((2,)),
        ],
        compiler_params=_cparams("arbitrary"),
        name="merge_out",
    )(o_a, o_b, o_c, z_main, z_main, z_main, x2, final_gain, w_branch, w_out)


W_GDOWN = 2 * GLA_KWIDTH + 2 * GLA_WIDTH
W_FQ = W_GDOWN + GK_RANK
W_FLOGIT = W_FQ + 3 * FOX_WIDTH
W_FGATE = W_FLOGIT + FOX_HEADS


def _pack_src_row(r):
    return (r + jnp.where(r >= W_GDOWN, GK_RANK, 0)
            + jnp.where(r + GK_RANK >= W_FLOGIT, FOX_HEADS, 0))


def _pack_copy(wt_hbm, buf, sem, step, *, tr, nblk):
    layer = step // nblk
    row = pl.multiple_of(_pack_src_row((step % nblk) * tr), 8)
    slot = step % 2
    return pltpu.make_async_copy(wt_hbm.at[layer, pl.ds(row, tr), :], buf.at[slot], sem.at[slot])


def _pack_kernel(wt_hbm, wm_ref, buf, sem, *, tr, nblk):
    s = pl.program_id(0)
    copy = functools.partial(_pack_copy, wt_hbm, buf, sem, tr=tr, nblk=nblk)

    @pl.when(s == 0)
    def _():
        copy(s).start()

    @pl.when(s + 1 < pl.num_programs(0))
    def _():
        copy(s + 1).start()

    copy(s).wait()
    wm_ref[...] = buf[s % 2].astype(BF16)


def _pack_w_in(w_in, *, tr):
    depth, d, in_cols = w_in.shape
    n_main = in_cols - GK_RANK - FOX_HEADS
    nblk = n_main // tr
    wt = jnp.swapaxes(w_in, 1, 2)
    w_main_t = pl.pallas_call(
        functools.partial(_pack_kernel, tr=tr, nblk=nblk),
        grid=(depth * nblk,),
        in_specs=[pl.BlockSpec(memory_space=pl.ANY)],
        out_specs=pl.BlockSpec((None, tr, d), lambda s: (s // nblk, s % nblk, 0)),
        out_shape=jax.ShapeDtypeStruct((depth, n_main, d), BF16),
        scratch_shapes=[pltpu.VMEM((2, tr, d), F32), pltpu.SemaphoreType.DMA((2,))],
        compiler_params=_cparams("arbitrary"),
        name="pack_w_in",
    )(wt)
    pad = jnp.zeros((depth, LANES - GD_COPIES * GK_RANK - FOX_HEADS, d), w_in.dtype)
    w_small_t = jnp.concatenate([wt[:, W_GDOWN:W_FQ, :]] * GD_COPIES
                                + [wt[:, W_FLOGIT:W_FGATE, :], pad], axis=1).astype(BF16)
    return w_main_t, w_small_t


def _trunk(x, mem, norm_gain, w_in, w_gk_up, b_gk, gla_norm_gain, b_f, mem_norm_gain,
           w_mem_kv, w_branch, w_out, final_gain, *, tiles):
    B, S, D = x.shape
    M = mem.shape[1]
    depth = w_in.shape[0]
    x2 = x.reshape(B * S, D)
    mem2 = mem.reshape(B * M, D)
    w_main, w_small = _pack_w_in(w_in, tr=tiles["pack_tr"])
    w_branch_bf = w_branch.astype(BF16)
    w_out_bf = w_out.astype(BF16)
    for l in range(depth):
        z_main, z_small = _inproj(x2, norm_gain[l][None, :], w_main, w_small,
                                  layer=l, tm=tiles["in_tm"], tn=tiles["in_tn"])
        o_a = _gla(z_main, z_small, w_gk_up[l], b_gk[l][None, :], gla_norm_gain[l][None, :],
                   batch=B, seq=S, ts=tiles["gla_ts"], unroll=tiles["gla_unroll"])
        bf_row = jnp.zeros((1, LANES), F32).at[0, ZS_FLOGIT:ZS_FLOGIT + FOX_HEADS].set(b_f[l])
        q_extra, k_extra, v_t = _fox_prep(z_main, z_small, bf_row, batch=B, seq=S, ts=tiles["prep_ts"])
        o_b = _fox_attn(z_main, q_extra, k_extra, v_t, batch=B, seq=S, tq=tiles["fox_tq"])
        mkv = _memkv(mem2, mem_norm_gain[l][None, :], w_mem_kv[l].astype(BF16), tm=tiles["memkv_tm"])
        o_c = _mem_attn(z_main, mkv, batch=B, seq=S, mem_len=M, tm=tiles["mem_tm"])
        x2 = _merge_out(o_a, o_b, o_c, z_main, x2, w_branch_bf, w_out_bf, final_gain[None, :],
                        layer=l, tm=tiles["merge_tm"], final_norm=(l == depth - 1))
    return x2.reshape(B, S, D)


_TILES = dict(pack_tr=256, in_tm=1024, in_tn=1536, gla_ts=1024, gla_unroll=8, prep_ts=512, fox_tq=512,
              memkv_tm=256, mem_tm=512, merge_tm=256)


def kernel(x, mem, norm_gain, w_in, w_gk_up, b_gk, gla_norm_gain, b_f, mem_norm_gain,
           w_mem_kv, w_branch, w_out, final_gain):
    return _trunk(x, mem, norm_gain, w_in, w_gk_up, b_gk, gla_norm_gain, b_f, mem_norm_gain,
                  w_mem_kv, w_branch, w_out, final_gain, tiles=_TILES)
```

```python
import functools

import jax
import jax.numpy as jnp
import numpy as np
from jax import lax
from jax.experimental import pallas as pl
from jax.experimental.pallas import tpu as pltpu

F32 = jnp.float32
BF16 = jnp.bfloat16

EPS = 1e-6

GLA_HEADS, GLA_DK, GLA_DV = 4, 128, 256
GLA_KWIDTH = GLA_HEADS * GLA_DK
GLA_WIDTH = GLA_HEADS * GLA_DV
GK_RANK = 16
GK_NORMALIZER = 16.0
FOX_HEADS, FOX_DH = 8, 64
FOX_WIDTH = FOX_HEADS * FOX_DH
MEM_HEADS, MEM_DH = 4, 128
MEM_WIDTH = MEM_HEADS * MEM_DH
N_BRANCH = 3

LANES = 128
VMEM_LIMIT_BYTES = 56 * 1024 * 1024

ZM_GQ = 0
ZM_GK = ZM_GQ + GLA_KWIDTH
ZM_GV = ZM_GK + GLA_KWIDTH
ZM_GGATE = ZM_GV + GLA_WIDTH
ZM_FQ = ZM_GGATE + GLA_WIDTH
ZM_FK = ZM_FQ + FOX_WIDTH
ZM_FV = ZM_FK + FOX_WIDTH
ZM_FGATE = ZM_FV + FOX_WIDTH
ZM_MQ = ZM_FGATE + FOX_WIDTH
ZM_MGATE = ZM_MQ + MEM_WIDTH
ZM_MERGE = ZM_MGATE + MEM_WIDTH
GD_COPIES = 6
ZS_FLOGIT = GD_COPIES * GK_RANK

GLA_CHUNK = 64
GLA_SUB = 8
FOX_EXTRA = 6
FOX_VROWS = 80
LOG2E = 1.4426950408889634


def _cparams(*sem):
    return pltpu.CompilerParams(dimension_semantics=sem, vmem_limit_bytes=VMEM_LIMIT_BYTES)


def _sigmoid(x):
    return 0.5 * jnp.tanh(0.5 * x) + 0.5


def _log_sigmoid(x):
    return jnp.minimum(x, 0.0) - jnp.log(1.0 + jnp.exp(-jnp.abs(x)))


def _split3(x):
    a = x.astype(BF16)
    r = x - a.astype(F32)
    b = r.astype(BF16)
    c = (r - b.astype(F32)).astype(BF16)
    return a, b, c


def _dot_nt(a, b):
    return lax.dot_general(a, b, (((1,), (1,)), ((), ())), preferred_element_type=F32)


def _tri_cumsum(tri_bf, x):
    a, b, c = _split3(x)
    dot = functools.partial(jnp.dot, preferred_element_type=F32)
    return dot(tri_bf, a) + dot(tri_bf, b) + dot(tri_bf, c)


def _inproj_kernel(x_ref, gain_ref, wm_ref, ws_ref, zm_ref, zs_ref, h_scr):
    @pl.when(pl.program_id(1) == 0)
    def _():
        x = x_ref[...]
        ms = jnp.mean(x * x, axis=-1, keepdims=True)
        h = (x * lax.rsqrt(ms + EPS) * gain_ref[...]).astype(BF16)
        h_scr[...] = h
        zs_ref[...] = _dot_nt(h, ws_ref[...])

    zm_ref[...] = _dot_nt(h_scr[...], wm_ref[...]).astype(BF16)


def _inproj(x2, gain, w_main_t, w_small_t, *, layer, tm, tn):
    T, D = x2.shape
    NM = w_main_t.shape[1]
    return pl.pallas_call(
        _inproj_kernel,
        grid=(T // tm, NM // tn),
        in_specs=[
            pl.BlockSpec((tm, D), lambda i, j: (i, 0)),
            pl.BlockSpec((1, D), lambda i, j: (0, 0)),
            pl.BlockSpec((None, tn, D), lambda i, j: (layer, j, 0)),
            pl.BlockSpec((None, LANES, D), lambda i, j: (layer, 0, 0)),
        ],
        out_specs=[
            pl.BlockSpec((tm, tn), lambda i, j: (i, j)),
            pl.BlockSpec((tm, LANES), lambda i, j: (i, 0)),
        ],
        out_shape=[
            jax.ShapeDtypeStruct((T, NM), BF16),
            jax.ShapeDtypeStruct((T, LANES), F32),
        ],
        scratch_shapes=[pltpu.VMEM((tm, D), BF16)],
        compiler_params=_cparams("parallel", "arbitrary"),
        name="inproj",
    )(x2, gain, w_main_t, w_small_t)


def _gla_kernel(q_ref, k_ref, v_ref, zs_ref, gate_ref, wup_ref, bgk_ref, gain_ref,
                o_ref, state_scr, b_scr, p_scr, *, nchunks, unroll):
    C, SUB = GLA_CHUNK, GLA_SUB
    NSUB = C // SUB
    NPAIR = SUB // 2
    DK, DV = GLA_DK, GLA_DV
    dot = functools.partial(jnp.dot, preferred_element_type=F32)

    @pl.when(pl.program_id(2) == 0)
    def _():
        state_scr[...] = jnp.zeros_like(state_scr)

    row = lax.broadcasted_iota(jnp.int32, (C, C), 0)
    col = lax.broadcasted_iota(jnp.int32, (C, C), 1)
    tri = (col <= row).astype(BF16)
    same_sub = (row // SUB) == (col // SUB)
    causal = col <= row

    z1, z2, z3 = _split3(zs_ref[...])
    zlane = lax.broadcasted_iota(jnp.int32, z1.shape, 1)
    lhs = jnp.where(zlane < 3 * GK_RANK, z1, jnp.where(zlane < 5 * GK_RANK, z2, z3))
    w1, w2, w3 = _split3(wup_ref[...])
    rhs = jnp.concatenate([w1, w2, w3, w1, w2, w1,
                           jnp.zeros((LANES - GD_COPIES * GK_RANK, DK), BF16)], axis=0)
    logits = dot(lhs, rhs) + bgk_ref[...]
    g = _log_sigmoid(logits) * (LOG2E / GK_NORMALIZER)
    for c in range(nchunks):
        b_scr[c * C:(c + 1) * C, :] = _tri_cumsum(tri, g[c * C:(c + 1) * C, :])

    cs_r = lax.broadcasted_iota(jnp.int32, (2 * DK, LANES), 0)
    cs_c = lax.broadcasted_iota(jnp.int32, (2 * DK, LANES), 1)
    chansum = ((cs_r // DK) == (cs_c % 2)).astype(BF16)
    pair_of_lane = (lax.broadcasted_iota(jnp.int32, (C, LANES), 1) % SUB) // 2
    scale = DK ** -0.5
    gain = gain_ref[...]

    def chunk(c, p_slot, state):
        r0 = pl.multiple_of(c * C, C)
        rows = pl.ds(r0, C)
        q = q_ref[rows, :].astype(F32) * scale
        k = k_ref[rows, :].astype(F32)
        v = v_ref[rows, :]
        b = b_scr[rows, :]
        b_last = b[C - 1:C, :]

        o = dot((q * jnp.exp2(b)).astype(BF16), state.astype(BF16))

        cross = [jnp.zeros((SUB, C), F32)]
        for i in range(1, NSUB):
            n = i * SUB
            bref = b[n - 1:n, :]
            q_t = (q[n:n + SUB, :] * jnp.exp2(b[n:n + SUB, :] - bref)).astype(BF16)
            k_t = (k[:n, :] * jnp.exp2(bref - b[:n, :])).astype(BF16)
            k_t = jnp.concatenate([k_t, jnp.zeros((C - n, DK), BF16)], axis=0)
            cross.append(lax.dot_general(q_t, k_t, (((1,), (1,)), ((), ())),
                                         preferred_element_type=F32))
        a_cross = jnp.concatenate(cross, axis=0)

        k3 = k.reshape(NSUB, SUB, DK)
        b3 = b.reshape(NSUB, SUB, DK)
        for j in range(SUB):
            kj = jnp.broadcast_to(k3[:, j:j + 1, :], (NSUB, SUB, DK)).reshape(C, DK)
            bj = jnp.broadcast_to(b3[:, j:j + 1, :], (NSUB, SUB, DK)).reshape(C, DK)
            p = q * kj * jnp.exp2(jnp.minimum(b - bj, 0.0))
            p_slot[(j // 2) * C:(j // 2 + 1) * C, (j % 2) * DK:(j % 2 + 1) * DK] = p.astype(BF16)
        d = dot(p_slot[...], chansum)
        a_diag = d[0:C, :]
        for t in range(1, NPAIR):
            a_diag = jnp.where(pair_of_lane == t, d[t * C:(t + 1) * C, :], a_diag)

        a = jnp.where(causal, jnp.where(same_sub, a_diag[:, :C], a_cross), 0.0)
        o = o + dot(a.astype(BF16), v)

        k_dec = (k * jnp.exp2(b_last - b)).astype(BF16)
        upd = lax.dot_general(k_dec, v, (((0,), (0,)), ((), ())),
                              preferred_element_type=F32)
        decay_col = jnp.transpose(jnp.broadcast_to(jnp.exp2(b_last), (DK, DK)))[:, 0:1]
        new_state = state * decay_col + upd

        ms = jnp.mean(o * o, axis=-1, keepdims=True)
        on = o * lax.rsqrt(ms + EPS) * gain
        gt = gate_ref[rows, :].astype(F32)
        o_ref[rows, :] = (on * (gt * _sigmoid(gt))).astype(o_ref.dtype)
        return new_state

    def group(gi, carry):
        state = state_scr[...]
        for u in range(unroll):
            state = chunk(gi * unroll + u, p_scr.at[u], state)
        state_scr[...] = state
        return carry

    lax.fori_loop(0, nchunks // unroll, group, 0)


def _gla(z_main, z_small, w_up, b_gk, gain, *, batch, seq, ts, unroll):
    T = z_main.shape[0]
    nsb = seq // ts
    nchunks = ts // GLA_CHUNK
    rowmap = lambda b, h, s: b * nsb + s
    kblk = ZM_GK // GLA_DK
    vblk = ZM_GV // GLA_DV
    gblk = ZM_GGATE // GLA_DV
    return pl.pallas_call(
        functools.partial(_gla_kernel, nchunks=nchunks, unroll=unroll),
        grid=(batch, GLA_HEADS, nsb),
        in_specs=[
            pl.BlockSpec((ts, GLA_DK), lambda b, h, s: (rowmap(b, h, s), h)),
            pl.BlockSpec((ts, GLA_DK), lambda b, h, s: (rowmap(b, h, s), kblk + h)),
            pl.BlockSpec((ts, GLA_DV), lambda b, h, s: (rowmap(b, h, s), vblk + h)),
            pl.BlockSpec((ts, LANES), lambda b, h, s: (rowmap(b, h, s), 0)),
            pl.BlockSpec((ts, GLA_DV), lambda b, h, s: (rowmap(b, h, s), gblk + h)),
            pl.BlockSpec((GK_RANK, GLA_DK), lambda b, h, s: (0, h)),
            pl.BlockSpec((1, GLA_DK), lambda b, h, s: (0, h)),
            pl.BlockSpec((1, GLA_DV), lambda b, h, s: (0, 0)),
        ],
        out_specs=pl.BlockSpec((ts, GLA_DV), lambda b, h, s: (rowmap(b, h, s), h)),
        out_shape=jax.ShapeDtypeStruct((T, GLA_WIDTH), BF16),
        scratch_shapes=[
            pltpu.VMEM((GLA_DK, GLA_DV), F32),
            pltpu.VMEM((ts, GLA_DK), F32),
            pltpu.VMEM((unroll, GLA_SUB // 2 * GLA_CHUNK, 2 * GLA_DK), BF16),
        ],
        compiler_params=_cparams("parallel", "parallel", "arbitrary"),
        name="gla",
    )(z_main, z_main, z_main, z_small, z_main, w_up, b_gk, gain)


def _fox_extra_tables():
    npair = FOX_HEADS // 2
    pq = np.zeros((3 * LANES, npair * LANES), np.float32)
    pk = np.zeros((3 * LANES, npair * LANES), np.float32)
    cq = np.zeros((1, npair * LANES), np.float32)
    ck = np.zeros((1, npair * LANES), np.float32)
    for h in range(FOX_HEADS):
        base = (h // 2) * LANES + (h % 2) * FOX_EXTRA
        for t in range(3):
            pq[t * LANES + ZS_FLOGIT + h, base + t] = 1.0
            cq[0, base + 3 + t] = 1.0
            ck[0, base + t] = 1.0
            pk[t * LANES + ZS_FLOGIT + h, base + 3 + t] = -1.0
    return jnp.asarray(pq, BF16), jnp.asarray(pk, BF16), jnp.asarray(cq), jnp.asarray(ck)


def _fox_prep_kernel(v_ref, zs_ref, bf_ref, pq_ref, pk_ref, cq_ref, ck_ref,
                     qx_ref, kx_ref, vt_ref, carry_scr, *, ts):
    @pl.when(pl.program_id(1) == 0)
    def _():
        carry_scr[...] = jnp.zeros_like(carry_scr)

    dot = functools.partial(jnp.dot, preferred_element_type=F32)
    row = lax.broadcasted_iota(jnp.int32, (ts, ts), 0)
    col = lax.broadcasted_iota(jnp.int32, (ts, ts), 1)
    tri = (col <= row).astype(BF16)
    log_f = _log_sigmoid(zs_ref[...] + bf_ref[...])
    c_all = _tri_cumsum(tri, log_f) + carry_scr[...]
    carry_scr[...] = c_all[ts - 1:ts, :]

    cs = jnp.concatenate(_split3(c_all * LOG2E), axis=1)
    qx_ref[...] = (dot(cs, pq_ref[...]) + cq_ref[...]).astype(BF16)
    kx_ref[...] = (dot(cs, pk_ref[...]) + ck_ref[...]).astype(BF16)

    v_t = jnp.transpose(v_ref[...].astype(F32))
    pad_rows = FOX_VROWS - FOX_DH
    ones_blk = jnp.where(lax.broadcasted_iota(jnp.int32, (pad_rows, ts), 0) == 0, 1.0, 0.0)
    pieces = []
    for h in range(FOX_HEADS):
        pieces += [v_t[h * FOX_DH:(h + 1) * FOX_DH, :], ones_blk]
    vt_ref[...] = jnp.concatenate(pieces, axis=0).astype(BF16)


def _fox_prep(z_main, z_small, bf_row, *, batch, seq, ts):
    T = z_main.shape[0]
    nsb = seq // ts
    vblk = ZM_FV // FOX_WIDTH
    npair = FOX_HEADS // 2
    extra = jax.ShapeDtypeStruct((T, npair * LANES), BF16)
    pq, pk, cq, ck = _fox_extra_tables()
    const = lambda b, s: (0, 0)
    return pl.pallas_call(
        functools.partial(_fox_prep_kernel, ts=ts),
        grid=(batch, nsb),
        in_specs=[
            pl.BlockSpec((ts, FOX_WIDTH), lambda b, s: (b * nsb + s, vblk)),
            pl.BlockSpec((ts, LANES), lambda b, s: (b * nsb + s, 0)),
            pl.BlockSpec((1, LANES), const),
            pl.BlockSpec(pq.shape, const),
            pl.BlockSpec(pk.shape, const),
            pl.BlockSpec(cq.shape, const),
            pl.BlockSpec(ck.shape, const),
        ],
        out_specs=[
            pl.BlockSpec((ts, npair * LANES), lambda b, s: (b * nsb + s, 0)),
            pl.BlockSpec((ts, npair * LANES), lambda b, s: (b * nsb + s, 0)),
            pl.BlockSpec((FOX_HEADS * FOX_VROWS, ts), lambda b, s: (b, s)),
        ],
        out_shape=[extra, extra,
                   jax.ShapeDtypeStruct((batch * FOX_HEADS * FOX_VROWS, seq), BF16)],
        scratch_shapes=[pltpu.VMEM((1, LANES), F32)],
        compiler_params=_cparams("parallel", "arbitrary"),
        name="fox_prep",
    )(z_main, z_small, bf_row, pq, pk, cq, ck)


def _fox_attn_kernel(q_ref, qx_ref, k_ref, kx_ref, vt_ref, gate_ref, o_ref, m_scr, acc_scr, *, tq, tk):
    qi = pl.program_id(2)
    m_scr[...] = jnp.full_like(m_scr, -jnp.inf)
    acc_scr[...] = jnp.zeros_like(acc_scr)

    q_scaled = (q_ref[...].astype(F32) * (FOX_DH ** -0.5 * LOG2E)).astype(BF16)
    q2 = jnp.concatenate([q_scaled, qx_ref[...]], axis=1)
    lane = lax.broadcasted_iota(jnp.int32, (1, 2 * LANES), 1)
    head_lanes = [((lane >= hh * FOX_DH) & (lane < (hh + 1) * FOX_DH))
                  | ((lane >= LANES + hh * FOX_EXTRA) & (lane < LANES + (hh + 1) * FOX_EXTRA))
                  for hh in range(2)]

    def scores(j, hh):
        k0 = pl.multiple_of(j * tk, tk)
        k2 = jnp.concatenate([k_ref[pl.ds(k0, tk), :], kx_ref[pl.ds(k0, tk), :]], axis=1)
        k_h = jnp.where(head_lanes[hh], k2, jnp.zeros_like(k2))
        return _dot_nt(k_h, q2)

    def absorb(j, hh, st, stats, key_offset):
        m_prev, acc = stats
        if key_offset is not None:
            k_pos = key_offset + lax.broadcasted_iota(jnp.int32, (tk, tq), 0)
            q_pos = lax.broadcasted_iota(jnp.int32, (tk, tq), 1)
            st = jnp.where(k_pos <= q_pos, st, -jnp.inf)
        m_new = jnp.maximum(m_prev, jnp.max(st, axis=0, keepdims=True))
        alpha = jnp.exp2(m_prev - m_new)
        p = jnp.exp2(st - m_new).astype(BF16)
        k0 = pl.multiple_of(j * tk, tk)
        vt = vt_ref[hh * FOX_VROWS:(hh + 1) * FOX_VROWS, pl.ds(k0, tk)]
        pv = jnp.dot(vt, p, preferred_element_type=F32)
        return m_new, acc * alpha + pv

    def run(chunks, key_offsets):
        sts = [[scores(j, hh) for hh in range(2)] for j in chunks]
        stats = [(m_scr[hh], acc_scr[hh]) for hh in range(2)]
        for ci, j in enumerate(chunks):
            for hh in range(2):
                stats[hh] = absorb(j, hh, sts[ci][hh], stats[hh], key_offsets[ci])
        for hh in range(2):
            m_scr[hh], acc_scr[hh] = stats[hh]

    sub = tq // tk
    span = lambda s: [sub * s + c for c in range(sub)]
    own_offsets = [c * tk for c in range(sub)]

    def pair(sp, carry):
        run(span(2 * sp) + span(2 * sp + 1), [None] * (2 * sub))
        return carry

    lax.fori_loop(0, qi // 2, pair, 0)

    @pl.when(qi % 2 == 1)
    def _():
        run(span(qi - 1) + span(qi), [None] * sub + own_offsets)

    @pl.when(qi % 2 == 0)
    def _():
        run(span(qi), own_offsets)

    outs = []
    for hh in range(2):
        acc = acc_scr[hh]
        outs.append(acc[:FOX_DH, :] / acc[FOX_DH:FOX_DH + 1, :])
    ot = jnp.concatenate(outs, axis=0)
    gt = gate_ref[...].astype(F32)
    o_ref[...] = (jnp.transpose(ot) * (gt * _sigmoid(gt))).astype(o_ref.dtype)


def _fox_attn(z_main, q_extra, k_extra, v_t, *, batch, seq, tq, tk):
    T = z_main.shape[0]
    nq = seq // tq
    npair = FOX_HEADS // 2
    qblk = ZM_FQ // LANES
    kblk = ZM_FK // LANES
    gblk = ZM_FGATE // LANES
    return pl.pallas_call(
        functools.partial(_fox_attn_kernel, tq=tq, tk=tk),
        grid=(batch, npair, nq),
        in_specs=[
            pl.BlockSpec((tq, LANES), lambda b, p, i: (b * nq + i, qblk + p)),
            pl.BlockSpec((tq, LANES), lambda b, p, i: (b * nq + i, p)),
            pl.BlockSpec((seq, LANES), lambda b, p, i: (b, kblk + p)),
            pl.BlockSpec((seq, LANES), lambda b, p, i: (b, p)),
            pl.BlockSpec((2 * FOX_VROWS, seq), lambda b, p, i: (b * npair + p, 0)),
            pl.BlockSpec((tq, LANES), lambda b, p, i: (b * nq + i, gblk + p)),
        ],
        out_specs=pl.BlockSpec((tq, LANES), lambda b, p, i: (b * nq + i, p)),
        out_shape=jax.ShapeDtypeStruct((T, FOX_WIDTH), BF16),
        scratch_shapes=[
            pltpu.VMEM((2, 1, tq), F32),
            pltpu.VMEM((2, FOX_VROWS, tq), F32),
        ],
        compiler_params=_cparams("parallel", "parallel", "arbitrary"),
        name="fox_attn",
    )(z_main, q_extra, z_main, k_extra, v_t, z_main)


def _memkv_kernel(mem_ref, gain_ref, w_ref, o_ref):
    x = mem_ref[...]
    ms = jnp.mean(x * x, axis=-1, keepdims=True)
    h = (x * lax.rsqrt(ms + EPS) * gain_ref[...]).astype(BF16)
    o_ref[...] = jnp.dot(h, w_ref[...], preferred_element_type=F32).astype(o_ref.dtype)


def _memkv(mem2, gain, w_kv, *, tm):
    R, D = mem2.shape
    N = w_kv.shape[1]
    return pl.pallas_call(
        _memkv_kernel,
        grid=(R // tm,),
        in_specs=[
            pl.BlockSpec((tm, D), lambda i: (i, 0)),
            pl.BlockSpec((1, D), lambda i: (0, 0)),
            pl.BlockSpec((D, N), lambda i: (0, 0)),
        ],
        out_specs=pl.BlockSpec((tm, N), lambda i: (i, 0)),
        out_shape=jax.ShapeDtypeStruct((R, N), BF16),
        compiler_params=_cparams("parallel"),
        name="memkv",
    )(mem2, gain, w_kv)


def _mem_attn_kernel(q_ref, gate_ref, mk_ref, mv_ref, o_ref):
    scale = MEM_DH ** -0.5
    outs = []
    for h in range(MEM_HEADS):
        sl = slice(h * MEM_DH, (h + 1) * MEM_DH)
        s = lax.dot_general(q_ref[:, sl], mk_ref[:, sl], (((1,), (1,)), ((), ())),
                            preferred_element_type=F32) * scale
        m = jnp.max(s, axis=-1, keepdims=True)
        p = jnp.exp(s - m)
        l = jnp.sum(p, axis=-1, keepdims=True)
        p = (p / l).astype(BF16)
        outs.append(jnp.dot(p, mv_ref[:, sl], preferred_element_type=F32))
    o = jnp.concatenate(outs, axis=1)
    gt = gate_ref[...].astype(F32)
    o_ref[...] = (o * (gt * _sigmoid(gt))).astype(o_ref.dtype)


def _mem_attn(z_main, mkv, *, batch, seq, mem_len, tm):
    T = z_main.shape[0]
    nsb = seq // tm
    qblk = ZM_MQ // MEM_WIDTH
    gblk = ZM_MGATE // MEM_WIDTH
    return pl.pallas_call(
        _mem_attn_kernel,
        grid=(batch, nsb),
        in_specs=[
            pl.BlockSpec((tm, MEM_WIDTH), lambda b, s: (b * nsb + s, qblk)),
            pl.BlockSpec((tm, MEM_WIDTH), lambda b, s: (b * nsb + s, gblk)),
            pl.BlockSpec((mem_len, MEM_WIDTH), lambda b, s: (b, 0)),
            pl.BlockSpec((mem_len, MEM_WIDTH), lambda b, s: (b, 1)),
        ],
        out_specs=pl.BlockSpec((tm, MEM_WIDTH), lambda b, s: (b * nsb + s, 0)),
        out_shape=jax.ShapeDtypeStruct((T, MEM_WIDTH), BF16),
        compiler_params=_cparams("parallel", "parallel"),
        name="mem_attn",
    )(z_main, z_main, mkv, mkv)


def _merge_out_kernel(oa_ref, ob_ref, oc_ref, g0_ref, g1_ref, g2_ref, x_ref, gain_ref,
                      wbr_hbm, wout_hbm, o_ref, wbr_vmem, wout_vmem, sem, *, layer, final_norm):
    @pl.when(pl.program_id(0) == 0)
    def _():
        copies = [pltpu.make_async_copy(wbr_hbm.at[layer], wbr_vmem, sem.at[0]),
                  pltpu.make_async_copy(wout_hbm.at[layer], wout_vmem, sem.at[1])]
        for cp in copies:
            cp.start()
        for cp in copies:
            cp.wait()

    dot = functools.partial(jnp.dot, preferred_element_type=F32)
    r_b, r_c = GLA_WIDTH, GLA_WIDTH + FOX_WIDTH
    y = _sigmoid(g0_ref[...].astype(F32)) * dot(oa_ref[...], wbr_vmem[0:r_b, :])
    y = y + _sigmoid(g1_ref[...].astype(F32)) * dot(ob_ref[...], wbr_vmem[r_b:r_c, :])
    y = y + _sigmoid(g2_ref[...].astype(F32)) * dot(oc_ref[...], wbr_vmem[r_c:, :])
    x = x_ref[...] + dot(y.astype(BF16), wout_vmem[...])
    if final_norm:
        ms = jnp.mean(x * x, axis=-1, keepdims=True)
        x = x * lax.rsqrt(ms + EPS) * gain_ref[...]
    o_ref[...] = x


def _merge_out(o_a, o_b, o_c, z_main, x2, w_branch, w_out, final_gain, *, layer, tm, final_norm):
    T, D = x2.shape
    gblk = ZM_MERGE // D
    return pl.pallas_call(
        functools.partial(_merge_out_kernel, layer=layer, final_norm=final_norm),
        grid=(T // tm,),
        in_specs=[
            pl.BlockSpec((tm, GLA_WIDTH), lambda i: (i, 0)),
            pl.BlockSpec((tm, FOX_WIDTH), lambda i: (i, 0)),
            pl.BlockSpec((tm, MEM_WIDTH), lambda i: (i, 0)),
            pl.BlockSpec((tm, D), lambda i: (i, gblk)),
            pl.BlockSpec((tm, D), lambda i: (i, gblk + 1)),
            pl.BlockSpec((tm, D), lambda i: (i, gblk + 2)),
            pl.BlockSpec((tm, D), lambda i: (i, 0)),
            pl.BlockSpec((1, D), lambda i: (0, 0)),
            pl.BlockSpec(memory_space=pl.ANY),
            pl.BlockSpec(memory_space=pl.ANY),
        ],
        out_specs=pl.BlockSpec((tm, D), lambda i: (i, 0)),
        out_shape=jax.ShapeDtypeStruct((T, D), F32),
        scratch_shapes=[
            pltpu.VMEM(w_branch.shape[1:], BF16),
            pltpu.VMEM(w_out.shape[1:], BF16),
            pltpu.SemaphoreType.DMA((2,)),
        ],
        compiler_params=_cparams("arbitrary"),
        name="merge_out",
    )(o_a, o_b, o_c, z_main, z_main, z_main, x2, final_gain, w_branch, w_out)


W_GDOWN = 2 * GLA_KWIDTH + 2 * GLA_WIDTH
W_FQ = W_GDOWN + GK_RANK
W_FLOGIT = W_FQ + 3 * FOX_WIDTH
W_FGATE = W_FLOGIT + FOX_HEADS


def _pack_src_row(r):
    return (r + jnp.where(r >= W_GDOWN, GK_RANK, 0)
            + jnp.where(r + GK_RANK >= W_FLOGIT, FOX_HEADS, 0))


def _pack_copy(wt_hbm, buf, sem, step, *, tr, nblk):
    layer = step // nblk
    row = pl.multiple_of(_pack_src_row((step % nblk) * tr), 8)
    slot = step % 2
    return pltpu.make_async_copy(wt_hbm.at[layer, pl.ds(row, tr), :], buf.at[slot], sem.at[slot])


def _pack_kernel(wt_hbm, wm_ref, buf, sem, *, tr, nblk):
    s = pl.program_id(0)
    copy = functools.partial(_pack_copy, wt_hbm, buf, sem, tr=tr, nblk=nblk)

    @pl.when(s == 0)
    def _():
        copy(s).start()

    @pl.when(s + 1 < pl.num_programs(0))
    def _():
        copy(s + 1).start()

    copy(s).wait()
    wm_ref[...] = buf[s % 2].astype(BF16)


def _pack_w_in(w_in, *, tr):
    depth, d, in_cols = w_in.shape
    n_main = in_cols - GK_RANK - FOX_HEADS
    nblk = n_main // tr
    wt = jnp.swapaxes(w_in, 1, 2)
    w_main_t = pl.pallas_call(
        functools.partial(_pack_kernel, tr=tr, nblk=nblk),
        grid=(depth * nblk,),
        in_specs=[pl.BlockSpec(memory_space=pl.ANY)],
        out_specs=pl.BlockSpec((None, tr, d), lambda s: (s // nblk, s % nblk, 0)),
        out_shape=jax.ShapeDtypeStruct((depth, n_main, d), BF16),
        scratch_shapes=[pltpu.VMEM((2, tr, d), F32), pltpu.SemaphoreType.DMA((2,))],
        compiler_params=_cparams("arbitrary"),
        name="pack_w_in",
    )(wt)
    pad = jnp.zeros((depth, LANES - GD_COPIES * GK_RANK - FOX_HEADS, d), w_in.dtype)
    w_small_t = jnp.concatenate([wt[:, W_GDOWN:W_FQ, :]] * GD_COPIES
                                + [wt[:, W_FLOGIT:W_FGATE, :], pad], axis=1).astype(BF16)
    return w_main_t, w_small_t


def _trunk(x, mem, norm_gain, w_in, w_gk_up, b_gk, gla_norm_gain, b_f, mem_norm_gain,
           w_mem_kv, w_branch, w_out, final_gain, *, tiles):
    B, S, D = x.shape
    M = mem.shape[1]
    depth = w_in.shape[0]
    x2 = x.reshape(B * S, D)
    mem2 = mem.reshape(B * M, D)
    w_main, w_small = _pack_w_in(w_in, tr=tiles["pack_tr"])
    w_branch_bf = w_branch.astype(BF16)
    w_out_bf = w_out.astype(BF16)
    for l in range(depth):
        z_main, z_small = _inproj(x2, norm_gain[l][None, :], w_main, w_small,
                                  layer=l, tm=tiles["in_tm"], tn=tiles["in_tn"])
        o_a = _gla(z_main, z_small, w_gk_up[l], b_gk[l][None, :], gla_norm_gain[l][None, :],
                   batch=B, seq=S, ts=tiles["gla_ts"], unroll=tiles["gla_unroll"])
        bf_row = jnp.zeros((1, LANES), F32).at[0, ZS_FLOGIT:ZS_FLOGIT + FOX_HEADS].set(b_f[l])
        q_extra, k_extra, v_t = _fox_prep(z_main, z_small, bf_row, batch=B, seq=S, ts=tiles["prep_ts"])
        o_b = _fox_attn(z_main, q_extra, k_extra, v_t, batch=B, seq=S,
                        tq=tiles["fox_tq"], tk=tiles["fox_tk"])
        mkv = _memkv(mem2, mem_norm_gain[l][None, :], w_mem_kv[l].astype(BF16), tm=tiles["memkv_tm"])
        o_c = _mem_attn(z_main, mkv, batch=B, seq=S, mem_len=M, tm=tiles["mem_tm"])
        x2 = _merge_out(o_a, o_b, o_c, z_main, x2, w_branch_bf, w_out_bf, final_gain[None, :],
                        layer=l, tm=tiles["merge_tm"], final_norm=(l == depth - 1))
    return x2.reshape(B, S, D)


_TILES = dict(pack_tr=512, in_tm=1024, in_tn=1536, gla_ts=1024, gla_unroll=8, prep_ts=512,
              fox_tq=512, fox_tk=256,
              memkv_tm=256, mem_tm=512, merge_tm=256)


def kernel(x, mem, norm_gain, w_in, w_gk_up, b_gk, gla_norm_gain, b_f, mem_norm_gain,
           w_mem_kv, w_branch, w_out, final_gain):
    return _trunk(x, mem, norm_gain, w_in, w_gk_up, b_gk, gla_norm_gain, b_f, mem_norm_gain,
                  w_mem_kv, w_branch, w_out, final_gain, tiles=_TILES)
```

```python
import functools

import jax
import jax.numpy as jnp
import numpy as np
from jax import lax
from jax.experimental import pallas as pl
from jax.experimental.pallas import tpu as pltpu

F32 = jnp.float32
BF16 = jnp.bfloat16

EPS = 1e-6

GLA_HEADS, GLA_DK, GLA_DV = 4, 128, 256
GLA_KWIDTH = GLA_HEADS * GLA_DK
GLA_WIDTH = GLA_HEADS * GLA_DV
GK_RANK = 16
GK_NORMALIZER = 16.0
FOX_HEADS, FOX_DH = 8, 64
FOX_WIDTH = FOX_HEADS * FOX_DH
MEM_HEADS, MEM_DH = 4, 128
MEM_WIDTH = MEM_HEADS * MEM_DH
N_BRANCH = 3

LANES = 128
VMEM_LIMIT_BYTES = 56 * 1024 * 1024

ZM_GQ = 0
ZM_GK = ZM_GQ + GLA_KWIDTH
ZM_GV = ZM_GK + GLA_KWIDTH
ZM_GGATE = ZM_GV + GLA_WIDTH
ZM_FQ = ZM_GGATE + GLA_WIDTH
ZM_FK = ZM_FQ + FOX_WIDTH
ZM_FV = ZM_FK + FOX_WIDTH
ZM_FGATE = ZM_FV + FOX_WIDTH
ZM_MQ = ZM_FGATE + FOX_WIDTH
ZM_MGATE = ZM_MQ + MEM_WIDTH
ZM_MERGE = ZM_MGATE + MEM_WIDTH
GD_COPIES = 6
ZS_FLOGIT = GD_COPIES * GK_RANK

GLA_CHUNK = 64
GLA_SUB = 8
FOX_EXTRA = 6
FOX_VROWS = 80
LOG2E = 1.4426950408889634


def _cparams(*sem):
    return pltpu.CompilerParams(dimension_semantics=sem, vmem_limit_bytes=VMEM_LIMIT_BYTES)


def _sigmoid(x):
    return 0.5 * jnp.tanh(0.5 * x) + 0.5


def _log_sigmoid(x):
    return jnp.minimum(x, 0.0) - jnp.log(1.0 + jnp.exp(-jnp.abs(x)))


def _split3(x):
    a = x.astype(BF16)
    r = x - a.astype(F32)
    b = r.astype(BF16)
    c = (r - b.astype(F32)).astype(BF16)
    return a, b, c


def _dot_nt(a, b):
    return lax.dot_general(a, b, (((1,), (1,)), ((), ())), preferred_element_type=F32)


def _tri_cumsum(tri_bf, x):
    a, b, c = _split3(x)
    dot = functools.partial(jnp.dot, preferred_element_type=F32)
    return dot(tri_bf, a) + dot(tri_bf, b) + dot(tri_bf, c)


def _inproj_kernel(x_ref, gain_ref, wm_ref, ws_ref, zm_ref, zs_ref, h_scr):
    @pl.when(pl.program_id(1) == 0)
    def _():
        x = x_ref[...]
        ms = jnp.mean(x * x, axis=-1, keepdims=True)
        h = (x * lax.rsqrt(ms + EPS) * gain_ref[...]).astype(BF16)
        h_scr[...] = h
        zs_ref[...] = _dot_nt(h, ws_ref[...])

    zm_ref[...] = _dot_nt(h_scr[...], wm_ref[...]).astype(BF16)


def _inproj(x2, gain, w_main_t, w_small_t, *, layer, tm, tn):
    T, D = x2.shape
    NM = w_main_t.shape[1]
    return pl.pallas_call(
        _inproj_kernel,
        grid=(T // tm, NM // tn),
        in_specs=[
            pl.BlockSpec((tm, D), lambda i, j: (i, 0)),
            pl.BlockSpec((1, D), lambda i, j: (0, 0)),
            pl.BlockSpec((None, tn, D), lambda i, j: (layer, j, 0)),
            pl.BlockSpec((None, LANES, D), lambda i, j: (layer, 0, 0)),
        ],
        out_specs=[
            pl.BlockSpec((tm, tn), lambda i, j: (i, j)),
            pl.BlockSpec((tm, LANES), lambda i, j: (i, 0)),
        ],
        out_shape=[
            jax.ShapeDtypeStruct((T, NM), BF16),
            jax.ShapeDtypeStruct((T, LANES), F32),
        ],
        scratch_shapes=[pltpu.VMEM((tm, D), BF16)],
        compiler_params=_cparams("parallel", "arbitrary"),
        name="inproj",
    )(x2, gain, w_main_t, w_small_t)


def _gla_kernel(q_ref, k_ref, v_ref, zs_ref, gate_ref, wup_ref, bgk_ref, gain_ref,
                o_ref, state_scr, b_scr, p_scr, *, nchunks, unroll):
    C, SUB = GLA_CHUNK, GLA_SUB
    NSUB = C // SUB
    NPAIR = SUB // 2
    DK, DV = GLA_DK, GLA_DV
    dot = functools.partial(jnp.dot, preferred_element_type=F32)

    @pl.when(pl.program_id(2) == 0)
    def _():
        state_scr[...] = jnp.zeros_like(state_scr)

    row = lax.broadcasted_iota(jnp.int32, (C, C), 0)
    col = lax.broadcasted_iota(jnp.int32, (C, C), 1)
    tri = (col <= row).astype(BF16)
    same_sub = (row // SUB) == (col // SUB)
    causal = col <= row

    z1, z2, z3 = _split3(zs_ref[...])
    zlane = lax.broadcasted_iota(jnp.int32, z1.shape, 1)
    lhs = jnp.where(zlane < 3 * GK_RANK, z1, jnp.where(zlane < 5 * GK_RANK, z2, z3))
    w1, w2, w3 = _split3(wup_ref[...])
    rhs = jnp.concatenate([w1, w2, w3, w1, w2, w1,
                           jnp.zeros((LANES - GD_COPIES * GK_RANK, DK), BF16)], axis=0)
    logits = dot(lhs, rhs) + bgk_ref[...]
    g = _log_sigmoid(logits) * (LOG2E / GK_NORMALIZER)
    for c in range(nchunks):
        b_scr[c * C:(c + 1) * C, :] = _tri_cumsum(tri, g[c * C:(c + 1) * C, :])

    cs_r = lax.broadcasted_iota(jnp.int32, (2 * DK, LANES), 0)
    cs_c = lax.broadcasted_iota(jnp.int32, (2 * DK, LANES), 1)
    chansum = ((cs_r // DK) == (cs_c % 2)).astype(BF16)
    pair_of_lane = (lax.broadcasted_iota(jnp.int32, (C, LANES), 1) % SUB) // 2
    scale = DK ** -0.5
    gain = gain_ref[...]

    def chunk(c, p_slot, state):
        r0 = pl.multiple_of(c * C, C)
        rows = pl.ds(r0, C)
        q = q_ref[rows, :].astype(F32) * scale
        k = k_ref[rows, :].astype(F32)
        v = v_ref[rows, :]
        b = b_scr[rows, :]
        b_last = b[C - 1:C, :]

        o = dot((q * jnp.exp2(b)).astype(BF16), state.astype(BF16))

        cross = [jnp.zeros((SUB, C), F32)]
        for i in range(1, NSUB):
            n = i * SUB
            bref = b[n - 1:n, :]
            q_t = (q[n:n + SUB, :] * jnp.exp2(b[n:n + SUB, :] - bref)).astype(BF16)
            k_t = (k[:n, :] * jnp.exp2(bref - b[:n, :])).astype(BF16)
            k_t = jnp.concatenate([k_t, jnp.zeros((C - n, DK), BF16)], axis=0)
            cross.append(lax.dot_general(q_t, k_t, (((1,), (1,)), ((), ())),
                                         preferred_element_type=F32))
        a_cross = jnp.concatenate(cross, axis=0)

        k3 = k.reshape(NSUB, SUB, DK)
        b3 = b.reshape(NSUB, SUB, DK)
        for j in range(SUB):
            kj = jnp.broadcast_to(k3[:, j:j + 1, :], (NSUB, SUB, DK)).reshape(C, DK)
            bj = jnp.broadcast_to(b3[:, j:j + 1, :], (NSUB, SUB, DK)).reshape(C, DK)
            p = q * kj * jnp.exp2(jnp.minimum(b - bj, 0.0))
            p_slot[(j // 2) * C:(j // 2 + 1) * C, (j % 2) * DK:(j % 2 + 1) * DK] = p.astype(BF16)
        d = dot(p_slot[...], chansum)
        a_diag = d[0:C, :]
        for t in range(1, NPAIR):
            a_diag = jnp.where(pair_of_lane == t, d[t * C:(t + 1) * C, :], a_diag)

        a = jnp.where(causal, jnp.where(same_sub, a_diag[:, :C], a_cross), 0.0)
        o = o + dot(a.astype(BF16), v)

        k_dec = (k * jnp.exp2(b_last - b)).astype(BF16)
        upd = lax.dot_general(k_dec, v, (((0,), (0,)), ((), ())),
                              preferred_element_type=F32)
        decay_col = jnp.transpose(jnp.broadcast_to(jnp.exp2(b_last), (DK, DK)))[:, 0:1]
        new_state = state * decay_col + upd

        ms = jnp.mean(o * o, axis=-1, keepdims=True)
        on = o * lax.rsqrt(ms + EPS) * gain
        gt = gate_ref[rows, :].astype(F32)
        o_ref[rows, :] = (on * (gt * _sigmoid(gt))).astype(o_ref.dtype)
        return new_state

    def group(gi, carry):
        state = state_scr[...]
        for u in range(unroll):
            state = chunk(gi * unroll + u, p_scr.at[u], state)
        state_scr[...] = state
        return carry

    lax.fori_loop(0, nchunks // unroll, group, 0)


def _gla(z_main, z_small, w_up, b_gk, gain, *, batch, seq, ts, unroll):
    T = z_main.shape[0]
    nsb = seq // ts
    nchunks = ts // GLA_CHUNK
    rowmap = lambda b, h_, s: b * nsb + s
    kblk = ZM_GK // GLA_DK
    vblk = ZM_GV // GLA_DV
    gblk = ZM_GGATE // GLA_DV
    return pl.pallas_call(
        functools.partial(_gla_kernel, nchunks=nchunks, unroll=unroll),
        grid=(batch, GLA_HEADS, nsb),
        in_specs=[
            pl.BlockSpec((ts, GLA_DK), lambda b, h_, s: (rowmap(b, h_, s), h_)),
            pl.BlockSpec((ts, GLA_DK), lambda b, h_, s: (rowmap(b, h_, s), kblk + h_)),
            pl.BlockSpec((ts, GLA_DV), lambda b, h_, s: (rowmap(b, h_, s), vblk + h_)),
            pl.BlockSpec((ts, LANES), lambda b, h_, s: (rowmap(b, h_, s), 0)),
            pl.BlockSpec((ts, GLA_DV), lambda b, h_, s: (rowmap(b, h_, s), gblk + h_)),
            pl.BlockSpec((GK_RANK, GLA_DK), lambda b, h_, s: (0, h_)),
            pl.BlockSpec((1, GLA_DK), lambda b, h_, s: (0, h_)),
            pl.BlockSpec((1, GLA_DV), lambda b, h_, s: (0, 0)),
        ],
        out_specs=pl.BlockSpec((ts, GLA_DV), lambda b, h_, s: (rowmap(b, h_, s), h_)),
        out_shape=jax.ShapeDtypeStruct((T, GLA_WIDTH), BF16),
        scratch_shapes=[
            pltpu.VMEM((GLA_DK, GLA_DV), F32),
            pltpu.VMEM((ts, GLA_DK), F32),
            pltpu.VMEM((unroll, GLA_SUB // 2 * GLA_CHUNK, 2 * GLA_DK), BF16),
        ],
        compiler_params=_cparams("parallel", "parallel", "arbitrary"),
        name="gla",
    )(z_main, z_main, z_main, z_small, z_main, w_up, b_gk, gain)


def _fox_extra_tables():
    npair = FOX_HEADS // 2
    pq = np.zeros((3 * LANES, npair * LANES), np.float32)
    pk = np.zeros((3 * LANES, npair * LANES), np.float32)
    cq = np.zeros((1, npair * LANES), np.float32)
    ck = np.zeros((1, npair * LANES), np.float32)
    for h in range(FOX_HEADS):
        base = (h // 2) * LANES + (h % 2) * FOX_EXTRA
        for t in range(3):
            pq[t * LANES + ZS_FLOGIT + h, base + t] = 1.0
            cq[0, base + 3 + t] = 1.0
            ck[0, base + t] = 1.0
            pk[t * LANES + ZS_FLOGIT + h, base + 3 + t] = -1.0
    return jnp.asarray(pq, BF16), jnp.asarray(pk, BF16), jnp.asarray(cq), jnp.asarray(ck)


def _fox_prep_kernel(v_ref, zs_ref, bf_ref, pq_ref, pk_ref, cq_ref, ck_ref,
                     qx_ref, kx_ref, vt_ref, carry_scr, *, ts):
    @pl.when(pl.program_id(1) == 0)
    def _():
        carry_scr[...] = jnp.zeros_like(carry_scr)

    dot = functools.partial(jnp.dot, preferred_element_type=F32)
    row = lax.broadcasted_iota(jnp.int32, (ts, ts), 0)
    col = lax.broadcasted_iota(jnp.int32, (ts, ts), 1)
    tri = (col <= row).astype(BF16)
    log_f = _log_sigmoid(zs_ref[...] + bf_ref[...])
    c_all = _tri_cumsum(tri, log_f) + carry_scr[...]
    carry_scr[...] = c_all[ts - 1:ts, :]

    cs = jnp.concatenate(_split3(c_all * LOG2E), axis=1)
    qx_ref[...] = (dot(cs, pq_ref[...]) + cq_ref[...]).astype(BF16)
    kx_ref[...] = (dot(cs, pk_ref[...]) + ck_ref[...]).astype(BF16)

    v_t = jnp.transpose(v_ref[...].astype(F32))
    pad_rows = FOX_VROWS - FOX_DH
    ones_blk = jnp.where(lax.broadcasted_iota(jnp.int32, (pad_rows, ts), 0) == 0, 1.0, 0.0)
    pieces = []
    for h in range(FOX_HEADS):
        pieces += [v_t[h * FOX_DH:(h + 1) * FOX_DH, :], ones_blk]
    vt_ref[...] = jnp.concatenate(pieces, axis=0).astype(BF16)


def _fox_prep(z_main, z_small, bf_row, *, batch, seq, ts):
    T = z_main.shape[0]
    nsb = seq // ts
    vblk = ZM_FV // FOX_WIDTH
    npair = FOX_HEADS // 2
    extra = jax.ShapeDtypeStruct((T, npair * LANES), BF16)
    pq, pk, cq, ck = _fox_extra_tables()
    const = lambda b, s: (0, 0)
    return pl.pallas_call(
        functools.partial(_fox_prep_kernel, ts=ts),
        grid=(batch, nsb),
        in_specs=[
            pl.BlockSpec((ts, FOX_WIDTH), lambda b, s: (b * nsb + s, vblk)),
            pl.BlockSpec((ts, LANES), lambda b, s: (b * nsb + s, 0)),
            pl.BlockSpec((1, LANES), const),
            pl.BlockSpec(pq.shape, const),
            pl.BlockSpec(pk.shape, const),
            pl.BlockSpec(cq.shape, const),
            pl.BlockSpec(ck.shape, const),
        ],
        out_specs=[
            pl.BlockSpec((ts, npair * LANES), lambda b, s: (b * nsb + s, 0)),
            pl.BlockSpec((ts, npair * LANES), lambda b, s: (b * nsb + s, 0)),
            pl.BlockSpec((FOX_HEADS * FOX_VROWS, ts), lambda b, s: (b, s)),
        ],
        out_shape=[extra, extra,
                   jax.ShapeDtypeStruct((batch * FOX_HEADS * FOX_VROWS, seq), BF16)],
        scratch_shapes=[pltpu.VMEM((1, LANES), F32)],
        compiler_params=_cparams("parallel", "arbitrary"),
        name="fox_prep",
    )(z_main, z_small, bf_row, pq, pk, cq, ck)


def _fox_attn_kernel(q_ref, qx_ref, k_ref, kx_ref, vt_ref, gate_ref, o_ref, m_scr, acc_scr, *, tq, tk):
    qi = pl.program_id(2)
    m_scr[...] = jnp.full_like(m_scr, -jnp.inf)
    acc_scr[...] = jnp.zeros_like(acc_scr)

    q_scaled = (q_ref[...].astype(F32) * (FOX_DH ** -0.5 * LOG2E)).astype(BF16)
    q2 = jnp.concatenate([q_scaled, qx_ref[...]], axis=1)
    lane = lax.broadcasted_iota(jnp.int32, (1, 2 * LANES), 1)
    head_lanes = [((lane >= hh * FOX_DH) & (lane < (hh + 1) * FOX_DH))
                  | ((lane >= LANES + hh * FOX_EXTRA) & (lane < LANES + (hh + 1) * FOX_EXTRA))
                  for hh in range(2)]

    def scores(j, hh):
        k0 = pl.multiple_of(j * tk, tk)
        k2 = jnp.concatenate([k_ref[pl.ds(k0, tk), :], kx_ref[pl.ds(k0, tk), :]], axis=1)
        k_h = jnp.where(head_lanes[hh], k2, jnp.zeros_like(k2))
        return _dot_nt(k_h, q2)

    def absorb(j, hh, st, stats, key_offset):
        m_prev, acc = stats
        if key_offset is not None:
            k_pos = key_offset + lax.broadcasted_iota(jnp.int32, (tk, tq), 0)
            q_pos = lax.broadcasted_iota(jnp.int32, (tk, tq), 1)
            st = jnp.where(k_pos <= q_pos, st, -jnp.inf)
        m_new = jnp.maximum(m_prev, jnp.max(st, axis=0, keepdims=True))
        alpha = jnp.exp2(m_prev - m_new)
        p = jnp.exp2(st - m_new).astype(BF16)
        k0 = pl.multiple_of(j * tk, tk)
        vt = vt_ref[hh * FOX_VROWS:(hh + 1) * FOX_VROWS, pl.ds(k0, tk)]
        pv = jnp.dot(vt, p, preferred_element_type=F32)
        return m_new, acc * alpha + pv

    def run(chunks, key_offsets):
        sts = [[scores(j, hh) for hh in range(2)] for j in chunks]
        stats = [(m_scr[hh], acc_scr[hh]) for hh in range(2)]
        for ci, j in enumerate(chunks):
            for hh in range(2):
                stats[hh] = absorb(j, hh, sts[ci][hh], stats[hh], key_offsets[ci])
        for hh in range(2):
            m_scr[hh], acc_scr[hh] = stats[hh]

    sub = tq // tk
    span = lambda s: [sub * s + c for c in range(sub)]
    own_offsets = [c * tk for c in range(sub)]

    def pair(sp, carry):
        run(span(2 * sp) + span(2 * sp + 1), [None] * (2 * sub))
        return carry

    lax.fori_loop(0, qi // 2, pair, 0)

    @pl.when(qi % 2 == 1)
    def _():
        run(span(qi - 1) + span(qi), [None] * sub + own_offsets)

    @pl.when(qi % 2 == 0)
    def _():
        run(span(qi), own_offsets)

    outs = []
    for hh in range(2):
        acc = acc_scr[hh]
        outs.append(acc[:FOX_DH, :] / acc[FOX_DH:FOX_DH + 1, :])
    ot = jnp.concatenate(outs, axis=0)
    gt = gate_ref[...].astype(F32)
    o_ref[...] = (jnp.transpose(ot) * (gt * _sigmoid(gt))).astype(o_ref.dtype)


def _fox_attn(z_main, q_extra, k_extra, v_t, *, batch, seq, tq, tk):
    T = z_main.shape[0]
    nq = seq // tq
    npair = FOX_HEADS // 2
    qblk = ZM_FQ // LANES
    kblk = ZM_FK // LANES
    gblk = ZM_FGATE // LANES
    return pl.pallas_call(
        functools.partial(_fox_attn_kernel, tq=tq, tk=tk),
        grid=(batch, npair, nq),
        in_specs=[
            pl.BlockSpec((tq, LANES), lambda b, p, i: (b * nq + i, qblk + p)),
            pl.BlockSpec((tq, LANES), lambda b, p, i: (b * nq + i, p)),
            pl.BlockSpec((seq, LANES), lambda b, p, i: (b, kblk + p)),
            pl.BlockSpec((seq, LANES), lambda b, p, i: (b, p)),
            pl.BlockSpec((2 * FOX_VROWS, seq), lambda b, p, i: (b * npair + p, 0)),
            pl.BlockSpec((tq, LANES), lambda b, p, i: (b * nq + i, gblk + p)),
        ],
        out_specs=pl.BlockSpec((tq, LANES), lambda b, p, i: (b * nq + i, p)),
        out_shape=jax.ShapeDtypeStruct((T, FOX_WIDTH), BF16),
        scratch_shapes=[
            pltpu.VMEM((2, 1, tq), F32),
            pltpu.VMEM((2, FOX_VROWS, tq), F32),
        ],
        compiler_params=_cparams("parallel", "parallel", "arbitrary"),
        name="fox_attn",
    )(z_main, q_extra, z_main, k_extra, v_t, z_main)


def _memkv_kernel(mem_ref, gain_ref, w_ref, o_ref):
    x = mem_ref[...]
    ms = jnp.mean(x * x, axis=-1, keepdims=True)
    h = (x * lax.rsqrt(ms + EPS) * gain_ref[...]).astype(BF16)
    o_ref[...] = jnp.dot(h, w_ref[...], preferred_element_type=F32).astype(o_ref.dtype)


def _memkv(mem2, gain, w_kv, *, tm):
    R, D = mem2.shape
    N = w_kv.shape[1]
    return pl.pallas_call(
        _memkv_kernel,
        grid=(R // tm,),
        in_specs=[
            pl.BlockSpec((tm, D), lambda i: (i, 0)),
            pl.BlockSpec((1, D), lambda i: (0, 0)),
            pl.BlockSpec((D, N), lambda i: (0, 0)),
        ],
        out_specs=pl.BlockSpec((tm, N), lambda i: (i, 0)),
        out_shape=jax.ShapeDtypeStruct((R, N), BF16),
        compiler_params=_cparams("parallel"),
        name="memkv",
    )(mem2, gain, w_kv)


def _mem_attn_kernel(q_ref, gate_ref, mk_ref, mv_ref, o_ref):
    scale = MEM_DH ** -0.5 * LOG2E
    outs = []
    for h in range(MEM_HEADS):
        sl = slice(h * MEM_DH, (h + 1) * MEM_DH)
        s = _dot_nt(q_ref[:, sl], mk_ref[:, sl]) * scale
        m = jnp.max(s, axis=-1, keepdims=True)
        p = jnp.exp2(s - m)
        l = jnp.sum(p, axis=-1, keepdims=True)
        pv = jnp.dot(p.astype(BF16), mv_ref[:, sl], preferred_element_type=F32)
        outs.append(pv / l)
    o = jnp.concatenate(outs, axis=1)
    gt = gate_ref[...].astype(F32)
    o_ref[...] = (o * (gt * _sigmoid(gt))).astype(o_ref.dtype)


def _mem_attn(z_main, mkv, *, batch, seq, mem_len, tm):
    T = z_main.shape[0]
    nsb = seq // tm
    qblk = ZM_MQ // MEM_WIDTH
    gblk = ZM_MGATE // MEM_WIDTH
    return pl.pallas_call(
        _mem_attn_kernel,
        grid=(batch, nsb),
        in_specs=[
            pl.BlockSpec((tm, MEM_WIDTH), lambda b, s: (b * nsb + s, qblk)),
            pl.BlockSpec((tm, MEM_WIDTH), lambda b, s: (b * nsb + s, gblk)),
            pl.BlockSpec((mem_len, MEM_WIDTH), lambda b, s: (b, 0)),
            pl.BlockSpec((mem_len, MEM_WIDTH), lambda b, s: (b, 1)),
        ],
        out_specs=pl.BlockSpec((tm, MEM_WIDTH), lambda b, s: (b * nsb + s, 0)),
        out_shape=jax.ShapeDtypeStruct((T, MEM_WIDTH), BF16),
        compiler_params=_cparams("parallel", "parallel"),
        name="mem_attn",
    )(z_main, z_main, mkv, mkv)


def _merge_out_kernel(oa_ref, ob_ref, oc_ref, g0_ref, g1_ref, g2_ref, x_ref, gain_ref,
                      wbr_hbm, wout_hbm, o_ref, wbr_vmem, wout_vmem, sem, *, layer, final_norm):
    @pl.when(pl.program_id(0) == 0)
    def _():
        copies = [pltpu.make_async_copy(wbr_hbm.at[layer], wbr_vmem, sem.at[0]),
                  pltpu.make_async_copy(wout_hbm.at[layer], wout_vmem, sem.at[1])]
        for cp in copies:
            cp.start()
        for cp in copies:
            cp.wait()

    dot = functools.partial(jnp.dot, preferred_element_type=F32)
    r_b, r_c = GLA_WIDTH, GLA_WIDTH + FOX_WIDTH
    y = _sigmoid(g0_ref[...].astype(F32)) * dot(oa_ref[...], wbr_vmem[0:r_b, :])
    y = y + _sigmoid(g1_ref[...].astype(F32)) * dot(ob_ref[...], wbr_vmem[r_b:r_c, :])
    y = y + _sigmoid(g2_ref[...].astype(F32)) * dot(oc_ref[...], wbr_vmem[r_c:, :])
    x = x_ref[...] + dot(y.astype(BF16), wout_vmem[...])
    if final_norm:
        ms = jnp.mean(x * x, axis=-1, keepdims=True)
        x = x * lax.rsqrt(ms + EPS) * gain_ref[...]
    o_ref[...] = x


def _merge_out(o_a, o_b, o_c, z_main, x2, w_branch, w_out, final_gain, *, layer, tm, final_norm):
    T, D = x2.shape
    gblk = ZM_MERGE // D
    return pl.pallas_call(
        functools.partial(_merge_out_kernel, layer=layer, final_norm=final_norm),
        grid=(T // tm,),
        in_specs=[
            pl.BlockSpec((tm, GLA_WIDTH), lambda i: (i, 0)),
            pl.BlockSpec((tm, FOX_WIDTH), lambda i: (i, 0)),
            pl.BlockSpec((tm, MEM_WIDTH), lambda i: (i, 0)),
            pl.BlockSpec((tm, D), lambda i: (i, gblk)),
            pl.BlockSpec((tm, D), lambda i: (i, gblk + 1)),
            pl.BlockSpec((tm, D), lambda i: (i, gblk + 2)),
            pl.BlockSpec((tm, D), lambda i: (i, 0)),
            pl.BlockSpec((1, D), lambda i: (0, 0)),
            pl.BlockSpec(memory_space=pl.ANY),
            pl.BlockSpec(memory_space=pl.ANY),
        ],
        out_specs=pl.BlockSpec((tm, D), lambda i: (i, 0)),
        out_shape=jax.ShapeDtypeStruct((T, D), F32),
        scratch_shapes=[
            pltpu.VMEM(w_branch.shape[1:], BF16),
            pltpu.VMEM(w_out.shape[1:], BF16),
            pltpu.SemaphoreType.DMA((2,)),
        ],
        compiler_params=_cparams("arbitrary"),
        name="merge_out",
    )(o_a, o_b, o_c, z_main, z_main, z_main, x2, final_gain, w_branch, w_out)


W_GDOWN = 2 * GLA_KWIDTH + 2 * GLA_WIDTH
W_FQ = W_GDOWN + GK_RANK
W_FLOGIT = W_FQ + 3 * FOX_WIDTH
W_FGATE = W_FLOGIT + FOX_HEADS


def _pack_src_row(r):
    return (r + jnp.where(r >= W_GDOWN, GK_RANK, 0)
            + jnp.where(r + GK_RANK >= W_FLOGIT, FOX_HEADS, 0))


def _pack_copy(wt_hbm, buf, sem, step, *, tr, nblk):
    layer = step // nblk
    row = pl.multiple_of(_pack_src_row((step % nblk) * tr), 8)
    slot = step % 2
    return pltpu.make_async_copy(wt_hbm.at[layer, pl.ds(row, tr), :], buf.at[slot], sem.at[slot])


def _pack_kernel(wt_hbm, wm_ref, buf, sem, *, tr, nblk):
    s = pl.program_id(0)
    copy = functools.partial(_pack_copy, wt_hbm, buf, sem, tr=tr, nblk=nblk)

    @pl.when(s == 0)
    def _():
        copy(s).start()

    @pl.when(s + 1 < pl.num_programs(0))
    def _():
        copy(s + 1).start()

    copy(s).wait()
    wm_ref[...] = buf[s % 2].astype(BF16)


def _pack_w_in(w_in, *, tr):
    depth, d, in_cols = w_in.shape
    n_main = in_cols - GK_RANK - FOX_HEADS
    nblk = n_main // tr
    wt = jnp.swapaxes(w_in, 1, 2)
    w_main_t = pl.pallas_call(
        functools.partial(_pack_kernel, tr=tr, nblk=nblk),
        grid=(depth * nblk,),
        in_specs=[pl.BlockSpec(memory_space=pl.ANY)],
        out_specs=pl.BlockSpec((None, tr, d), lambda s: (s // nblk, s % nblk, 0)),
        out_shape=jax.ShapeDtypeStruct((depth, n_main, d), BF16),
        scratch_shapes=[pltpu.VMEM((2, tr, d), F32), pltpu.SemaphoreType.DMA((2,))],
        compiler_params=_cparams("arbitrary"),
        name="pack_w_in",
    )(wt)
    pad = jnp.zeros((depth, LANES - GD_COPIES * GK_RANK - FOX_HEADS, d), w_in.dtype)
    w_small_t = jnp.concatenate([wt[:, W_GDOWN:W_FQ, :]] * GD_COPIES
                                + [wt[:, W_FLOGIT:W_FGATE, :], pad], axis=1).astype(BF16)
    return w_main_t, w_small_t


def _trunk(x, mem, norm_gain, w_in, w_gk_up, b_gk, gla_norm_gain, b_f, mem_norm_gain,
           w_mem_kv, w_branch, w_out, final_gain, *, tiles):
    B, S, D = x.shape
    M = mem.shape[1]
    depth = w_in.shape[0]
    x2 = x.reshape(B * S, D)
    mem2 = mem.reshape(B * M, D)
    w_main, w_small = _pack_w_in(w_in, tr=tiles["pack_tr"])
    w_branch_bf = w_branch.astype(BF16)
    w_out_bf = w_out.astype(BF16)
    for l in range(depth):
        z_main, z_small = _inproj(x2, norm_gain[l][None, :], w_main, w_small,
                                  layer=l, tm=tiles["in_tm"], tn=tiles["in_tn"])
        o_a = _gla(z_main, z_small, w_gk_up[l], b_gk[l][None, :], gla_norm_gain[l][None, :],
                   batch=B, seq=S, ts=tiles["gla_ts"], unroll=tiles["gla_unroll"])
        bf_row = jnp.zeros((1, LANES), F32).at[0, ZS_FLOGIT:ZS_FLOGIT + FOX_HEADS].set(b_f[l])
        q_extra, k_extra, v_t = _fox_prep(z_main, z_small, bf_row, batch=B, seq=S, ts=tiles["prep_ts"])
        o_b = _fox_attn(z_main, q_extra, k_extra, v_t, batch=B, seq=S,
                        tq=tiles["fox_tq"], tk=tiles["fox_tk"])
        mkv = _memkv(mem2, mem_norm_gain[l][None, :], w_mem_kv[l].astype(BF16), tm=tiles["memkv_tm"])
        o_c = _mem_attn(z_main, mkv, batch=B, seq=S, mem_len=M, tm=tiles["mem_tm"])
        x2 = _merge_out(o_a, o_b, o_c, z_main, x2, w_branch_bf, w_out_bf, final_gain[None, :],
                        layer=l, tm=tiles["merge_tm"], final_norm=(l == depth - 1))
    return x2.reshape(B, S, D)


_TILES = dict(pack_tr=512, in_tm=1024, in_tn=1536, gla_ts=2048, gla_unroll=16, prep_ts=512,
              fox_tq=512, fox_tk=256,
              memkv_tm=256, mem_tm=512, merge_tm=256)


def kernel(x, mem, norm_gain, w_in, w_gk_up, b_gk, gla_norm_gain, b_f, mem_norm_gain,
           w_mem_kv, w_branch, w_out, final_gain):
    return _trunk(x, mem, norm_gain, w_in, w_gk_up, b_gk, gla_norm_gain, b_f, mem_norm_gain,
                  w_mem_kv, w_branch, w_out, final_gain, tiles=_TILES)
```

```python
import functools

import jax
import jax.numpy as jnp
import numpy as np
from jax import lax
from jax.experimental import pallas as pl
from jax.experimental.pallas import tpu as pltpu

F32 = jnp.float32
BF16 = jnp.bfloat16

EPS = 1e-6

GLA_HEADS, GLA_DK, GLA_DV = 4, 128, 256
GLA_KWIDTH = GLA_HEADS * GLA_DK
GLA_WIDTH = GLA_HEADS * GLA_DV
GK_RANK = 16
GK_NORMALIZER = 16.0
FOX_HEADS, FOX_DH = 8, 64
FOX_WIDTH = FOX_HEADS * FOX_DH
MEM_HEADS, MEM_DH = 4, 128
MEM_WIDTH = MEM_HEADS * MEM_DH
N_BRANCH = 3

LANES = 128
VMEM_LIMIT_BYTES = 56 * 1024 * 1024

ZM_GQ = 0
ZM_GK = ZM_GQ + GLA_KWIDTH
ZM_GV = ZM_GK + GLA_KWIDTH
ZM_GGATE = ZM_GV + GLA_WIDTH
ZM_FQ = ZM_GGATE + GLA_WIDTH
ZM_FK = ZM_FQ + FOX_WIDTH
ZM_FV = ZM_FK + FOX_WIDTH
ZM_FGATE = ZM_FV + FOX_WIDTH
ZM_MQ = ZM_FGATE + FOX_WIDTH
ZM_MGATE = ZM_MQ + MEM_WIDTH
ZM_MERGE = ZM_MGATE + MEM_WIDTH
GD_COPIES = 6
ZS_FLOGIT = GD_COPIES * GK_RANK

GLA_CHUNK = 64
GLA_SUB = 8
FOX_EXTRA = 6
FOX_VROWS = 80
LOG2E = 1.4426950408889634


def _cparams(*sem):
    return pltpu.CompilerParams(dimension_semantics=sem, vmem_limit_bytes=VMEM_LIMIT_BYTES)


def _sigmoid(x):
    return 0.5 * jnp.tanh(0.5 * x) + 0.5


def _log_sigmoid(x):
    return jnp.minimum(x, 0.0) - jnp.log(1.0 + jnp.exp(-jnp.abs(x)))


def _split3(x):
    a = x.astype(BF16)
    r = x - a.astype(F32)
    b = r.astype(BF16)
    c = (r - b.astype(F32)).astype(BF16)
    return a, b, c


def _dot_nt(a, b):
    return lax.dot_general(a, b, (((1,), (1,)), ((), ())), preferred_element_type=F32)


def _tri_cumsum(tri_bf, x):
    a, b, c = _split3(x)
    dot = functools.partial(jnp.dot, preferred_element_type=F32)
    return dot(tri_bf, a) + dot(tri_bf, b) + dot(tri_bf, c)


def _inproj_kernel(x_ref, gain_ref, wm_ref, ws_ref, zm_ref, zs_ref, h_scr):
    @pl.when(pl.program_id(1) == 0)
    def _():
        x = x_ref[...]
        ms = jnp.mean(x * x, axis=-1, keepdims=True)
        h = (x * lax.rsqrt(ms + EPS) * gain_ref[...]).astype(BF16)
        h_scr[...] = h
        zs_ref[...] = _dot_nt(h, ws_ref[...])

    zm_ref[...] = _dot_nt(h_scr[...], wm_ref[...]).astype(BF16)


def _inproj(x2, gain, w_main_t, w_small_t, *, layer, tm, tn):
    T, D = x2.shape
    NM = w_main_t.shape[1]
    return pl.pallas_call(
        _inproj_kernel,
        grid=(T // tm, NM // tn),
        in_specs=[
            pl.BlockSpec((tm, D), lambda i, j: (i, 0)),
            pl.BlockSpec((1, D), lambda i, j: (0, 0)),
            pl.BlockSpec((None, tn, D), lambda i, j: (layer, j, 0)),
            pl.BlockSpec((None, LANES, D), lambda i, j: (layer, 0, 0)),
        ],
        out_specs=[
            pl.BlockSpec((tm, tn), lambda i, j: (i, j)),
            pl.BlockSpec((tm, LANES), lambda i, j: (i, 0)),
        ],
        out_shape=[
            jax.ShapeDtypeStruct((T, NM), BF16),
            jax.ShapeDtypeStruct((T, LANES), F32),
        ],
        scratch_shapes=[pltpu.VMEM((tm, D), BF16)],
        compiler_params=_cparams("parallel", "arbitrary"),
        name="inproj",
    )(x2, gain, w_main_t, w_small_t)


def _gla_kernel(q_ref, k_ref, v_ref, zs_ref, gate_ref, wup_ref, bgk_ref, gain_ref,
                o_ref, state_scr, b_scr, p_scr, *, nchunks, unroll):
    C, SUB = GLA_CHUNK, GLA_SUB
    NSUB = C // SUB
    NPAIR = SUB // 2
    DK, DV = GLA_DK, GLA_DV
    dot = functools.partial(jnp.dot, preferred_element_type=F32)

    @pl.when(pl.program_id(2) == 0)
    def _():
        state_scr[...] = jnp.zeros_like(state_scr)

    row = lax.broadcasted_iota(jnp.int32, (C, C), 0)
    col = lax.broadcasted_iota(jnp.int32, (C, C), 1)
    tri = (col <= row).astype(BF16)
    same_sub = (row // SUB) == (col // SUB)
    causal = col <= row

    z1, z2, z3 = _split3(zs_ref[...])
    zlane = lax.broadcasted_iota(jnp.int32, z1.shape, 1)
    lhs = jnp.where(zlane < 3 * GK_RANK, z1, jnp.where(zlane < 5 * GK_RANK, z2, z3))
    w1, w2, w3 = _split3(wup_ref[...])
    rhs = jnp.concatenate([w1, w2, w3, w1, w2, w1,
                           jnp.zeros((LANES - GD_COPIES * GK_RANK, DK), BF16)], axis=0)
    logits = dot(lhs, rhs) + bgk_ref[...]
    g = _log_sigmoid(logits) * (LOG2E / GK_NORMALIZER)
    for c in range(nchunks):
        b_scr[c * C:(c + 1) * C, :] = _tri_cumsum(tri, g[c * C:(c + 1) * C, :])

    cs_r = lax.broadcasted_iota(jnp.int32, (2 * DK, LANES), 0)
    cs_c = lax.broadcasted_iota(jnp.int32, (2 * DK, LANES), 1)
    chansum = ((cs_r // DK) == (cs_c % 2)).astype(BF16)
    pair_of_lane = (lax.broadcasted_iota(jnp.int32, (C, LANES), 1) % SUB) // 2
    scale = DK ** -0.5
    gain = gain_ref[...]

    def chunk(c, p_slot, state):
        r0 = pl.multiple_of(c * C, C)
        rows = pl.ds(r0, C)
        q = q_ref[rows, :].astype(F32) * scale
        k = k_ref[rows, :].astype(F32)
        v = v_ref[rows, :]
        b = b_scr[rows, :]
        b_last = b[C - 1:C, :]

        o = dot((q * jnp.exp2(b)).astype(BF16), state.astype(BF16))

        cross = [jnp.zeros((SUB, C), F32)]
        for i in range(1, NSUB):
            n = i * SUB
            bref = b[n - 1:n, :]
            q_t = (q[n:n + SUB, :] * jnp.exp2(b[n:n + SUB, :] - bref)).astype(BF16)
            k_t = (k[:n, :] * jnp.exp2(bref - b[:n, :])).astype(BF16)
            k_t = jnp.concatenate([k_t, jnp.zeros((C - n, DK), BF16)], axis=0)
            cross.append(lax.dot_general(q_t, k_t, (((1,), (1,)), ((), ())),
                                         preferred_element_type=F32))
        a_cross = jnp.concatenate(cross, axis=0)

        k3 = k.reshape(NSUB, SUB, DK)
        b3 = b.reshape(NSUB, SUB, DK)
        for j in range(SUB):
            kj = jnp.broadcast_to(k3[:, j:j + 1, :], (NSUB, SUB, DK)).reshape(C, DK)
            bj = jnp.broadcast_to(b3[:, j:j + 1, :], (NSUB, SUB, DK)).reshape(C, DK)
            p = q * kj * jnp.exp2(jnp.minimum(b - bj, 0.0))
            p_slot[(j // 2) * C:(j // 2 + 1) * C, (j % 2) * DK:(j % 2 + 1) * DK] = p.astype(BF16)
        d = dot(p_slot[...], chansum)
        a_diag = d[0:C, :]
        for t in range(1, NPAIR):
            a_diag = jnp.where(pair_of_lane == t, d[t * C:(t + 1) * C, :], a_diag)

        a = jnp.where(causal, jnp.where(same_sub, a_diag[:, :C], a_cross), 0.0)
        o = o + dot(a.astype(BF16), v)

        k_dec = (k * jnp.exp2(b_last - b)).astype(BF16)
        upd = lax.dot_general(k_dec, v, (((0,), (0,)), ((), ())),
                              preferred_element_type=F32)
        decay_col = jnp.transpose(jnp.broadcast_to(jnp.exp2(b_last), (DK, DK)))[:, 0:1]
        new_state = state * decay_col + upd

        ms = jnp.mean(o * o, axis=-1, keepdims=True)
        on = o * lax.rsqrt(ms + EPS) * gain
        gt = gate_ref[rows, :].astype(F32)
        o_ref[rows, :] = (on * (gt * _sigmoid(gt))).astype(o_ref.dtype)
        return new_state

    def group(gi, carry):
        state = state_scr[...]
        for u in range(unroll):
            state = chunk(gi * unroll + u, p_scr.at[u], state)
        state_scr[...] = state
        return carry

    lax.fori_loop(0, nchunks // unroll, group, 0)


def _gla(z_main, z_small, w_up, b_gk, gain, *, batch, seq, ts, unroll):
    T = z_main.shape[0]
    nsb = seq // ts
    nchunks = ts // GLA_CHUNK
    rowmap = lambda b, h_, s: b * nsb + s
    kblk = ZM_GK // GLA_DK
    vblk = ZM_GV // GLA_DV
    gblk = ZM_GGATE // GLA_DV
    return pl.pallas_call(
        functools.partial(_gla_kernel, nchunks=nchunks, unroll=unroll),
        grid=(batch, GLA_HEADS, nsb),
        in_specs=[
            pl.BlockSpec((ts, GLA_DK), lambda b, h_, s: (rowmap(b, h_, s), h_)),
            pl.BlockSpec((ts, GLA_DK), lambda b, h_, s: (rowmap(b, h_, s), kblk + h_)),
            pl.BlockSpec((ts, GLA_DV), lambda b, h_, s: (rowmap(b, h_, s), vblk + h_)),
            pl.BlockSpec((ts, LANES), lambda b, h_, s: (rowmap(b, h_, s), 0)),
            pl.BlockSpec((ts, GLA_DV), lambda b, h_, s: (rowmap(b, h_, s), gblk + h_)),
            pl.BlockSpec((GK_RANK, GLA_DK), lambda b, h_, s: (0, h_)),
            pl.BlockSpec((1, GLA_DK), lambda b, h_, s: (0, h_)),
            pl.BlockSpec((1, GLA_DV), lambda b, h_, s: (0, 0)),
        ],
        out_specs=pl.BlockSpec((ts, GLA_DV), lambda b, h_, s: (rowmap(b, h_, s), h_)),
        out_shape=jax.ShapeDtypeStruct((T, GLA_WIDTH), BF16),
        scratch_shapes=[
            pltpu.VMEM((GLA_DK, GLA_DV), F32),
            pltpu.VMEM((ts, GLA_DK), F32),
            pltpu.VMEM((unroll, GLA_SUB // 2 * GLA_CHUNK, 2 * GLA_DK), BF16),
        ],
        compiler_params=_cparams("parallel", "parallel", "arbitrary"),
        name="gla",
    )(z_main, z_main, z_main, z_small, z_main, w_up, b_gk, gain)


def _fox_extra_tables():
    npair = FOX_HEADS // 2
    pq = np.zeros((3 * LANES, npair * LANES), np.float32)
    pk = np.zeros((3 * LANES, npair * LANES), np.float32)
    cq = np.zeros((1, npair * LANES), np.float32)
    ck = np.zeros((1, npair * LANES), np.float32)
    for h in range(FOX_HEADS):
        base = (h // 2) * LANES + (h % 2) * FOX_EXTRA
        for t in range(3):
            pq[t * LANES + ZS_FLOGIT + h, base + t] = 1.0
            cq[0, base + 3 + t] = 1.0
            ck[0, base + t] = 1.0
            pk[t * LANES + ZS_FLOGIT + h, base + 3 + t] = -1.0
    return jnp.asarray(pq, BF16), jnp.asarray(pk, BF16), jnp.asarray(cq), jnp.asarray(ck)


def _fox_prep_kernel(v_ref, zs_ref, bf_ref, pq_ref, pk_ref, cq_ref, ck_ref,
                     qx_ref, kx_ref, vt_ref, carry_scr, *, ts):
    @pl.when(pl.program_id(1) == 0)
    def _():
        carry_scr[...] = jnp.zeros_like(carry_scr)

    dot = functools.partial(jnp.dot, preferred_element_type=F32)
    row = lax.broadcasted_iota(jnp.int32, (ts, ts), 0)
    col = lax.broadcasted_iota(jnp.int32, (ts, ts), 1)
    tri = (col <= row).astype(BF16)
    log_f = _log_sigmoid(zs_ref[...] + bf_ref[...])
    c_all = _tri_cumsum(tri, log_f) + carry_scr[...]
    carry_scr[...] = c_all[ts - 1:ts, :]

    cs = jnp.concatenate(_split3(c_all * LOG2E), axis=1)
    qx_ref[...] = (dot(cs, pq_ref[...]) + cq_ref[...]).astype(BF16)
    kx_ref[...] = (dot(cs, pk_ref[...]) + ck_ref[...]).astype(BF16)

    v_t = jnp.transpose(v_ref[...].astype(F32))
    pad_rows = FOX_VROWS - FOX_DH
    ones_blk = jnp.where(lax.broadcasted_iota(jnp.int32, (pad_rows, ts), 0) == 0, 1.0, 0.0)
    pieces = []
    for h in range(FOX_HEADS):
        pieces += [v_t[h * FOX_DH:(h + 1) * FOX_DH, :], ones_blk]
    vt_ref[...] = jnp.concatenate(pieces, axis=0).astype(BF16)


def _fox_prep(z_main, z_small, bf_row, *, batch, seq, ts):
    T = z_main.shape[0]
    nsb = seq // ts
    vblk = ZM_FV // FOX_WIDTH
    npair = FOX_HEADS // 2
    extra = jax.ShapeDtypeStruct((T, npair * LANES), BF16)
    pq, pk, cq, ck = _fox_extra_tables()
    const = lambda b, s: (0, 0)
    return pl.pallas_call(
        functools.partial(_fox_prep_kernel, ts=ts),
        grid=(batch, nsb),
        in_specs=[
            pl.BlockSpec((ts, FOX_WIDTH), lambda b, s: (b * nsb + s, vblk)),
            pl.BlockSpec((ts, LANES), lambda b, s: (b * nsb + s, 0)),
            pl.BlockSpec((1, LANES), const),
            pl.BlockSpec(pq.shape, const),
            pl.BlockSpec(pk.shape, const),
            pl.BlockSpec(cq.shape, const),
            pl.BlockSpec(ck.shape, const),
        ],
        out_specs=[
            pl.BlockSpec((ts, npair * LANES), lambda b, s: (b * nsb + s, 0)),
            pl.BlockSpec((ts, npair * LANES), lambda b, s: (b * nsb + s, 0)),
            pl.BlockSpec((FOX_HEADS * FOX_VROWS, ts), lambda b, s: (b, s)),
        ],
        out_shape=[extra, extra,
                   jax.ShapeDtypeStruct((batch * FOX_HEADS * FOX_VROWS, seq), BF16)],
        scratch_shapes=[pltpu.VMEM((1, LANES), F32)],
        compiler_params=_cparams("parallel", "arbitrary"),
        name="fox_prep",
    )(z_main, z_small, bf_row, pq, pk, cq, ck)


def _fox_attn_kernel(q_ref, qx_ref, k_ref, kx_ref, vt_ref, gate_ref, o_ref, m_scr, acc_scr, *, tq, tk):
    qi = pl.program_id(2)
    m_scr[...] = jnp.full_like(m_scr, -jnp.inf)
    acc_scr[...] = jnp.zeros_like(acc_scr)

    q_scaled = (q_ref[...].astype(F32) * (FOX_DH ** -0.5 * LOG2E)).astype(BF16)
    q2 = jnp.concatenate([q_scaled, qx_ref[...]], axis=1)
    lane = lax.broadcasted_iota(jnp.int32, (1, 2 * LANES), 1)
    head_lanes = [((lane >= hh * FOX_DH) & (lane < (hh + 1) * FOX_DH))
                  | ((lane >= LANES + hh * FOX_EXTRA) & (lane < LANES + (hh + 1) * FOX_EXTRA))
                  for hh in range(2)]

    def scores(j, hh):
        k0 = pl.multiple_of(j * tk, tk)
        k2 = jnp.concatenate([k_ref[pl.ds(k0, tk), :], kx_ref[pl.ds(k0, tk), :]], axis=1)
        k_h = jnp.where(head_lanes[hh], k2, jnp.zeros_like(k2))
        return _dot_nt(k_h, q2)

    def absorb(j, hh, st, stats, key_offset):
        m_prev, acc = stats
        if key_offset is not None:
            k_pos = key_offset + lax.broadcasted_iota(jnp.int32, (tk, tq), 0)
            q_pos = lax.broadcasted_iota(jnp.int32, (tk, tq), 1)
            st = jnp.where(k_pos <= q_pos, st, -jnp.inf)
        m_new = jnp.maximum(m_prev, jnp.max(st, axis=0, keepdims=True))
        alpha = jnp.exp2(m_prev - m_new)
        p = jnp.exp2(st - m_new).astype(BF16)
        k0 = pl.multiple_of(j * tk, tk)
        vt = vt_ref[hh * FOX_VROWS:(hh + 1) * FOX_VROWS, pl.ds(k0, tk)]
        pv = jnp.dot(vt, p, preferred_element_type=F32)
        return m_new, acc * alpha + pv

    def run(chunks, key_offsets):
        sts = [[scores(j, hh) for hh in range(2)] for j in chunks]
        stats = [(m_scr[hh], acc_scr[hh]) for hh in range(2)]
        for ci, j in enumerate(chunks):
            for hh in range(2):
                stats[hh] = absorb(j, hh, sts[ci][hh], stats[hh], key_offsets[ci])
        for hh in range(2):
            m_scr[hh], acc_scr[hh] = stats[hh]

    sub = tq // tk
    span = lambda s: [sub * s + c for c in range(sub)]
    own_offsets = [c * tk for c in range(sub)]

    def pair(sp, carry):
        run(span(2 * sp) + span(2 * sp + 1), [None] * (2 * sub))
        return carry

    lax.fori_loop(0, qi // 2, pair, 0)

    @pl.when(qi % 2 == 1)
    def _():
        run(span(qi - 1) + span(qi), [None] * sub + own_offsets)

    @pl.when(qi % 2 == 0)
    def _():
        run(span(qi), own_offsets)

    outs = []
    for hh in range(2):
        acc = acc_scr[hh]
        outs.append(acc[:FOX_DH, :] / acc[FOX_DH:FOX_DH + 1, :])
    ot = jnp.concatenate(outs, axis=0)
    gt = gate_ref[...].astype(F32)
    o_ref[...] = (jnp.transpose(ot) * (gt * _sigmoid(gt))).astype(o_ref.dtype)


def _fox_attn(z_main, q_extra, k_extra, v_t, *, batch, seq, tq, tk):
    T = z_main.shape[0]
    nq = seq // tq
    npair = FOX_HEADS // 2
    qblk = ZM_FQ // LANES
    kblk = ZM_FK // LANES
    gblk = ZM_FGATE // LANES
    return pl.pallas_call(
        functools.partial(_fox_attn_kernel, tq=tq, tk=tk),
        grid=(batch, npair, nq),
        in_specs=[
            pl.BlockSpec((tq, LANES), lambda b, p, i: (b * nq + i, qblk + p)),
            pl.BlockSpec((tq, LANES), lambda b, p, i: (b * nq + i, p)),
            pl.BlockSpec((seq, LANES), lambda b, p, i: (b, kblk + p)),
            pl.BlockSpec((seq, LANES), lambda b, p, i: (b, p)),
            pl.BlockSpec((2 * FOX_VROWS, seq), lambda b, p, i: (b * npair + p, 0)),
            pl.BlockSpec((tq, LANES), lambda b, p, i: (b * nq + i, gblk + p)),
        ],
        out_specs=pl.BlockSpec((tq, LANES), lambda b, p, i: (b * nq + i, p)),
        out_shape=jax.ShapeDtypeStruct((T, FOX_WIDTH), BF16),
        scratch_shapes=[
            pltpu.VMEM((2, 1, tq), F32),
            pltpu.VMEM((2, FOX_VROWS, tq), F32),
        ],
        compiler_params=_cparams("parallel", "parallel", "arbitrary"),
        name="fox_attn",
    )(z_main, q_extra, z_main, k_extra, v_t, z_main)


def _memkv_kernel(mem_ref, gain_ref, w_ref, o_ref):
    x = mem_ref[...]
    ms = jnp.mean(x * x, axis=-1, keepdims=True)
    h = (x * lax.rsqrt(ms + EPS) * gain_ref[...]).astype(BF16)
    o_ref[...] = jnp.dot(h, w_ref[...], preferred_element_type=F32).astype(o_ref.dtype)


def _memkv(mem2, gain, w_kv, *, tm):
    R, D = mem2.shape
    N = w_kv.shape[1]
    return pl.pallas_call(
        _memkv_kernel,
        grid=(R // tm,),
        in_specs=[
            pl.BlockSpec((tm, D), lambda i: (i, 0)),
            pl.BlockSpec((1, D), lambda i: (0, 0)),
            pl.BlockSpec((D, N), lambda i: (0, 0)),
        ],
        out_specs=pl.BlockSpec((tm, N), lambda i: (i, 0)),
        out_shape=jax.ShapeDtypeStruct((R, N), BF16),
        compiler_params=_cparams("parallel"),
        name="memkv",
    )(mem2, gain, w_kv)


def _mem_attn_kernel(q_ref, gate_ref, mk_ref, mv_ref, o_ref):
    scale = MEM_DH ** -0.5 * LOG2E
    outs = []
    for h in range(MEM_HEADS):
        sl = slice(h * MEM_DH, (h + 1) * MEM_DH)
        s = _dot_nt(q_ref[:, sl], mk_ref[:, sl]) * scale
        m = jnp.max(s, axis=-1, keepdims=True)
        p = jnp.exp2(s - m)
        l = jnp.sum(p, axis=-1, keepdims=True)
        pv = jnp.dot(p.astype(BF16), mv_ref[:, sl], preferred_element_type=F32)
        outs.append(pv / l)
    o = jnp.concatenate(outs, axis=1)
    gt = gate_ref[...].astype(F32)
    o_ref[...] = (o * (gt * _sigmoid(gt))).astype(o_ref.dtype)


def _mem_attn(z_main, mkv, *, batch, seq, mem_len, tm):
    T = z_main.shape[0]
    nsb = seq // tm
    qblk = ZM_MQ // MEM_WIDTH
    gblk = ZM_MGATE // MEM_WIDTH
    return pl.pallas_call(
        _mem_attn_kernel,
        grid=(batch, nsb),
        in_specs=[
            pl.BlockSpec((tm, MEM_WIDTH), lambda b, s: (b * nsb + s, qblk)),
            pl.BlockSpec((tm, MEM_WIDTH), lambda b, s: (b * nsb + s, gblk)),
            pl.BlockSpec((mem_len, MEM_WIDTH), lambda b, s: (b, 0)),
            pl.BlockSpec((mem_len, MEM_WIDTH), lambda b, s: (b, 1)),
        ],
        out_specs=pl.BlockSpec((tm, MEM_WIDTH), lambda b, s: (b * nsb + s, 0)),
        out_shape=jax.ShapeDtypeStruct((T, MEM_WIDTH), BF16),
        compiler_params=_cparams("parallel", "parallel"),
        name="mem_attn",
    )(z_main, z_main, mkv, mkv)


def _merge_out_kernel(oa_ref, ob_ref, oc_ref, g0_ref, g1_ref, g2_ref, x_ref, gain_ref,
                      wbr_hbm, wout_hbm, o_ref, wbr_vmem, wout_vmem, sem, *, layer, final_norm):
    @pl.when(pl.program_id(0) == 0)
    def _():
        copies = [pltpu.make_async_copy(wbr_hbm.at[layer], wbr_vmem, sem.at[0]),
                  pltpu.make_async_copy(wout_hbm.at[layer], wout_vmem, sem.at[1])]
        for cp in copies:
            cp.start()
        for cp in copies:
            cp.wait()

    dot = functools.partial(jnp.dot, preferred_element_type=F32)
    r_b, r_c = GLA_WIDTH, GLA_WIDTH + FOX_WIDTH
    y = _sigmoid(g0_ref[...].astype(F32)) * dot(oa_ref[...], wbr_vmem[0:r_b, :])
    y = y + _sigmoid(g1_ref[...].astype(F32)) * dot(ob_ref[...], wbr_vmem[r_b:r_c, :])
    y = y + _sigmoid(g2_ref[...].astype(F32)) * dot(oc_ref[...], wbr_vmem[r_c:, :])
    x = x_ref[...] + dot(y.astype(BF16), wout_vmem[...])
    if final_norm:
        ms = jnp.mean(x * x, axis=-1, keepdims=True)
        x = x * lax.rsqrt(ms + EPS) * gain_ref[...]
    o_ref[...] = x


def _merge_out(o_a, o_b, o_c, z_main, x2, w_branch, w_out, final_gain, *, layer, tm, final_norm):
    T, D = x2.shape
    gblk = ZM_MERGE // D
    return pl.pallas_call(
        functools.partial(_merge_out_kernel, layer=layer, final_norm=final_norm),
        grid=(T // tm,),
        in_specs=[
            pl.BlockSpec((tm, GLA_WIDTH), lambda i: (i, 0)),
            pl.BlockSpec((tm, FOX_WIDTH), lambda i: (i, 0)),
            pl.BlockSpec((tm, MEM_WIDTH), lambda i: (i, 0)),
            pl.BlockSpec((tm, D), lambda i: (i, gblk)),
            pl.BlockSpec((tm, D), lambda i: (i, gblk + 1)),
            pl.BlockSpec((tm, D), lambda i: (i, gblk + 2)),
            pl.BlockSpec((tm, D), lambda i: (i, 0)),
            pl.BlockSpec((1, D), lambda i: (0, 0)),
            pl.BlockSpec(memory_space=pl.ANY),
            pl.BlockSpec(memory_space=pl.ANY),
        ],
        out_specs=pl.BlockSpec((tm, D), lambda i: (i, 0)),
        out_shape=jax.ShapeDtypeStruct((T, D), F32),
        scratch_shapes=[
            pltpu.VMEM(w_branch.shape[1:], BF16),
            pltpu.VMEM(w_out.shape[1:], BF16),
            pltpu.SemaphoreType.DMA((2,)),
        ],
        compiler_params=_cparams("arbitrary"),
        name="merge_out",
    )(o_a, o_b, o_c, z_main, z_main, z_main, x2, final_gain, w_branch, w_out)


W_GDOWN = 2 * GLA_KWIDTH + 2 * GLA_WIDTH
W_FQ = W_GDOWN + GK_RANK
W_FLOGIT = W_FQ + 3 * FOX_WIDTH
W_FGATE = W_FLOGIT + FOX_HEADS


def _pack_src_row(r):
    return (r + jnp.where(r >= W_GDOWN, GK_RANK, 0)
            + jnp.where(r + GK_RANK >= W_FLOGIT, FOX_HEADS, 0))


def _pack_copy(wt_hbm, buf, sem, step, *, tr, nblk):
    layer = step // nblk
    row = pl.multiple_of(_pack_src_row((step % nblk) * tr), 8)
    slot = step % 2
    return pltpu.make_async_copy(wt_hbm.at[layer, pl.ds(row, tr), :], buf.at[slot], sem.at[slot])


def _pack_kernel(wt_hbm, wm_ref, buf, sem, *, tr, nblk):
    s = pl.program_id(0)
    copy = functools.partial(_pack_copy, wt_hbm, buf, sem, tr=tr, nblk=nblk)

    @pl.when(s == 0)
    def _():
        copy(s).start()

    @pl.when(s + 1 < pl.num_programs(0))
    def _():
        copy(s + 1).start()

    copy(s).wait()
    wm_ref[...] = buf[s % 2].astype(BF16)


def _pack_w_in(w_in, *, tr):
    depth, d, in_cols = w_in.shape
    n_main = in_cols - GK_RANK - FOX_HEADS
    nblk = n_main // tr
    wt = jnp.swapaxes(w_in, 1, 2)
    w_main_t = pl.pallas_call(
        functools.partial(_pack_kernel, tr=tr, nblk=nblk),
        grid=(depth * nblk,),
        in_specs=[pl.BlockSpec(memory_space=pl.ANY)],
        out_specs=pl.BlockSpec((None, tr, d), lambda s: (s // nblk, s % nblk, 0)),
        out_shape=jax.ShapeDtypeStruct((depth, n_main, d), BF16),
        scratch_shapes=[pltpu.VMEM((2, tr, d), F32), pltpu.SemaphoreType.DMA((2,))],
        compiler_params=_cparams("arbitrary"),
        name="pack_w_in",
    )(wt)
    pad = jnp.zeros((depth, LANES - GD_COPIES * GK_RANK - FOX_HEADS, d), w_in.dtype)
    w_small_t = jnp.concatenate([wt[:, W_GDOWN:W_FQ, :]] * GD_COPIES
                                + [wt[:, W_FLOGIT:W_FGATE, :], pad], axis=1).astype(BF16)
    return w_main_t, w_small_t


def _trunk(x, mem, norm_gain, w_in, w_gk_up, b_gk, gla_norm_gain, b_f, mem_norm_gain,
           w_mem_kv, w_branch, w_out, final_gain, *, tiles):
    B, S, D = x.shape
    M = mem.shape[1]
    depth = w_in.shape[0]
    x2 = x.reshape(B * S, D)
    mem2 = mem.reshape(B * M, D)
    w_main, w_small = _pack_w_in(w_in, tr=tiles["pack_tr"])
    w_branch_bf = w_branch.astype(BF16)
    w_out_bf = w_out.astype(BF16)
    for l in range(depth):
        z_main, z_small = _inproj(x2, norm_gain[l][None, :], w_main, w_small,
                                  layer=l, tm=tiles["in_tm"], tn=tiles["in_tn"])
        o_a = _gla(z_main, z_small, w_gk_up[l], b_gk[l][None, :], gla_norm_gain[l][None, :],
                   batch=B, seq=S, ts=tiles["gla_ts"], unroll=tiles["gla_unroll"])
        bf_row = jnp.zeros((1, LANES), F32).at[0, ZS_FLOGIT:ZS_FLOGIT + FOX_HEADS].set(b_f[l])
        q_extra, k_extra, v_t = _fox_prep(z_main, z_small, bf_row, batch=B, seq=S, ts=tiles["prep_ts"])
        o_b = _fox_attn(z_main, q_extra, k_extra, v_t, batch=B, seq=S,
                        tq=tiles["fox_tq"], tk=tiles["fox_tk"])
        mkv = _memkv(mem2, mem_norm_gain[l][None, :], w_mem_kv[l].astype(BF16), tm=tiles["memkv_tm"])
        o_c = _mem_attn(z_main, mkv, batch=B, seq=S, mem_len=M, tm=tiles["mem_tm"])
        x2 = _merge_out(o_a, o_b, o_c, z_main, x2, w_branch_bf, w_out_bf, final_gain[None, :],
                        layer=l, tm=tiles["merge_tm"], final_norm=(l == depth - 1))
    return x2.reshape(B, S, D)


_TILES = dict(pack_tr=512, in_tm=1024, in_tn=1536, gla_ts=2048, gla_unroll=16, prep_ts=512,
              fox_tq=512, fox_tk=256,
              memkv_tm=256, mem_tm=1024, merge_tm=256)


def kernel(x, mem, norm_gain, w_in, w_gk_up, b_gk, gla_norm_gain, b_f, mem_norm_gain,
           w_mem_kv, w_branch, w_out, final_gain):
    return _trunk(x, mem, norm_gain, w_in, w_gk_up, b_gk, gla_norm_gain, b_f, mem_norm_gain,
                  w_mem_kv, w_branch, w_out, final_gain, tiles=_TILES)
```

```python
import functools

import jax
import jax.numpy as jnp
import numpy as np
from jax import lax
from jax.experimental import pallas as pl
from jax.experimental.pallas import tpu as pltpu

F32 = jnp.float32
BF16 = jnp.bfloat16

EPS = 1e-6

GLA_HEADS, GLA_DK, GLA_DV = 4, 128, 256
GLA_KWIDTH = GLA_HEADS * GLA_DK
GLA_WIDTH = GLA_HEADS * GLA_DV
GK_RANK = 16
GK_NORMALIZER = 16.0
FOX_HEADS, FOX_DH = 8, 64
FOX_WIDTH = FOX_HEADS * FOX_DH
MEM_HEADS, MEM_DH = 4, 128
MEM_WIDTH = MEM_HEADS * MEM_DH
N_BRANCH = 3

LANES = 128
VMEM_LIMIT_BYTES = 56 * 1024 * 1024

ZM_GQ = 0
ZM_GK = ZM_GQ + GLA_KWIDTH
ZM_GV = ZM_GK + GLA_KWIDTH
ZM_GGATE = ZM_GV + GLA_WIDTH
ZM_FQ = ZM_GGATE + GLA_WIDTH
ZM_FK = ZM_FQ + FOX_WIDTH
ZM_FV = ZM_FK + FOX_WIDTH
ZM_FGATE = ZM_FV + FOX_WIDTH
ZM_MQ = ZM_FGATE + FOX_WIDTH
ZM_MGATE = ZM_MQ + MEM_WIDTH
ZM_MERGE = ZM_MGATE + MEM_WIDTH
GD_COPIES = 6
ZS_FLOGIT = GD_COPIES * GK_RANK

GLA_CHUNK = 64
GLA_SUB = 8
FOX_EXTRA = 6
FOX_VROWS = 80
LOG2E = 1.4426950408889634


def _cparams(*sem):
    return pltpu.CompilerParams(dimension_semantics=sem, vmem_limit_bytes=VMEM_LIMIT_BYTES)


def _sigmoid(x):
    return 0.5 * jnp.tanh(0.5 * x) + 0.5


def _log_sigmoid(x):
    return jnp.minimum(x, 0.0) - jnp.log(1.0 + jnp.exp(-jnp.abs(x)))


def _split3(x):
    a = x.astype(BF16)
    r = x - a.astype(F32)
    b = r.astype(BF16)
    c = (r - b.astype(F32)).astype(BF16)
    return a, b, c


def _dot_nt(a, b):
    return lax.dot_general(a, b, (((1,), (1,)), ((), ())), preferred_element_type=F32)


def _tri_cumsum(tri_bf, x):
    a, b, c = _split3(x)
    dot = functools.partial(jnp.dot, preferred_element_type=F32)
    return dot(tri_bf, a) + dot(tri_bf, b) + dot(tri_bf, c)


def _inproj_kernel(x_ref, gain_ref, wm_ref, ws_ref, zm_ref, zs_ref, h_scr):
    @pl.when(pl.program_id(1) == 0)
    def _():
        x = x_ref[...]
        ms = jnp.mean(x * x, axis=-1, keepdims=True)
        h = (x * lax.rsqrt(ms + EPS) * gain_ref[...]).astype(BF16)
        h_scr[...] = h
        zs_ref[...] = _dot_nt(h, ws_ref[...])

    zm_ref[...] = _dot_nt(h_scr[...], wm_ref[...]).astype(BF16)


def _inproj(x2, gain, w_main_t, w_small_t, *, layer, tm, tn):
    T, D = x2.shape
    NM = w_main_t.shape[1]
    return pl.pallas_call(
        _inproj_kernel,
        grid=(T // tm, NM // tn),
        in_specs=[
            pl.BlockSpec((tm, D), lambda i, j: (i, 0)),
            pl.BlockSpec((1, D), lambda i, j: (0, 0)),
            pl.BlockSpec((None, tn, D), lambda i, j: (layer, j, 0)),
            pl.BlockSpec((None, LANES, D), lambda i, j: (layer, 0, 0)),
        ],
        out_specs=[
            pl.BlockSpec((tm, tn), lambda i, j: (i, j)),
            pl.BlockSpec((tm, LANES), lambda i, j: (i, 0)),
        ],
        out_shape=[
            jax.ShapeDtypeStruct((T, NM), BF16),
            jax.ShapeDtypeStruct((T, LANES), F32),
        ],
        scratch_shapes=[pltpu.VMEM((tm, D), BF16)],
        compiler_params=_cparams("parallel", "arbitrary"),
        name="inproj",
    )(x2, gain, w_main_t, w_small_t)


def _gla_kernel(q_ref, k_ref, v_ref, zs_ref, gate_ref, wup_ref, bgk_ref, gain_ref,
                o_ref, state_scr, b_scr, p_scr, *, nchunks, unroll):
    C, SUB = GLA_CHUNK, GLA_SUB
    NSUB = C // SUB
    NPAIR = SUB // 2
    DK, DV = GLA_DK, GLA_DV
    dot = functools.partial(jnp.dot, preferred_element_type=F32)

    @pl.when(pl.program_id(2) == 0)
    def _():
        state_scr[...] = jnp.zeros_like(state_scr)

    row = lax.broadcasted_iota(jnp.int32, (C, C), 0)
    col = lax.broadcasted_iota(jnp.int32, (C, C), 1)
    tri = (col <= row).astype(BF16)
    same_sub = (row // SUB) == (col // SUB)
    causal = col <= row

    z1, z2, z3 = _split3(zs_ref[...])
    zlane = lax.broadcasted_iota(jnp.int32, z1.shape, 1)
    lhs = jnp.where(zlane < 3 * GK_RANK, z1, jnp.where(zlane < 5 * GK_RANK, z2, z3))
    w1, w2, w3 = _split3(wup_ref[...])
    rhs = jnp.concatenate([w1, w2, w3, w1, w2, w1,
                           jnp.zeros((LANES - GD_COPIES * GK_RANK, DK), BF16)], axis=0)
    logits = dot(lhs, rhs) + bgk_ref[...]
    g = _log_sigmoid(logits) * (LOG2E / GK_NORMALIZER)
    for c in range(nchunks):
        b_scr[c * C:(c + 1) * C, :] = _tri_cumsum(tri, g[c * C:(c + 1) * C, :])

    cs_r = lax.broadcasted_iota(jnp.int32, (2 * DK, LANES), 0)
    cs_c = lax.broadcasted_iota(jnp.int32, (2 * DK, LANES), 1)
    chansum = ((cs_r // DK) == (cs_c % 2)).astype(BF16)
    pair_of_lane = (lax.broadcasted_iota(jnp.int32, (C, LANES), 1) % SUB) // 2
    scale = DK ** -0.5
    gain = gain_ref[...]

    def chunk(c, p_slot, state):
        r0 = pl.multiple_of(c * C, C)
        rows = pl.ds(r0, C)
        q = q_ref[rows, :].astype(F32) * scale
        k = k_ref[rows, :].astype(F32)
        v = v_ref[rows, :]
        b = b_scr[rows, :]
        b_last = b[C - 1:C, :]

        o = dot((q * jnp.exp2(b)).astype(BF16), state.astype(BF16))

        cross = [jnp.zeros((SUB, C), F32)]
        for i in range(1, NSUB):
            n = i * SUB
            bref = b[n - 1:n, :]
            q_t = (q[n:n + SUB, :] * jnp.exp2(b[n:n + SUB, :] - bref)).astype(BF16)
            k_t = (k[:n, :] * jnp.exp2(bref - b[:n, :])).astype(BF16)
            k_t = jnp.concatenate([k_t, jnp.zeros((C - n, DK), BF16)], axis=0)
            cross.append(lax.dot_general(q_t, k_t, (((1,), (1,)), ((), ())),
                                         preferred_element_type=F32))
        a_cross = jnp.concatenate(cross, axis=0)

        k3 = k.reshape(NSUB, SUB, DK)
        b3 = b.reshape(NSUB, SUB, DK)
        for j in range(SUB):
            kj = jnp.broadcast_to(k3[:, j:j + 1, :], (NSUB, SUB, DK)).reshape(C, DK)
            bj = jnp.broadcast_to(b3[:, j:j + 1, :], (NSUB, SUB, DK)).reshape(C, DK)
            p = q * kj * jnp.exp2(jnp.minimum(b - bj, 0.0))
            p_slot[(j // 2) * C:(j // 2 + 1) * C, (j % 2) * DK:(j % 2 + 1) * DK] = p.astype(BF16)
        d = dot(p_slot[...], chansum)
        a_diag = d[0:C, :]
        for t in range(1, NPAIR):
            a_diag = jnp.where(pair_of_lane == t, d[t * C:(t + 1) * C, :], a_diag)

        a = jnp.where(causal, jnp.where(same_sub, a_diag[:, :C], a_cross), 0.0)
        o = o + dot(a.astype(BF16), v)

        k_dec = (k * jnp.exp2(b_last - b)).astype(BF16)
        upd = lax.dot_general(k_dec, v, (((0,), (0,)), ((), ())),
                              preferred_element_type=F32)
        decay_col = jnp.transpose(jnp.broadcast_to(jnp.exp2(b_last), (DK, DK)))[:, 0:1]
        new_state = state * decay_col + upd

        ms = jnp.mean(o * o, axis=-1, keepdims=True)
        on = o * lax.rsqrt(ms + EPS) * gain
        gt = gate_ref[rows, :].astype(F32)
        o_ref[rows, :] = (on * (gt * _sigmoid(gt))).astype(o_ref.dtype)
        return new_state

    def group(gi, carry):
        state = state_scr[...]
        for u in range(unroll):
            state = chunk(gi * unroll + u, p_scr.at[u], state)
        state_scr[...] = state
        return carry

    lax.fori_loop(0, nchunks // unroll, group, 0)


def _gla(z_main, z_small, w_up, b_gk, gain, *, batch, seq, ts, unroll):
    T = z_main.shape[0]
    nsb = seq // ts
    nchunks = ts // GLA_CHUNK
    rowmap = lambda b, h_, s: b * nsb + s
    kblk = ZM_GK // GLA_DK
    vblk = ZM_GV // GLA_DV
    gblk = ZM_GGATE // GLA_DV
    return pl.pallas_call(
        functools.partial(_gla_kernel, nchunks=nchunks, unroll=unroll),
        grid=(batch, GLA_HEADS, nsb),
        in_specs=[
            pl.BlockSpec((ts, GLA_DK), lambda b, h_, s: (rowmap(b, h_, s), h_)),
            pl.BlockSpec((ts, GLA_DK), lambda b, h_, s: (rowmap(b, h_, s), kblk + h_)),
            pl.BlockSpec((ts, GLA_DV), lambda b, h_, s: (rowmap(b, h_, s), vblk + h_)),
            pl.BlockSpec((ts, LANES), lambda b, h_, s: (rowmap(b, h_, s), 0)),
            pl.BlockSpec((ts, GLA_DV), lambda b, h_, s: (rowmap(b, h_, s), gblk + h_)),
            pl.BlockSpec((GK_RANK, GLA_DK), lambda b, h_, s: (0, h_)),
            pl.BlockSpec((1, GLA_DK), lambda b, h_, s: (0, h_)),
            pl.BlockSpec((1, GLA_DV), lambda b, h_, s: (0, 0)),
        ],
        out_specs=pl.BlockSpec((ts, GLA_DV), lambda b, h_, s: (rowmap(b, h_, s), h_)),
        out_shape=jax.ShapeDtypeStruct((T, GLA_WIDTH), BF16),
        scratch_shapes=[
            pltpu.VMEM((GLA_DK, GLA_DV), F32),
            pltpu.VMEM((ts, GLA_DK), F32),
            pltpu.VMEM((unroll, GLA_SUB // 2 * GLA_CHUNK, 2 * GLA_DK), BF16),
        ],
        compiler_params=_cparams("parallel", "parallel", "arbitrary"),
        name="gla",
    )(z_main, z_main, z_main, z_small, z_main, w_up, b_gk, gain)


def _fox_extra_tables():
    npair = FOX_HEADS // 2
    pq = np.zeros((3 * LANES, npair * LANES), np.float32)
    pk = np.zeros((3 * LANES, npair * LANES), np.float32)
    cq = np.zeros((1, npair * LANES), np.float32)
    ck = np.zeros((1, npair * LANES), np.float32)
    for h in range(FOX_HEADS):
        base = (h // 2) * LANES + (h % 2) * FOX_EXTRA
        for t in range(3):
            pq[t * LANES + ZS_FLOGIT + h, base + t] = 1.0
            cq[0, base + 3 + t] = 1.0
            ck[0, base + t] = 1.0
            pk[t * LANES + ZS_FLOGIT + h, base + 3 + t] = -1.0
    return jnp.asarray(pq, BF16), jnp.asarray(pk, BF16), jnp.asarray(cq), jnp.asarray(ck)


def _fox_prep_kernel(v_ref, zs_ref, bf_ref, pq_ref, pk_ref, cq_ref, ck_ref,
                     qx_ref, kx_ref, vt_ref, carry_scr, *, ts):
    @pl.when(pl.program_id(1) == 0)
    def _():
        carry_scr[...] = jnp.zeros_like(carry_scr)

    dot = functools.partial(jnp.dot, preferred_element_type=F32)
    row = lax.broadcasted_iota(jnp.int32, (ts, ts), 0)
    col = lax.broadcasted_iota(jnp.int32, (ts, ts), 1)
    tri = (col <= row).astype(BF16)
    log_f = _log_sigmoid(zs_ref[...] + bf_ref[...])
    c_all = _tri_cumsum(tri, log_f) + carry_scr[...]
    carry_scr[...] = c_all[ts - 1:ts, :]

    cs = jnp.concatenate(_split3(c_all * LOG2E), axis=1)
    qx_ref[...] = (dot(cs, pq_ref[...]) + cq_ref[...]).astype(BF16)
    kx_ref[...] = (dot(cs, pk_ref[...]) + ck_ref[...]).astype(BF16)

    v_t = jnp.transpose(v_ref[...].astype(F32))
    pad_rows = FOX_VROWS - FOX_DH
    ones_blk = jnp.where(lax.broadcasted_iota(jnp.int32, (pad_rows, ts), 0) == 0, 1.0, 0.0)
    pieces = []
    for h in range(FOX_HEADS):
        pieces += [v_t[h * FOX_DH:(h + 1) * FOX_DH, :], ones_blk]
    vt_ref[...] = jnp.concatenate(pieces, axis=0).astype(BF16)


def _fox_prep(z_main, z_small, bf_row, *, batch, seq, ts):
    T = z_main.shape[0]
    nsb = seq // ts
    vblk = ZM_FV // FOX_WIDTH
    npair = FOX_HEADS // 2
    extra = jax.ShapeDtypeStruct((T, npair * LANES), BF16)
    pq, pk, cq, ck = _fox_extra_tables()
    const = lambda b, s: (0, 0)
    return pl.pallas_call(
        functools.partial(_fox_prep_kernel, ts=ts),
        grid=(batch, nsb),
        in_specs=[
            pl.BlockSpec((ts, FOX_WIDTH), lambda b, s: (b * nsb + s, vblk)),
            pl.BlockSpec((ts, LANES), lambda b, s: (b * nsb + s, 0)),
            pl.BlockSpec((1, LANES), const),
            pl.BlockSpec(pq.shape, const),
            pl.BlockSpec(pk.shape, const),
            pl.BlockSpec(cq.shape, const),
            pl.BlockSpec(ck.shape, const),
        ],
        out_specs=[
            pl.BlockSpec((ts, npair * LANES), lambda b, s: (b * nsb + s, 0)),
            pl.BlockSpec((ts, npair * LANES), lambda b, s: (b * nsb + s, 0)),
            pl.BlockSpec((FOX_HEADS * FOX_VROWS, ts), lambda b, s: (b, s)),
        ],
        out_shape=[extra, extra,
                   jax.ShapeDtypeStruct((batch * FOX_HEADS * FOX_VROWS, seq), BF16)],
        scratch_shapes=[pltpu.VMEM((1, LANES), F32)],
        compiler_params=_cparams("parallel", "arbitrary"),
        name="fox_prep",
    )(z_main, z_small, bf_row, pq, pk, cq, ck)


def _fox_attn_kernel(q_ref, qx_ref, k_ref, kx_ref, vt_ref, gate_ref, o_ref, m_scr, acc_scr, *, tq, tk):
    qi = pl.program_id(2)
    m_scr[...] = jnp.full_like(m_scr, -jnp.inf)
    acc_scr[...] = jnp.zeros_like(acc_scr)

    q_scaled = (q_ref[...].astype(F32) * (FOX_DH ** -0.5 * LOG2E)).astype(BF16)
    q2 = jnp.concatenate([q_scaled, qx_ref[...]], axis=1)
    lane = lax.broadcasted_iota(jnp.int32, (1, 2 * LANES), 1)
    head_lanes = [((lane >= hh * FOX_DH) & (lane < (hh + 1) * FOX_DH))
                  | ((lane >= LANES + hh * FOX_EXTRA) & (lane < LANES + (hh + 1) * FOX_EXTRA))
                  for hh in range(2)]

    def scores(j, hh):
        k0 = pl.multiple_of(j * tk, tk)
        k2 = jnp.concatenate([k_ref[pl.ds(k0, tk), :], kx_ref[pl.ds(k0, tk), :]], axis=1)
        k_h = jnp.where(head_lanes[hh], k2, jnp.zeros_like(k2))
        return _dot_nt(k_h, q2)

    def absorb(j, hh, st, stats, key_offset):
        m_prev, acc = stats
        if key_offset is not None:
            k_pos = key_offset + lax.broadcasted_iota(jnp.int32, (tk, tq), 0)
            q_pos = lax.broadcasted_iota(jnp.int32, (tk, tq), 1)
            st = jnp.where(k_pos <= q_pos, st, -jnp.inf)
        m_new = jnp.maximum(m_prev, jnp.max(st, axis=0, keepdims=True))
        alpha = jnp.exp2(m_prev - m_new)
        p = jnp.exp2(st - m_new).astype(BF16)
        k0 = pl.multiple_of(j * tk, tk)
        vt = vt_ref[hh * FOX_VROWS:(hh + 1) * FOX_VROWS, pl.ds(k0, tk)]
        pv = jnp.dot(vt, p, preferred_element_type=F32)
        return m_new, acc * alpha + pv

    def run(chunks, key_offsets):
        sts = [[scores(j, hh) for hh in range(2)] for j in chunks]
        stats = [(m_scr[hh], acc_scr[hh]) for hh in range(2)]
        for ci, j in enumerate(chunks):
            for hh in range(2):
                stats[hh] = absorb(j, hh, sts[ci][hh], stats[hh], key_offsets[ci])
        for hh in range(2):
            m_scr[hh], acc_scr[hh] = stats[hh]

    sub = tq // tk
    span = lambda s: [sub * s + c for c in range(sub)]
    own_offsets = [c * tk for c in range(sub)]

    def pair(sp, carry):
        run(span(2 * sp) + span(2 * sp + 1), [None] * (2 * sub))
        return carry

    lax.fori_loop(0, qi // 2, pair, 0)

    @pl.when(qi % 2 == 1)
    def _():
        run(span(qi - 1) + span(qi), [None] * sub + own_offsets)

    @pl.when(qi % 2 == 0)
    def _():
        run(span(qi), own_offsets)

    outs = []
    for hh in range(2):
        acc = acc_scr[hh]
        outs.append(acc[:FOX_DH, :] / acc[FOX_DH:FOX_DH + 1, :])
    ot = jnp.concatenate(outs, axis=0)
    gt = gate_ref[...].astype(F32)
    o_ref[...] = (jnp.transpose(ot) * (gt * _sigmoid(gt))).astype(o_ref.dtype)


def _fox_attn(z_main, q_extra, k_extra, v_t, *, batch, seq, tq, tk):
    T = z_main.shape[0]
    nq = seq // tq
    npair = FOX_HEADS // 2
    qblk = ZM_FQ // LANES
    kblk = ZM_FK // LANES
    gblk = ZM_FGATE // LANES
    return pl.pallas_call(
        functools.partial(_fox_attn_kernel, tq=tq, tk=tk),
        grid=(batch, npair, nq),
        in_specs=[
            pl.BlockSpec((tq, LANES), lambda b, p, i: (b * nq + i, qblk + p)),
            pl.BlockSpec((tq, LANES), lambda b, p, i: (b * nq + i, p)),
            pl.BlockSpec((seq, LANES), lambda b, p, i: (b, kblk + p)),
            pl.BlockSpec((seq, LANES), lambda b, p, i: (b, p)),
            pl.BlockSpec((2 * FOX_VROWS, seq), lambda b, p, i: (b * npair + p, 0)),
            pl.BlockSpec((tq, LANES), lambda b, p, i: (b * nq + i, gblk + p)),
        ],
        out_specs=pl.BlockSpec((tq, LANES), lambda b, p, i: (b * nq + i, p)),
        out_shape=jax.ShapeDtypeStruct((T, FOX_WIDTH), BF16),
        scratch_shapes=[
            pltpu.VMEM((2, 1, tq), F32),
            pltpu.VMEM((2, FOX_VROWS, tq), F32),
        ],
        compiler_params=_cparams("parallel", "parallel", "arbitrary"),
        name="fox_attn",
    )(z_main, q_extra, z_main, k_extra, v_t, z_main)


def _memkv_kernel(mem_ref, gain_ref, w_ref, o_ref):
    x = mem_ref[...]
    ms = jnp.mean(x * x, axis=-1, keepdims=True)
    h = (x * lax.rsqrt(ms + EPS) * gain_ref[...]).astype(BF16)
    o_ref[...] = jnp.dot(h, w_ref[...], preferred_element_type=F32).astype(o_ref.dtype)


def _memkv(mem2, gain, w_kv, *, tm):
    R, D = mem2.shape
    N = w_kv.shape[1]
    return pl.pallas_call(
        _memkv_kernel,
        grid=(R // tm,),
        in_specs=[
            pl.BlockSpec((tm, D), lambda i: (i, 0)),
            pl.BlockSpec((1, D), lambda i: (0, 0)),
            pl.BlockSpec((D, N), lambda i: (0, 0)),
        ],
        out_specs=pl.BlockSpec((tm, N), lambda i: (i, 0)),
        out_shape=jax.ShapeDtypeStruct((R, N), BF16),
        compiler_params=_cparams("parallel"),
        name="memkv",
    )(mem2, gain, w_kv)


def _mem_attn_kernel(q_ref, gate_ref, mk_ref, mv_ref, o_ref):
    scale = MEM_DH ** -0.5 * LOG2E
    outs = []
    for h in range(MEM_HEADS):
        sl = slice(h * MEM_DH, (h + 1) * MEM_DH)
        s = _dot_nt(q_ref[:, sl], mk_ref[:, sl]) * scale
        m = jnp.max(s, axis=-1, keepdims=True)
        p = jnp.exp2(s - m)
        l = jnp.sum(p, axis=-1, keepdims=True)
        pv = jnp.dot(p.astype(BF16), mv_ref[:, sl], preferred_element_type=F32)
        outs.append(pv / l)
    o = jnp.concatenate(outs, axis=1)
    gt = gate_ref[...].astype(F32)
    o_ref[...] = (o * (gt * _sigmoid(gt))).astype(o_ref.dtype)


def _mem_attn(z_main, mkv, *, batch, seq, mem_len, tm):
    T = z_main.shape[0]
    nsb = seq // tm
    qblk = ZM_MQ // MEM_WIDTH
    gblk = ZM_MGATE // MEM_WIDTH
    return pl.pallas_call(
        _mem_attn_kernel,
        grid=(batch, nsb),
        in_specs=[
            pl.BlockSpec((tm, MEM_WIDTH), lambda b, s: (b * nsb + s, qblk)),
            pl.BlockSpec((tm, MEM_WIDTH), lambda b, s: (b * nsb + s, gblk)),
            pl.BlockSpec((mem_len, MEM_WIDTH), lambda b, s: (b, 0)),
            pl.BlockSpec((mem_len, MEM_WIDTH), lambda b, s: (b, 1)),
        ],
        out_specs=pl.BlockSpec((tm, MEM_WIDTH), lambda b, s: (b * nsb + s, 0)),
        out_shape=jax.ShapeDtypeStruct((T, MEM_WIDTH), BF16),
        compiler_params=_cparams("parallel", "parallel"),
        name="mem_attn",
    )(z_main, z_main, mkv, mkv)


def _merge_out_kernel(oa_ref, ob_ref, oc_ref, g0_ref, g1_ref, g2_ref, x_ref, gain_ref,
                      wbr_hbm, wout_hbm, o_ref, wbr_vmem, wout_vmem, sem, *, layer, final_norm):
    @pl.when(pl.program_id(0) == 0)
    def _():
        copies = [pltpu.make_async_copy(wbr_hbm.at[layer], wbr_vmem, sem.at[0]),
                  pltpu.make_async_copy(wout_hbm.at[layer], wout_vmem, sem.at[1])]
        for cp in copies:
            cp.start()
        for cp in copies:
            cp.wait()

    dot = functools.partial(jnp.dot, preferred_element_type=F32)
    r_b, r_c = GLA_WIDTH, GLA_WIDTH + FOX_WIDTH
    y = _sigmoid(g0_ref[...].astype(F32)) * dot(oa_ref[...], wbr_vmem[0:r_b, :])
    y = y + _sigmoid(g1_ref[...].astype(F32)) * dot(ob_ref[...], wbr_vmem[r_b:r_c, :])
    y = y + _sigmoid(g2_ref[...].astype(F32)) * dot(oc_ref[...], wbr_vmem[r_c:, :])
    x = x_ref[...] + dot(y.astype(BF16), wout_vmem[...])
    if final_norm:
        ms = jnp.mean(x * x, axis=-1, keepdims=True)
        x = x * lax.rsqrt(ms + EPS) * gain_ref[...]
    o_ref[...] = x


def _merge_out(o_a, o_b, o_c, z_main, x2, w_branch, w_out, final_gain, *, layer, tm, final_norm):
    T, D = x2.shape
    gblk = ZM_MERGE // D
    return pl.pallas_call(
        functools.partial(_merge_out_kernel, layer=layer, final_norm=final_norm),
        grid=(T // tm,),
        in_specs=[
            pl.BlockSpec((tm, GLA_WIDTH), lambda i: (i, 0)),
            pl.BlockSpec((tm, FOX_WIDTH), lambda i: (i, 0)),
            pl.BlockSpec((tm, MEM_WIDTH), lambda i: (i, 0)),
            pl.BlockSpec((tm, D), lambda i: (i, gblk)),
            pl.BlockSpec((tm, D), lambda i: (i, gblk + 1)),
            pl.BlockSpec((tm, D), lambda i: (i, gblk + 2)),
            pl.BlockSpec((tm, D), lambda i: (i, 0)),
            pl.BlockSpec((1, D), lambda i: (0, 0)),
            pl.BlockSpec(memory_space=pl.ANY),
            pl.BlockSpec(memory_space=pl.ANY),
        ],
        out_specs=pl.BlockSpec((tm, D), lambda i: (i, 0)),
        out_shape=jax.ShapeDtypeStruct((T, D), F32),
        scratch_shapes=[
            pltpu.VMEM(w_branch.shape[1:], BF16),
            pltpu.VMEM(w_out.shape[1:], BF16),
            pltpu.SemaphoreType.DMA((2,)),
        ],
        compiler_params=_cparams("arbitrary"),
        name="merge_out",
    )(o_a, o_b, o_c, z_main, z_main, z_main, x2, final_gain, w_branch, w_out)


W_GDOWN = 2 * GLA_KWIDTH + 2 * GLA_WIDTH
W_FQ = W_GDOWN + GK_RANK
W_FLOGIT = W_FQ + 3 * FOX_WIDTH
W_FGATE = W_FLOGIT + FOX_HEADS


def _pack_src_row(r):
    return (r + jnp.where(r >= W_GDOWN, GK_RANK, 0)
            + jnp.where(r + GK_RANK >= W_FLOGIT, FOX_HEADS, 0))


def _pack_copy(wt_hbm, buf, sem, step, *, tr, nblk):
    layer = step // nblk
    row = pl.multiple_of(_pack_src_row((step % nblk) * tr), 8)
    slot = step % 2
    return pltpu.make_async_copy(wt_hbm.at[layer, pl.ds(row, tr), :], buf.at[slot], sem.at[slot])


def _pack_kernel(wt_hbm, wm_ref, buf, sem, *, tr, nblk):
    s = pl.program_id(0)
    copy = functools.partial(_pack_copy, wt_hbm, buf, sem, tr=tr, nblk=nblk)

    @pl.when(s == 0)
    def _():
        copy(s).start()

    @pl.when(s + 1 < pl.num_programs(0))
    def _():
        copy(s + 1).start()

    copy(s).wait()
    wm_ref[...] = buf[s % 2].astype(BF16)


def _pack_w_in(w_in, *, tr):
    depth, d, in_cols = w_in.shape
    n_main = in_cols - GK_RANK - FOX_HEADS
    nblk = n_main // tr
    wt = jnp.swapaxes(w_in, 1, 2)
    w_main_t = pl.pallas_call(
        functools.partial(_pack_kernel, tr=tr, nblk=nblk),
        grid=(depth * nblk,),
        in_specs=[pl.BlockSpec(memory_space=pl.ANY)],
        out_specs=pl.BlockSpec((None, tr, d), lambda s: (s // nblk, s % nblk, 0)),
        out_shape=jax.ShapeDtypeStruct((depth, n_main, d), BF16),
        scratch_shapes=[pltpu.VMEM((2, tr, d), F32), pltpu.SemaphoreType.DMA((2,))],
        compiler_params=_cparams("arbitrary"),
        name="pack_w_in",
    )(wt)
    pad = jnp.zeros((depth, LANES - GD_COPIES * GK_RANK - FOX_HEADS, d), w_in.dtype)
    w_small_t = jnp.concatenate([wt[:, W_GDOWN:W_FQ, :]] * GD_COPIES
                                + [wt[:, W_FLOGIT:W_FGATE, :], pad], axis=1).astype(BF16)
    return w_main_t, w_small_t


def _trunk(x, mem, norm_gain, w_in, w_gk_up, b_gk, gla_norm_gain, b_f, mem_norm_gain,
           w_mem_kv, w_branch, w_out, final_gain, *, tiles):
    B, S, D = x.shape
    M = mem.shape[1]
    depth = w_in.shape[0]
    x2 = x.reshape(B * S, D)
    mem2 = mem.reshape(B * M, D)
    w_main, w_small = _pack_w_in(w_in, tr=tiles["pack_tr"])
    w_branch_bf = w_branch.astype(BF16)
    w_out_bf = w_out.astype(BF16)
    for l in range(depth):
        z_main, z_small = _inproj(x2, norm_gain[l][None, :], w_main, w_small,
                                  layer=l, tm=tiles["in_tm"], tn=tiles["in_tn"])
        o_a = _gla(z_main, z_small, w_gk_up[l], b_gk[l][None, :], gla_norm_gain[l][None, :],
                   batch=B, seq=S, ts=tiles["gla_ts"], unroll=tiles["gla_unroll"])
        bf_row = jnp.zeros((1, LANES), F32).at[0, ZS_FLOGIT:ZS_FLOGIT + FOX_HEADS].set(b_f[l])
        q_extra, k_extra, v_t = _fox_prep(z_main, z_small, bf_row, batch=B, seq=S, ts=tiles["prep_ts"])
        o_b = _fox_attn(z_main, q_extra, k_extra, v_t, batch=B, seq=S,
                        tq=tiles["fox_tq"], tk=tiles["fox_tk"])
        mkv = _memkv(mem2, mem_norm_gain[l][None, :], w_mem_kv[l].astype(BF16), tm=tiles["memkv_tm"])
        o_c = _mem_attn(z_main, mkv, batch=B, seq=S, mem_len=M, tm=tiles["mem_tm"])
        x2 = _merge_out(o_a, o_b, o_c, z_main, x2, w_branch_bf, w_out_bf, final_gain[None, :],
                        layer=l, tm=tiles["merge_tm"], final_norm=(l == depth - 1))
    return x2.reshape(B, S, D)


_TILES = dict(pack_tr=512, in_tm=1024, in_tn=2048, gla_ts=4096, gla_unroll=16, prep_ts=512,
              fox_tq=512, fox_tk=256,
              memkv_tm=256, mem_tm=2048, merge_tm=256)


def kernel(x, mem, norm_gain, w_in, w_gk_up, b_gk, gla_norm_gain, b_f, mem_norm_gain,
           w_mem_kv, w_branch, w_out, final_gain):
    return _trunk(x, mem, norm_gain, w_in, w_gk_up, b_gk, gla_norm_gain, b_f, mem_norm_gain,
                  w_mem_kv, w_branch, w_out, final_gain, tiles=_TILES)
```

```python
import functools

import jax
import jax.numpy as jnp
import numpy as np
from jax import lax
from jax.experimental import pallas as pl
from jax.experimental.pallas import tpu as pltpu

F32 = jnp.float32
BF16 = jnp.bfloat16

EPS = 1e-6

GLA_HEADS, GLA_DK, GLA_DV = 4, 128, 256
GLA_KWIDTH = GLA_HEADS * GLA_DK
GLA_WIDTH = GLA_HEADS * GLA_DV
GK_RANK = 16
GK_NORMALIZER = 16.0
FOX_HEADS, FOX_DH = 8, 64
FOX_WIDTH = FOX_HEADS * FOX_DH
MEM_HEADS, MEM_DH = 4, 128
MEM_WIDTH = MEM_HEADS * MEM_DH
N_BRANCH = 3

LANES = 128
VMEM_LIMIT_BYTES = 56 * 1024 * 1024

ZM_GQ = 0
ZM_GK = ZM_GQ + GLA_KWIDTH
ZM_GV = ZM_GK + GLA_KWIDTH
ZM_GGATE = ZM_GV + GLA_WIDTH
ZM_FQ = ZM_GGATE + GLA_WIDTH
ZM_FK = ZM_FQ + FOX_WIDTH
ZM_FV = ZM_FK + FOX_WIDTH
ZM_FGATE = ZM_FV + FOX_WIDTH
ZM_MQ = ZM_FGATE + FOX_WIDTH
ZM_MGATE = ZM_MQ + MEM_WIDTH
ZM_MERGE = ZM_MGATE + MEM_WIDTH
GD_COPIES = 6
FL_COPIES = 3
ZS_FLOGIT = GD_COPIES * GK_RANK

GLA_CHUNK = 64
GLA_SUB = 8
FOX_EXTRA = 6
FOX_VROWS = 80
PREP_CUMSUM_ROWS = 128
LOG2E = 1.4426950408889634


def _cparams(*sem):
    return pltpu.CompilerParams(dimension_semantics=sem, vmem_limit_bytes=VMEM_LIMIT_BYTES)


def _sigmoid(x):
    return 0.5 * jnp.tanh(0.5 * x) + 0.5


def _log_sigmoid(x):
    return jnp.minimum(x, 0.0) - jnp.log(1.0 + jnp.exp(-jnp.abs(x)))


def _split3(x):
    a = x.astype(BF16)
    r = x - a.astype(F32)
    b = r.astype(BF16)
    c = (r - b.astype(F32)).astype(BF16)
    return a, b, c


def _dot_nt(a, b):
    return lax.dot_general(a, b, (((1,), (1,)), ((), ())), preferred_element_type=F32)


def _tri_cumsum(tri_bf, x):
    a, b, c = _split3(x)
    dot = functools.partial(jnp.dot, preferred_element_type=F32)
    return dot(tri_bf, a) + dot(tri_bf, b) + dot(tri_bf, c)


def _inproj_kernel(x_ref, gain_ref, wm_ref, ws_ref, zm_ref, zs_ref, h_scr):
    @pl.when(pl.program_id(1) == 0)
    def _():
        x = x_ref[...]
        ms = jnp.mean(x * x, axis=-1, keepdims=True)
        h = (x * lax.rsqrt(ms + EPS) * gain_ref[...]).astype(BF16)
        h_scr[...] = h
        zs_ref[...] = _dot_nt(h, ws_ref[...])

    zm_ref[...] = _dot_nt(h_scr[...], wm_ref[...]).astype(BF16)


def _inproj(x2, gain, w_main_t, w_small_t, *, layer, tm, tn):
    T, D = x2.shape
    NM = w_main_t.shape[1]
    return pl.pallas_call(
        _inproj_kernel,
        grid=(T // tm, NM // tn),
        in_specs=[
            pl.BlockSpec((tm, D), lambda i, j: (i, 0)),
            pl.BlockSpec((1, D), lambda i, j: (0, 0)),
            pl.BlockSpec((None, tn, D), lambda i, j: (layer, j, 0)),
            pl.BlockSpec((None, LANES, D), lambda i, j: (layer, 0, 0)),
        ],
        out_specs=[
            pl.BlockSpec((tm, tn), lambda i, j: (i, j)),
            pl.BlockSpec((tm, LANES), lambda i, j: (i, 0)),
        ],
        out_shape=[
            jax.ShapeDtypeStruct((T, NM), BF16),
            jax.ShapeDtypeStruct((T, LANES), F32),
        ],
        scratch_shapes=[pltpu.VMEM((tm, D), BF16)],
        compiler_params=_cparams("parallel", "arbitrary"),
        name="inproj",
    )(x2, gain, w_main_t, w_small_t)


def _gla_kernel(q_ref, k_ref, v_ref, zs_ref, gate_ref, wup_ref, bgk_ref, gain_ref,
                o_ref, state_scr, b_scr, p_scr, *, nchunks, unroll):
    C, SUB = GLA_CHUNK, GLA_SUB
    NSUB = C // SUB
    NPAIR = SUB // 2
    DK, DV = GLA_DK, GLA_DV
    dot = functools.partial(jnp.dot, preferred_element_type=F32)

    @pl.when(pl.program_id(2) == 0)
    def _():
        state_scr[...] = jnp.zeros_like(state_scr)

    row = lax.broadcasted_iota(jnp.int32, (C, C), 0)
    col = lax.broadcasted_iota(jnp.int32, (C, C), 1)
    tri = (col <= row).astype(BF16)
    same_sub = (row // SUB) == (col // SUB)
    causal = col <= row

    z1, z2, z3 = _split3(zs_ref[...])
    zlane = lax.broadcasted_iota(jnp.int32, z1.shape, 1)
    lhs = jnp.where(zlane < 3 * GK_RANK, z1, jnp.where(zlane < 5 * GK_RANK, z2, z3))
    w1, w2, w3 = _split3(wup_ref[...])
    rhs = jnp.concatenate([w1, w2, w3, w1, w2, w1,
                           jnp.zeros((LANES - GD_COPIES * GK_RANK, DK), BF16)], axis=0)
    logits = dot(lhs, rhs) + bgk_ref[...]
    g = _log_sigmoid(logits) * (LOG2E / GK_NORMALIZER)
    for c in range(nchunks):
        b_scr[c * C:(c + 1) * C, :] = _tri_cumsum(tri, g[c * C:(c + 1) * C, :])

    cs_r = lax.broadcasted_iota(jnp.int32, (2 * DK, LANES), 0)
    cs_c = lax.broadcasted_iota(jnp.int32, (2 * DK, LANES), 1)
    chansum = ((cs_r // DK) == (cs_c % 2)).astype(BF16)
    pair_of_lane = (lax.broadcasted_iota(jnp.int32, (C, LANES), 1) % SUB) // 2
    scale = DK ** -0.5
    gain = gain_ref[...]

    def chunk(c, p_slot, state):
        r0 = pl.multiple_of(c * C, C)
        rows = pl.ds(r0, C)
        q = q_ref[rows, :].astype(F32) * scale
        k = k_ref[rows, :].astype(F32)
        v = v_ref[rows, :]
        b = b_scr[rows, :]
        b_last = b[C - 1:C, :]

        o = dot((q * jnp.exp2(b)).astype(BF16), state.astype(BF16))

        cross = [jnp.zeros((SUB, C), F32)]
        for i in range(1, NSUB):
            n = i * SUB
            bref = b[n - 1:n, :]
            q_t = (q[n:n + SUB, :] * jnp.exp2(b[n:n + SUB, :] - bref)).astype(BF16)
            k_t = (k[:n, :] * jnp.exp2(bref - b[:n, :])).astype(BF16)
            k_t = jnp.concatenate([k_t, jnp.zeros((C - n, DK), BF16)], axis=0)
            cross.append(lax.dot_general(q_t, k_t, (((1,), (1,)), ((), ())),
                                         preferred_element_type=F32))
        a_cross = jnp.concatenate(cross, axis=0)

        k3 = k.reshape(NSUB, SUB, DK)
        b3 = b.reshape(NSUB, SUB, DK)
        for j in range(SUB):
            kj = jnp.broadcast_to(k3[:, j:j + 1, :], (NSUB, SUB, DK)).reshape(C, DK)
            bj = jnp.broadcast_to(b3[:, j:j + 1, :], (NSUB, SUB, DK)).reshape(C, DK)
            p = q * kj * jnp.exp2(jnp.minimum(b - bj, 0.0))
            p_slot[(j // 2) * C:(j // 2 + 1) * C, (j % 2) * DK:(j % 2 + 1) * DK] = p.astype(BF16)
        d = dot(p_slot[...], chansum)
        a_diag = d[0:C, :]
        for t in range(1, NPAIR):
            a_diag = jnp.where(pair_of_lane == t, d[t * C:(t + 1) * C, :], a_diag)

        a = jnp.where(causal, jnp.where(same_sub, a_diag[:, :C], a_cross), 0.0)
        o = o + dot(a.astype(BF16), v)

        k_dec = (k * jnp.exp2(b_last - b)).astype(BF16)
        upd = lax.dot_general(k_dec, v, (((0,), (0,)), ((), ())),
                              preferred_element_type=F32)
        decay_col = jnp.transpose(jnp.broadcast_to(jnp.exp2(b_last), (DK, DK)))[:, 0:1]
        new_state = state * decay_col + upd

        ms = jnp.mean(o * o, axis=-1, keepdims=True)
        on = o * lax.rsqrt(ms + EPS) * gain
        gt = gate_ref[rows, :].astype(F32)
        o_ref[rows, :] = (on * (gt * _sigmoid(gt))).astype(o_ref.dtype)
        return new_state

    def group(gi, carry):
        state = state_scr[...]
        for u in range(unroll):
            state = chunk(gi * unroll + u, p_scr.at[u], state)
        state_scr[...] = state
        return carry

    lax.fori_loop(0, nchunks // unroll, group, 0)


def _gla(z_main, z_small, w_up, b_gk, gain, *, batch, seq, ts, unroll):
    T = z_main.shape[0]
    nsb = seq // ts
    nchunks = ts // GLA_CHUNK
    rowmap = lambda b, h_, s: b * nsb + s
    kblk = ZM_GK // GLA_DK
    vblk = ZM_GV // GLA_DV
    gblk = ZM_GGATE // GLA_DV
    return pl.pallas_call(
        functools.partial(_gla_kernel, nchunks=nchunks, unroll=unroll),
        grid=(batch, GLA_HEADS, nsb),
        in_specs=[
            pl.BlockSpec((ts, GLA_DK), lambda b, h_, s: (rowmap(b, h_, s), h_)),
            pl.BlockSpec((ts, GLA_DK), lambda b, h_, s: (rowmap(b, h_, s), kblk + h_)),
            pl.BlockSpec((ts, GLA_DV), lambda b, h_, s: (rowmap(b, h_, s), vblk + h_)),
            pl.BlockSpec((ts, LANES), lambda b, h_, s: (rowmap(b, h_, s), 0)),
            pl.BlockSpec((ts, GLA_DV), lambda b, h_, s: (rowmap(b, h_, s), gblk + h_)),
            pl.BlockSpec((GK_RANK, GLA_DK), lambda b, h_, s: (0, h_)),
            pl.BlockSpec((1, GLA_DK), lambda b, h_, s: (0, h_)),
            pl.BlockSpec((1, GLA_DV), lambda b, h_, s: (0, 0)),
        ],
        out_specs=pl.BlockSpec((ts, GLA_DV), lambda b, h_, s: (rowmap(b, h_, s), h_)),
        out_shape=jax.ShapeDtypeStruct((T, GLA_WIDTH), BF16),
        scratch_shapes=[
            pltpu.VMEM((GLA_DK, GLA_DV), F32),
            pltpu.VMEM((ts, GLA_DK), F32),
            pltpu.VMEM((unroll, GLA_SUB // 2 * GLA_CHUNK, 2 * GLA_DK), BF16),
        ],
        compiler_params=_cparams("parallel", "parallel", "arbitrary"),
        name="gla",
    )(z_main, z_main, z_main, z_small, z_main, w_up, b_gk, gain)


def _fox_extra_tables():
    npair = FOX_HEADS // 2
    pq = np.zeros((LANES, npair * LANES), np.float32)
    pk = np.zeros((LANES, npair * LANES), np.float32)
    cq = np.zeros((1, npair * LANES), np.float32)
    ck = np.zeros((1, npair * LANES), np.float32)
    for h in range(FOX_HEADS):
        base = (h // 2) * LANES + (h % 2) * FOX_EXTRA
        for t in range(FL_COPIES):
            pq[ZS_FLOGIT + FOX_HEADS * t + h, base + t] = 1.0
            cq[0, base + FL_COPIES + t] = 1.0
            ck[0, base + t] = 1.0
            pk[ZS_FLOGIT + FOX_HEADS * t + h, base + FL_COPIES + t] = -1.0
    return jnp.asarray(pq, BF16), jnp.asarray(pk, BF16), jnp.asarray(cq), jnp.asarray(ck)


def _fox_prep_kernel(v_ref, zs_ref, bf_ref, pq_ref, pk_ref, cq_ref, ck_ref,
                     qx_ref, kx_ref, vt_ref, carry_scr, *, ts):
    @pl.when(pl.program_id(1) == 0)
    def _():
        carry_scr[...] = jnp.zeros_like(carry_scr)

    dot = functools.partial(jnp.dot, preferred_element_type=F32)
    blk = PREP_CUMSUM_ROWS
    row = lax.broadcasted_iota(jnp.int32, (blk, blk), 0)
    col = lax.broadcasted_iota(jnp.int32, (blk, blk), 1)
    tri = (col <= row).astype(BF16)
    log_f = _log_sigmoid(zs_ref[...] + bf_ref[...])
    carry = carry_scr[...]
    parts = []
    for r in range(0, ts, blk):
        part = _tri_cumsum(tri, log_f[r:r + blk, :]) + carry
        carry = part[blk - 1:blk, :]
        parts.append(part)
    carry_scr[...] = carry
    c_all = jnp.concatenate(parts, axis=0)

    c1, c2, c3 = _split3(c_all * LOG2E)
    lane = lax.broadcasted_iota(jnp.int32, (ts, LANES), 1)
    cs = jnp.where(lane < ZS_FLOGIT + FOX_HEADS, c1, jnp.where(lane < ZS_FLOGIT + 2 * FOX_HEADS, c2, c3))
    qx_ref[...] = (dot(cs, pq_ref[...]) + cq_ref[...]).astype(BF16)
    kx_ref[...] = (dot(cs, pk_ref[...]) + ck_ref[...]).astype(BF16)

    v_t = jnp.transpose(v_ref[...].astype(F32))
    pad_rows = FOX_VROWS - FOX_DH
    ones_blk = jnp.where(lax.broadcasted_iota(jnp.int32, (pad_rows, ts), 0) == 0, 1.0, 0.0)
    pieces = []
    for h in range(FOX_HEADS):
        pieces += [v_t[h * FOX_DH:(h + 1) * FOX_DH, :], ones_blk]
    vt_ref[...] = jnp.concatenate(pieces, axis=0).astype(BF16)


def _fox_prep(z_main, z_small, bf_row, *, batch, seq, ts):
    T = z_main.shape[0]
    nsb = seq // ts
    vblk = ZM_FV // FOX_WIDTH
    npair = FOX_HEADS // 2
    extra = jax.ShapeDtypeStruct((T, npair * LANES), BF16)
    pq, pk, cq, ck = _fox_extra_tables()
    const = lambda b, s: (0, 0)
    return pl.pallas_call(
        functools.partial(_fox_prep_kernel, ts=ts),
        grid=(batch, nsb),
        in_specs=[
            pl.BlockSpec((ts, FOX_WIDTH), lambda b, s: (b * nsb + s, vblk)),
            pl.BlockSpec((ts, LANES), lambda b, s: (b * nsb + s, 0)),
            pl.BlockSpec((1, LANES), const),
            pl.BlockSpec(pq.shape, const),
            pl.BlockSpec(pk.shape, const),
            pl.BlockSpec(cq.shape, const),
            pl.BlockSpec(ck.shape, const),
        ],
        out_specs=[
            pl.BlockSpec((ts, npair * LANES), lambda b, s: (b * nsb + s, 0)),
            pl.BlockSpec((ts, npair * LANES), lambda b, s: (b * nsb + s, 0)),
            pl.BlockSpec((FOX_HEADS * FOX_VROWS, ts), lambda b, s: (b, s)),
        ],
        out_shape=[extra, extra,
                   jax.ShapeDtypeStruct((batch * FOX_HEADS * FOX_VROWS, seq), BF16)],
        scratch_shapes=[pltpu.VMEM((1, LANES), F32)],
        compiler_params=_cparams("parallel", "arbitrary"),
        name="fox_prep",
    )(z_main, z_small, bf_row, pq, pk, cq, ck)


def _fox_attn_kernel(q_ref, qx_ref, k_ref, kx_ref, vt_ref, gate_ref, o_ref, m_scr, acc_scr, *, tq, tk):
    qi = pl.program_id(2)
    m_scr[...] = jnp.full_like(m_scr, -jnp.inf)
    acc_scr[...] = jnp.zeros_like(acc_scr)

    q_scaled = (q_ref[...].astype(F32) * (FOX_DH ** -0.5 * LOG2E)).astype(BF16)
    q2 = jnp.concatenate([q_scaled, qx_ref[...]], axis=1)
    lane = lax.broadcasted_iota(jnp.int32, (1, 2 * LANES), 1)
    head_lanes = [((lane >= hh * FOX_DH) & (lane < (hh + 1) * FOX_DH))
                  | ((lane >= LANES + hh * FOX_EXTRA) & (lane < LANES + (hh + 1) * FOX_EXTRA))
                  for hh in range(2)]

    def scores(j, hh):
        k0 = pl.multiple_of(j * tk, tk)
        k2 = jnp.concatenate([k_ref[pl.ds(k0, tk), :], kx_ref[pl.ds(k0, tk), :]], axis=1)
        k_h = jnp.where(head_lanes[hh], k2, jnp.zeros_like(k2))
        return _dot_nt(k_h, q2)

    def absorb(j, hh, st, stats, key_offset):
        m_prev, acc = stats
        if key_offset is not None:
            k_pos = key_offset + lax.broadcasted_iota(jnp.int32, (tk, tq), 0)
            q_pos = lax.broadcasted_iota(jnp.int32, (tk, tq), 1)
            st = jnp.where(k_pos <= q_pos, st, -jnp.inf)
        m_new = jnp.maximum(m_prev, jnp.max(st, axis=0, keepdims=True))
        alpha = jnp.exp2(m_prev - m_new)
        p = jnp.exp2(st - m_new).astype(BF16)
        k0 = pl.multiple_of(j * tk, tk)
        vt = vt_ref[hh * FOX_VROWS:(hh + 1) * FOX_VROWS, pl.ds(k0, tk)]
        pv = jnp.dot(vt, p, preferred_element_type=F32)
        return m_new, acc * alpha + pv

    def run(chunks, key_offsets):
        sts = [[scores(j, hh) for hh in range(2)] for j in chunks]
        stats = [(m_scr[hh], acc_scr[hh]) for hh in range(2)]
        for ci, j in enumerate(chunks):
            for hh in range(2):
                stats[hh] = absorb(j, hh, sts[ci][hh], stats[hh], key_offsets[ci])
        for hh in range(2):
            m_scr[hh], acc_scr[hh] = stats[hh]

    sub = tq // tk
    span = lambda s: [sub * s + c for c in range(sub)]
    own_offsets = [c * tk for c in range(sub)]

    def pair(sp, carry):
        run(span(2 * sp) + span(2 * sp + 1), [None] * (2 * sub))
        return carry

    lax.fori_loop(0, qi // 2, pair, 0)

    @pl.when(qi % 2 == 1)
    def _():
        run(span(qi - 1) + span(qi), [None] * sub + own_offsets)

    @pl.when(qi % 2 == 0)
    def _():
        run(span(qi), own_offsets)

    outs = []
    for hh in range(2):
        acc = acc_scr[hh]
        outs.append(acc[:FOX_DH, :] / acc[FOX_DH:FOX_DH + 1, :])
    ot = jnp.concatenate(outs, axis=0)
    gt = gate_ref[...].astype(F32)
    o_ref[...] = (jnp.transpose(ot) * (gt * _sigmoid(gt))).astype(o_ref.dtype)


def _fox_attn(z_main, q_extra, k_extra, v_t, *, batch, seq, tq, tk):
    T = z_main.shape[0]
    nq = seq // tq
    npair = FOX_HEADS // 2
    qblk = ZM_FQ // LANES
    kblk = ZM_FK // LANES
    gblk = ZM_FGATE // LANES
    return pl.pallas_call(
        functools.partial(_fox_attn_kernel, tq=tq, tk=tk),
        grid=(batch, npair, nq),
        in_specs=[
            pl.BlockSpec((tq, LANES), lambda b, p, i: (b * nq + i, qblk + p)),
            pl.BlockSpec((tq, LANES), lambda b, p, i: (b * nq + i, p)),
            pl.BlockSpec((seq, LANES), lambda b, p, i: (b, kblk + p)),
            pl.BlockSpec((seq, LANES), lambda b, p, i: (b, p)),
            pl.BlockSpec((2 * FOX_VROWS, seq), lambda b, p, i: (b * npair + p, 0)),
            pl.BlockSpec((tq, LANES), lambda b, p, i: (b * nq + i, gblk + p)),
        ],
        out_specs=pl.BlockSpec((tq, LANES), lambda b, p, i: (b * nq + i, p)),
        out_shape=jax.ShapeDtypeStruct((T, FOX_WIDTH), BF16),
        scratch_shapes=[
            pltpu.VMEM((2, 1, tq), F32),
            pltpu.VMEM((2, FOX_VROWS, tq), F32),
        ],
        compiler_params=_cparams("parallel", "parallel", "arbitrary"),
        name="fox_attn",
    )(z_main, q_extra, z_main, k_extra, v_t, z_main)


def _memkv_kernel(mem_ref, gain_ref, w_ref, o_ref):
    x = mem_ref[...]
    ms = jnp.mean(x * x, axis=-1, keepdims=True)
    h = (x * lax.rsqrt(ms + EPS) * gain_ref[...]).astype(BF16)
    o_ref[...] = jnp.dot(h, w_ref[...], preferred_element_type=F32).astype(o_ref.dtype)


def _memkv(mem2, gain, w_kv, *, tm):
    R, D = mem2.shape
    N = w_kv.shape[1]
    return pl.pallas_call(
        _memkv_kernel,
        grid=(R // tm,),
        in_specs=[
            pl.BlockSpec((tm, D), lambda i: (i, 0)),
            pl.BlockSpec((1, D), lambda i: (0, 0)),
            pl.BlockSpec((D, N), lambda i: (0, 0)),
        ],
        out_specs=pl.BlockSpec((tm, N), lambda i: (i, 0)),
        out_shape=jax.ShapeDtypeStruct((R, N), BF16),
        compiler_params=_cparams("parallel"),
        name="memkv",
    )(mem2, gain, w_kv)


def _mem_attn_kernel(q_ref, gate_ref, mk_ref, mv_ref, o_ref):
    scale = MEM_DH ** -0.5 * LOG2E
    outs = []
    for h in range(MEM_HEADS):
        sl = slice(h * MEM_DH, (h + 1) * MEM_DH)
        s = _dot_nt(q_ref[:, sl], mk_ref[:, sl]) * scale
        m = jnp.max(s, axis=-1, keepdims=True)
        p = jnp.exp2(s - m)
        l = jnp.sum(p, axis=-1, keepdims=True)
        pv = jnp.dot(p.astype(BF16), mv_ref[:, sl], preferred_element_type=F32)
        outs.append(pv / l)
    o = jnp.concatenate(outs, axis=1)
    gt = gate_ref[...].astype(F32)
    o_ref[...] = (o * (gt * _sigmoid(gt))).astype(o_ref.dtype)


def _mem_attn(z_main, mkv, *, batch, seq, mem_len, tm):
    T = z_main.shape[0]
    nsb = seq // tm
    qblk = ZM_MQ // MEM_WIDTH
    gblk = ZM_MGATE // MEM_WIDTH
    return pl.pallas_call(
        _mem_attn_kernel,
        grid=(batch, nsb),
        in_specs=[
            pl.BlockSpec((tm, MEM_WIDTH), lambda b, s: (b * nsb + s, qblk)),
            pl.BlockSpec((tm, MEM_WIDTH), lambda b, s: (b * nsb + s, gblk)),
            pl.BlockSpec((mem_len, MEM_WIDTH), lambda b, s: (b, 0)),
            pl.BlockSpec((mem_len, MEM_WIDTH), lambda b, s: (b, 1)),
        ],
        out_specs=pl.BlockSpec((tm, MEM_WIDTH), lambda b, s: (b * nsb + s, 0)),
        out_shape=jax.ShapeDtypeStruct((T, MEM_WIDTH), BF16),
        compiler_params=_cparams("parallel", "parallel"),
        name="mem_attn",
    )(z_main, z_main, mkv, mkv)


def _merge_out_kernel(oa_ref, ob_ref, oc_ref, g0_ref, g1_ref, g2_ref, x_ref, gain_ref,
                      wbr_hbm, wout_hbm, o_ref, wbr_vmem, wout_vmem, sem, *, layer, final_norm):
    @pl.when(pl.program_id(0) == 0)
    def _():
        copies = [pltpu.make_async_copy(wbr_hbm.at[layer], wbr_vmem, sem.at[0]),
                  pltpu.make_async_copy(wout_hbm.at[layer], wout_vmem, sem.at[1])]
        for cp in copies:
            cp.start()
        for cp in copies:
            cp.wait()

    dot = functools.partial(jnp.dot, preferred_element_type=F32)
    r_b, r_c = GLA_WIDTH, GLA_WIDTH + FOX_WIDTH
    y = _sigmoid(g0_ref[...].astype(F32)) * dot(oa_ref[...], wbr_vmem[0:r_b, :])
    y = y + _sigmoid(g1_ref[...].astype(F32)) * dot(ob_ref[...], wbr_vmem[r_b:r_c, :])
    y = y + _sigmoid(g2_ref[...].astype(F32)) * dot(oc_ref[...], wbr_vmem[r_c:, :])
    x = x_ref[...] + dot(y.astype(BF16), wout_vmem[...])
    if final_norm:
        ms = jnp.mean(x * x, axis=-1, keepdims=True)
        x = x * lax.rsqrt(ms + EPS) * gain_ref[...]
    o_ref[...] = x


def _merge_out(o_a, o_b, o_c, z_main, x2, w_branch, w_out, final_gain, *, layer, tm, final_norm):
    T, D = x2.shape
    gblk = ZM_MERGE // D
    return pl.pallas_call(
        functools.partial(_merge_out_kernel, layer=layer, final_norm=final_norm),
        grid=(T // tm,),
        in_specs=[
            pl.BlockSpec((tm, GLA_WIDTH), lambda i: (i, 0)),
            pl.BlockSpec((tm, FOX_WIDTH), lambda i: (i, 0)),
            pl.BlockSpec((tm, MEM_WIDTH), lambda i: (i, 0)),
            pl.BlockSpec((tm, D), lambda i: (i, gblk)),
            pl.BlockSpec((tm, D), lambda i: (i, gblk + 1)),
            pl.BlockSpec((tm, D), lambda i: (i, gblk + 2)),
            pl.BlockSpec((tm, D), lambda i: (i, 0)),
            pl.BlockSpec((1, D), lambda i: (0, 0)),
            pl.BlockSpec(memory_space=pl.ANY),
            pl.BlockSpec(memory_space=pl.ANY),
        ],
        out_specs=pl.BlockSpec((tm, D), lambda i: (i, 0)),
        out_shape=jax.ShapeDtypeStruct((T, D), F32),
        scratch_shapes=[
            pltpu.VMEM(w_branch.shape[1:], BF16),
            pltpu.VMEM(w_out.shape[1:], BF16),
            pltpu.SemaphoreType.DMA((2,)),
        ],
        compiler_params=_cparams("arbitrary"),
        name="merge_out",
    )(o_a, o_b, o_c, z_main, z_main, z_main, x2, final_gain, w_branch, w_out)


W_GDOWN = 2 * GLA_KWIDTH + 2 * GLA_WIDTH
W_FQ = W_GDOWN + GK_RANK
W_FLOGIT = W_FQ + 3 * FOX_WIDTH
W_FGATE = W_FLOGIT + FOX_HEADS


def _pack_src_row(r):
    return (r + jnp.where(r >= W_GDOWN, GK_RANK, 0)
            + jnp.where(r + GK_RANK >= W_FLOGIT, FOX_HEADS, 0))


def _pack_copy(wt_hbm, buf, sem, step, *, tr, nblk):
    layer = step // nblk
    row = pl.multiple_of(_pack_src_row((step % nblk) * tr), 8)
    slot = step % 2
    return pltpu.make_async_copy(wt_hbm.at[layer, pl.ds(row, tr), :], buf.at[slot], sem.at[slot])


def _pack_kernel(wt_hbm, wm_ref, buf, sem, *, tr, nblk):
    s = pl.program_id(0)
    copy = functools.partial(_pack_copy, wt_hbm, buf, sem, tr=tr, nblk=nblk)

    @pl.when(s == 0)
    def _():
        copy(s).start()

    @pl.when(s + 1 < pl.num_programs(0))
    def _():
        copy(s + 1).start()

    copy(s).wait()
    wm_ref[...] = buf[s % 2].astype(BF16)


def _pack_w_in(w_in, *, tr):
    depth, d, in_cols = w_in.shape
    n_main = in_cols - GK_RANK - FOX_HEADS
    nblk = n_main // tr
    wt = jnp.swapaxes(w_in, 1, 2)
    w_main_t = pl.pallas_call(
        functools.partial(_pack_kernel, tr=tr, nblk=nblk),
        grid=(depth * nblk,),
        in_specs=[pl.BlockSpec(memory_space=pl.ANY)],
        out_specs=pl.BlockSpec((None, tr, d), lambda s: (s // nblk, s % nblk, 0)),
        out_shape=jax.ShapeDtypeStruct((depth, n_main, d), BF16),
        scratch_shapes=[pltpu.VMEM((2, tr, d), F32), pltpu.SemaphoreType.DMA((2,))],
        compiler_params=_cparams("arbitrary"),
        name="pack_w_in",
    )(wt)
    pad = jnp.zeros((depth, LANES - GD_COPIES * GK_RANK - FL_COPIES * FOX_HEADS, d), w_in.dtype)
    w_small_t = jnp.concatenate([wt[:, W_GDOWN:W_FQ, :]] * GD_COPIES
                                + [wt[:, W_FLOGIT:W_FGATE, :]] * FL_COPIES + [pad], axis=1).astype(BF16)
    return w_main_t, w_small_t


def _trunk(x, mem, norm_gain, w_in, w_gk_up, b_gk, gla_norm_gain, b_f, mem_norm_gain,
           w_mem_kv, w_branch, w_out, final_gain, *, tiles):
    B, S, D = x.shape
    M = mem.shape[1]
    depth = w_in.shape[0]
    x2 = x.reshape(B * S, D)
    mem2 = mem.reshape(B * M, D)
    w_main, w_small = _pack_w_in(w_in, tr=tiles["pack_tr"])
    w_branch_bf = w_branch.astype(BF16)
    w_out_bf = w_out.astype(BF16)
    for l in range(depth):
        z_main, z_small = _inproj(x2, norm_gain[l][None, :], w_main, w_small,
                                  layer=l, tm=tiles["in_tm"], tn=tiles["in_tn"])
        o_a = _gla(z_main, z_small, w_gk_up[l], b_gk[l][None, :], gla_norm_gain[l][None, :],
                   batch=B, seq=S, ts=tiles["gla_ts"], unroll=tiles["gla_unroll"])
        bf_row = jnp.zeros((1, LANES), F32).at[0, ZS_FLOGIT:ZS_FLOGIT + FL_COPIES * FOX_HEADS].set(
            jnp.tile(b_f[l], FL_COPIES))
        q_extra, k_extra, v_t = _fox_prep(z_main, z_small, bf_row, batch=B, seq=S, ts=tiles["prep_ts"])
        o_b = _fox_attn(z_main, q_extra, k_extra, v_t, batch=B, seq=S,
                        tq=tiles["fox_tq"], tk=tiles["fox_tk"])
        mkv = _memkv(mem2, mem_norm_gain[l][None, :], w_mem_kv[l].astype(BF16), tm=tiles["memkv_tm"])
        o_c = _mem_attn(z_main, mkv, batch=B, seq=S, mem_len=M, tm=tiles["mem_tm"])
        x2 = _merge_out(o_a, o_b, o_c, z_main, x2, w_branch_bf, w_out_bf, final_gain[None, :],
                        layer=l, tm=tiles["merge_tm"], final_norm=(l == depth - 1))
    return x2.reshape(B, S, D)


_TILES = dict(pack_tr=512, in_tm=1024, in_tn=2048, gla_ts=4096, gla_unroll=16, prep_ts=512,
              fox_tq=512, fox_tk=256,
              memkv_tm=256, mem_tm=2048, merge_tm=256)


def kernel(x, mem, norm_gain, w_in, w_gk_up, b_gk, gla_norm_gain, b_f, mem_norm_gain,
           w_mem_kv, w_branch, w_out, final_gain):
    return _trunk(x, mem, norm_gain, w_in, w_gk_up, b_gk, gla_norm_gain, b_f, mem_norm_gain,
                  w_mem_kv, w_branch, w_out, final_gain, tiles=_TILES)
```

```python
import functools

import jax
import jax.numpy as jnp
import numpy as np
from jax import lax
from jax.experimental import pallas as pl
from jax.experimental.pallas import tpu as pltpu

F32 = jnp.float32
BF16 = jnp.bfloat16

EPS = 1e-6

GLA_HEADS, GLA_DK, GLA_DV = 4, 128, 256
GLA_KWIDTH = GLA_HEADS * GLA_DK
GLA_WIDTH = GLA_HEADS * GLA_DV
GK_RANK = 16
GK_NORMALIZER = 16.0
FOX_HEADS, FOX_DH = 8, 64
FOX_WIDTH = FOX_HEADS * FOX_DH
MEM_HEADS, MEM_DH = 4, 128
MEM_WIDTH = MEM_HEADS * MEM_DH
N_BRANCH = 3

LANES = 128
VMEM_LIMIT_BYTES = 56 * 1024 * 1024

ZM_GQ = 0
ZM_GK = ZM_GQ + GLA_KWIDTH
ZM_GV = ZM_GK + GLA_KWIDTH
ZM_GGATE = ZM_GV + GLA_WIDTH
ZM_FQ = ZM_GGATE + GLA_WIDTH
ZM_FK = ZM_FQ + FOX_WIDTH
ZM_FV = ZM_FK + FOX_WIDTH
ZM_FGATE = ZM_FV + FOX_WIDTH
ZM_MQ = ZM_FGATE + FOX_WIDTH
ZM_MGATE = ZM_MQ + MEM_WIDTH
ZM_MERGE = ZM_MGATE + MEM_WIDTH
GD_COPIES = 6
FL_COPIES = 3
ZS_FLOGIT = GD_COPIES * GK_RANK

GLA_CHUNK = 128
GLA_SUB = 8
FOX_EXTRA = 6
FOX_VROWS = 80
PREP_CUMSUM_ROWS = 128
LOG2E = 1.4426950408889634


def _cparams(*sem):
    return pltpu.CompilerParams(dimension_semantics=sem, vmem_limit_bytes=VMEM_LIMIT_BYTES)


def _sigmoid(x):
    return 0.5 * jnp.tanh(0.5 * x) + 0.5


def _log_sigmoid(x):
    return jnp.minimum(x, 0.0) - jnp.log(1.0 + jnp.exp(-jnp.abs(x)))


def _split3(x):
    a = x.astype(BF16)
    r = x - a.astype(F32)
    b = r.astype(BF16)
    c = (r - b.astype(F32)).astype(BF16)
    return a, b, c


def _dot_nt(a, b):
    return lax.dot_general(a, b, (((1,), (1,)), ((), ())), preferred_element_type=F32)


def _tri_cumsum(tri_bf, x):
    a, b, c = _split3(x)
    dot = functools.partial(jnp.dot, preferred_element_type=F32)
    return dot(tri_bf, a) + dot(tri_bf, b) + dot(tri_bf, c)


def _inproj_kernel(x_ref, gain_ref, wm_ref, ws_ref, zm_ref, zs_ref, h_scr):
    @pl.when(pl.program_id(1) == 0)
    def _():
        x = x_ref[...]
        ms = jnp.mean(x * x, axis=-1, keepdims=True)
        h = (x * lax.rsqrt(ms + EPS) * gain_ref[...]).astype(BF16)
        h_scr[...] = h
        zs_ref[...] = _dot_nt(h, ws_ref[...])

    zm_ref[...] = _dot_nt(h_scr[...], wm_ref[...]).astype(BF16)


def _inproj(x2, gain, w_main_t, w_small_t, *, layer, tm, tn):
    T, D = x2.shape
    NM = w_main_t.shape[1]
    return pl.pallas_call(
        _inproj_kernel,
        grid=(T // tm, NM // tn),
        in_specs=[
            pl.BlockSpec((tm, D), lambda i, j: (i, 0)),
            pl.BlockSpec((1, D), lambda i, j: (0, 0)),
            pl.BlockSpec((None, tn, D), lambda i, j: (layer, j, 0)),
            pl.BlockSpec((None, LANES, D), lambda i, j: (layer, 0, 0)),
        ],
        out_specs=[
            pl.BlockSpec((tm, tn), lambda i, j: (i, j)),
            pl.BlockSpec((tm, LANES), lambda i, j: (i, 0)),
        ],
        out_shape=[
            jax.ShapeDtypeStruct((T, NM), BF16),
            jax.ShapeDtypeStruct((T, LANES), F32),
        ],
        scratch_shapes=[pltpu.VMEM((tm, D), BF16)],
        compiler_params=_cparams("parallel", "arbitrary"),
        name="inproj",
    )(x2, gain, w_main_t, w_small_t)


def _gla_kernel(q_ref, k_ref, v_ref, zs_ref, gate_ref, wup_ref, bgk_ref, gain_ref,
                o_ref, state_scr, b_scr, p_scr, *, nchunks, unroll):
    C, SUB = GLA_CHUNK, GLA_SUB
    NSUB = C // SUB
    NPAIR = SUB // 2
    DK, DV = GLA_DK, GLA_DV
    dot = functools.partial(jnp.dot, preferred_element_type=F32)

    @pl.when(pl.program_id(2) == 0)
    def _():
        state_scr[...] = jnp.zeros_like(state_scr)

    row = lax.broadcasted_iota(jnp.int32, (C, C), 0)
    col = lax.broadcasted_iota(jnp.int32, (C, C), 1)
    tri = (col <= row).astype(BF16)
    same_sub = (row // SUB) == (col // SUB)
    causal = col <= row

    z1, z2, z3 = _split3(zs_ref[...])
    zlane = lax.broadcasted_iota(jnp.int32, z1.shape, 1)
    lhs = jnp.where(zlane < 3 * GK_RANK, z1, jnp.where(zlane < 5 * GK_RANK, z2, z3))
    w1, w2, w3 = _split3(wup_ref[...])
    rhs = jnp.concatenate([w1, w2, w3, w1, w2, w1,
                           jnp.zeros((LANES - GD_COPIES * GK_RANK, DK), BF16)], axis=0)
    logits = dot(lhs, rhs) + bgk_ref[...]
    g = _log_sigmoid(logits) * (LOG2E / GK_NORMALIZER)
    for c in range(nchunks):
        b_scr[c * C:(c + 1) * C, :] = _tri_cumsum(tri, g[c * C:(c + 1) * C, :])

    cs_r = lax.broadcasted_iota(jnp.int32, (2 * DK, LANES), 0)
    cs_c = lax.broadcasted_iota(jnp.int32, (2 * DK, LANES), 1)
    chansum = ((cs_r // DK) == (cs_c % 2)).astype(BF16)
    pair_of_lane = (lax.broadcasted_iota(jnp.int32, (C, LANES), 1) % SUB) // 2
    scale = DK ** -0.5
    gain = gain_ref[...]

    def chunk(c, p_slot, state):
        r0 = pl.multiple_of(c * C, C)
        rows = pl.ds(r0, C)
        q = q_ref[rows, :].astype(F32) * scale
        k = k_ref[rows, :].astype(F32)
        v = v_ref[rows, :]
        b = b_scr[rows, :]
        b_last = b[C - 1:C, :]

        o = dot((q * jnp.exp2(b)).astype(BF16), state.astype(BF16))

        cross = [jnp.zeros((SUB, C), F32)]
        for i in range(1, NSUB):
            n = i * SUB
            bref = b[n - 1:n, :]
            q_t = (q[n:n + SUB, :] * jnp.exp2(b[n:n + SUB, :] - bref)).astype(BF16)
            k_t = (k[:n, :] * jnp.exp2(bref - b[:n, :])).astype(BF16)
            k_t = jnp.concatenate([k_t, jnp.zeros((C - n, DK), BF16)], axis=0)
            cross.append(lax.dot_general(q_t, k_t, (((1,), (1,)), ((), ())),
                                         preferred_element_type=F32))
        a_cross = jnp.concatenate(cross, axis=0)

        k3 = k.reshape(NSUB, SUB, DK)
        b3 = b.reshape(NSUB, SUB, DK)
        for j in range(SUB):
            kj = jnp.broadcast_to(k3[:, j:j + 1, :], (NSUB, SUB, DK)).reshape(C, DK)
            bj = jnp.broadcast_to(b3[:, j:j + 1, :], (NSUB, SUB, DK)).reshape(C, DK)
            p = q * kj * jnp.exp2(jnp.minimum(b - bj, 0.0))
            p_slot[(j // 2) * C:(j // 2 + 1) * C, (j % 2) * DK:(j % 2 + 1) * DK] = p.astype(BF16)
        d = dot(p_slot[...], chansum)
        a_diag = d[0:C, :]
        for t in range(1, NPAIR):
            a_diag = jnp.where(pair_of_lane == t, d[t * C:(t + 1) * C, :], a_diag)

        a = jnp.where(causal, jnp.where(same_sub, a_diag[:, :C], a_cross), 0.0)
        o = o + dot(a.astype(BF16), v)

        k_dec = (k * jnp.exp2(b_last - b)).astype(BF16)
        upd = lax.dot_general(k_dec, v, (((0,), (0,)), ((), ())),
                              preferred_element_type=F32)
        decay_col = jnp.transpose(jnp.broadcast_to(jnp.exp2(b_last), (DK, DK)))[:, 0:1]
        new_state = state * decay_col + upd

        ms = jnp.mean(o * o, axis=-1, keepdims=True)
        on = o * lax.rsqrt(ms + EPS) * gain
        gt = gate_ref[rows, :].astype(F32)
        o_ref[rows, :] = (on * (gt * _sigmoid(gt))).astype(o_ref.dtype)
        return new_state

    def group(gi, carry):
        state = state_scr[...]
        for u in range(unroll):
            state = chunk(gi * unroll + u, p_scr.at[u], state)
        state_scr[...] = state
        return carry

    lax.fori_loop(0, nchunks // unroll, group, 0)


def _gla(z_main, z_small, w_up, b_gk, gain, *, batch, seq, ts, unroll):
    T = z_main.shape[0]
    nsb = seq // ts
    nchunks = ts // GLA_CHUNK
    rowmap = lambda b, h_, s: b * nsb + s
    kblk = ZM_GK // GLA_DK
    vblk = ZM_GV // GLA_DV
    gblk = ZM_GGATE // GLA_DV
    return pl.pallas_call(
        functools.partial(_gla_kernel, nchunks=nchunks, unroll=unroll),
        grid=(batch, GLA_HEADS, nsb),
        in_specs=[
            pl.BlockSpec((ts, GLA_DK), lambda b, h_, s: (rowmap(b, h_, s), h_)),
            pl.BlockSpec((ts, GLA_DK), lambda b, h_, s: (rowmap(b, h_, s), kblk + h_)),
            pl.BlockSpec((ts, GLA_DV), lambda b, h_, s: (rowmap(b, h_, s), vblk + h_)),
            pl.BlockSpec((ts, LANES), lambda b, h_, s: (rowmap(b, h_, s), 0)),
            pl.BlockSpec((ts, GLA_DV), lambda b, h_, s: (rowmap(b, h_, s), gblk + h_)),
            pl.BlockSpec((GK_RANK, GLA_DK), lambda b, h_, s: (0, h_)),
            pl.BlockSpec((1, GLA_DK), lambda b, h_, s: (0, h_)),
            pl.BlockSpec((1, GLA_DV), lambda b, h_, s: (0, 0)),
        ],
        out_specs=pl.BlockSpec((ts, GLA_DV), lambda b, h_, s: (rowmap(b, h_, s), h_)),
        out_shape=jax.ShapeDtypeStruct((T, GLA_WIDTH), BF16),
        scratch_shapes=[
            pltpu.VMEM((GLA_DK, GLA_DV), F32),
            pltpu.VMEM((ts, GLA_DK), F32),
            pltpu.VMEM((unroll, GLA_SUB // 2 * GLA_CHUNK, 2 * GLA_DK), BF16),
        ],
        compiler_params=_cparams("parallel", "parallel", "arbitrary"),
        name="gla",
    )(z_main, z_main, z_main, z_small, z_main, w_up, b_gk, gain)


def _fox_extra_tables():
    npair = FOX_HEADS // 2
    pq = np.zeros((LANES, npair * LANES), np.float32)
    pk = np.zeros((LANES, npair * LANES), np.float32)
    cq = np.zeros((1, npair * LANES), np.float32)
    ck = np.zeros((1, npair * LANES), np.float32)
    for h in range(FOX_HEADS):
        base = (h // 2) * LANES + (h % 2) * FOX_EXTRA
        for t in range(FL_COPIES):
            pq[ZS_FLOGIT + FOX_HEADS * t + h, base + t] = 1.0
            cq[0, base + FL_COPIES + t] = 1.0
            ck[0, base + t] = 1.0
            pk[ZS_FLOGIT + FOX_HEADS * t + h, base + FL_COPIES + t] = -1.0
    return jnp.asarray(pq, BF16), jnp.asarray(pk, BF16), jnp.asarray(cq), jnp.asarray(ck)


def _fox_prep_kernel(v_ref, zs_ref, bf_ref, pq_ref, pk_ref, cq_ref, ck_ref,
                     qx_ref, kx_ref, vt_ref, carry_scr, *, ts):
    @pl.when(pl.program_id(1) == 0)
    def _():
        carry_scr[...] = jnp.zeros_like(carry_scr)

    dot = functools.partial(jnp.dot, preferred_element_type=F32)
    blk = PREP_CUMSUM_ROWS
    row = lax.broadcasted_iota(jnp.int32, (blk, blk), 0)
    col = lax.broadcasted_iota(jnp.int32, (blk, blk), 1)
    tri = (col <= row).astype(BF16)
    log_f = _log_sigmoid(zs_ref[...] + bf_ref[...])
    carry = carry_scr[...]
    parts = []
    for r in range(0, ts, blk):
        part = _tri_cumsum(tri, log_f[r:r + blk, :]) + carry
        carry = part[blk - 1:blk, :]
        parts.append(part)
    carry_scr[...] = carry
    c_all = jnp.concatenate(parts, axis=0)

    c1, c2, c3 = _split3(c_all * LOG2E)
    lane = lax.broadcasted_iota(jnp.int32, (ts, LANES), 1)
    cs = jnp.where(lane < ZS_FLOGIT + FOX_HEADS, c1, jnp.where(lane < ZS_FLOGIT + 2 * FOX_HEADS, c2, c3))
    qx_ref[...] = (dot(cs, pq_ref[...]) + cq_ref[...]).astype(BF16)
    kx_ref[...] = (dot(cs, pk_ref[...]) + ck_ref[...]).astype(BF16)

    v_t = jnp.transpose(v_ref[...].astype(F32))
    pad_rows = FOX_VROWS - FOX_DH
    ones_blk = jnp.where(lax.broadcasted_iota(jnp.int32, (pad_rows, ts), 0) == 0, 1.0, 0.0)
    pieces = []
    for h in range(FOX_HEADS):
        pieces += [v_t[h * FOX_DH:(h + 1) * FOX_DH, :], ones_blk]
    vt_ref[...] = jnp.concatenate(pieces, axis=0).astype(BF16)


def _fox_prep(z_main, z_small, bf_row, *, batch, seq, ts):
    T = z_main.shape[0]
    nsb = seq // ts
    vblk = ZM_FV // FOX_WIDTH
    npair = FOX_HEADS // 2
    extra = jax.ShapeDtypeStruct((T, npair * LANES), BF16)
    pq, pk, cq, ck = _fox_extra_tables()
    const = lambda b, s: (0, 0)
    return pl.pallas_call(
        functools.partial(_fox_prep_kernel, ts=ts),
        grid=(batch, nsb),
        in_specs=[
            pl.BlockSpec((ts, FOX_WIDTH), lambda b, s: (b * nsb + s, vblk)),
            pl.BlockSpec((ts, LANES), lambda b, s: (b * nsb + s, 0)),
            pl.BlockSpec((1, LANES), const),
            pl.BlockSpec(pq.shape, const),
            pl.BlockSpec(pk.shape, const),
            pl.BlockSpec(cq.shape, const),
            pl.BlockSpec(ck.shape, const),
        ],
        out_specs=[
            pl.BlockSpec((ts, npair * LANES), lambda b, s: (b * nsb + s, 0)),
            pl.BlockSpec((ts, npair * LANES), lambda b, s: (b * nsb + s, 0)),
            pl.BlockSpec((FOX_HEADS * FOX_VROWS, ts), lambda b, s: (b, s)),
        ],
        out_shape=[extra, extra,
                   jax.ShapeDtypeStruct((batch * FOX_HEADS * FOX_VROWS, seq), BF16)],
        scratch_shapes=[pltpu.VMEM((1, LANES), F32)],
        compiler_params=_cparams("parallel", "arbitrary"),
        name="fox_prep",
    )(z_main, z_small, bf_row, pq, pk, cq, ck)


def _fox_attn_kernel(q_ref, qx_ref, k_ref, kx_ref, vt_ref, gate_ref, o_ref, m_scr, acc_scr, *, tq, tk):
    qi = pl.program_id(2)
    m_scr[...] = jnp.full_like(m_scr, -jnp.inf)
    acc_scr[...] = jnp.zeros_like(acc_scr)

    q_scaled = (q_ref[...].astype(F32) * (FOX_DH ** -0.5 * LOG2E)).astype(BF16)
    q2 = jnp.concatenate([q_scaled, qx_ref[...]], axis=1)
    lane = lax.broadcasted_iota(jnp.int32, (1, 2 * LANES), 1)
    head_lanes = [((lane >= hh * FOX_DH) & (lane < (hh + 1) * FOX_DH))
                  | ((lane >= LANES + hh * FOX_EXTRA) & (lane < LANES + (hh + 1) * FOX_EXTRA))
                  for hh in range(2)]

    def scores(j, hh):
        k0 = pl.multiple_of(j * tk, tk)
        k2 = jnp.concatenate([k_ref[pl.ds(k0, tk), :], kx_ref[pl.ds(k0, tk), :]], axis=1)
        k_h = jnp.where(head_lanes[hh], k2, jnp.zeros_like(k2))
        return _dot_nt(k_h, q2)

    def absorb(j, hh, st, stats, key_offset):
        m_prev, acc = stats
        if key_offset is not None:
            k_pos = key_offset + lax.broadcasted_iota(jnp.int32, (tk, tq), 0)
            q_pos = lax.broadcasted_iota(jnp.int32, (tk, tq), 1)
            st = jnp.where(k_pos <= q_pos, st, -jnp.inf)
        m_new = jnp.maximum(m_prev, jnp.max(st, axis=0, keepdims=True))
        alpha = jnp.exp2(m_prev - m_new)
        p = jnp.exp2(st - m_new).astype(BF16)
        k0 = pl.multiple_of(j * tk, tk)
        vt = vt_ref[hh * FOX_VROWS:(hh + 1) * FOX_VROWS, pl.ds(k0, tk)]
        pv = jnp.dot(vt, p, preferred_element_type=F32)
        return m_new, acc * alpha + pv

    def run(chunks, key_offsets):
        sts = [[scores(j, hh) for hh in range(2)] for j in chunks]
        stats = [(m_scr[hh], acc_scr[hh]) for hh in range(2)]
        for ci, j in enumerate(chunks):
            for hh in range(2):
                stats[hh] = absorb(j, hh, sts[ci][hh], stats[hh], key_offsets[ci])
        for hh in range(2):
            m_scr[hh], acc_scr[hh] = stats[hh]

    sub = tq // tk
    span = lambda s: [sub * s + c for c in range(sub)]
    own_offsets = [c * tk for c in range(sub)]

    def pair(sp, carry):
        run(span(2 * sp) + span(2 * sp + 1), [None] * (2 * sub))
        return carry

    lax.fori_loop(0, qi // 2, pair, 0)

    @pl.when(qi % 2 == 1)
    def _():
        run(span(qi - 1) + span(qi), [None] * sub + own_offsets)

    @pl.when(qi % 2 == 0)
    def _():
        run(span(qi), own_offsets)

    outs = []
    for hh in range(2):
        acc = acc_scr[hh]
        outs.append(acc[:FOX_DH, :] / acc[FOX_DH:FOX_DH + 1, :])
    ot = jnp.concatenate(outs, axis=0)
    gt = gate_ref[...].astype(F32)
    o_ref[...] = (jnp.transpose(ot) * (gt * _sigmoid(gt))).astype(o_ref.dtype)


def _fox_attn(z_main, q_extra, k_extra, v_t, *, batch, seq, tq, tk):
    T = z_main.shape[0]
    nq = seq // tq
    npair = FOX_HEADS // 2
    qblk = ZM_FQ // LANES
    kblk = ZM_FK // LANES
    gblk = ZM_FGATE // LANES
    return pl.pallas_call(
        functools.partial(_fox_attn_kernel, tq=tq, tk=tk),
        grid=(batch, npair, nq),
        in_specs=[
            pl.BlockSpec((tq, LANES), lambda b, p, i: (b * nq + i, qblk + p)),
            pl.BlockSpec((tq, LANES), lambda b, p, i: (b * nq + i, p)),
            pl.BlockSpec((seq, LANES), lambda b, p, i: (b, kblk + p)),
            pl.BlockSpec((seq, LANES), lambda b, p, i: (b, p)),
            pl.BlockSpec((2 * FOX_VROWS, seq), lambda b, p, i: (b * npair + p, 0)),
            pl.BlockSpec((tq, LANES), lambda b, p, i: (b * nq + i, gblk + p)),
        ],
        out_specs=pl.BlockSpec((tq, LANES), lambda b, p, i: (b * nq + i, p)),
        out_shape=jax.ShapeDtypeStruct((T, FOX_WIDTH), BF16),
        scratch_shapes=[
            pltpu.VMEM((2, 1, tq), F32),
            pltpu.VMEM((2, FOX_VROWS, tq), F32),
        ],
        compiler_params=_cparams("parallel", "parallel", "arbitrary"),
        name="fox_attn",
    )(z_main, q_extra, z_main, k_extra, v_t, z_main)


def _memkv_kernel(mem_ref, gain_ref, w_ref, o_ref):
    x = mem_ref[...]
    ms = jnp.mean(x * x, axis=-1, keepdims=True)
    h = (x * lax.rsqrt(ms + EPS) * gain_ref[...]).astype(BF16)
    o_ref[...] = jnp.dot(h, w_ref[...], preferred_element_type=F32).astype(o_ref.dtype)


def _memkv(mem2, gain, w_kv, *, tm):
    R, D = mem2.shape
    N = w_kv.shape[1]
    return pl.pallas_call(
        _memkv_kernel,
        grid=(R // tm,),
        in_specs=[
            pl.BlockSpec((tm, D), lambda i: (i, 0)),
            pl.BlockSpec((1, D), lambda i: (0, 0)),
            pl.BlockSpec((D, N), lambda i: (0, 0)),
        ],
        out_specs=pl.BlockSpec((tm, N), lambda i: (i, 0)),
        out_shape=jax.ShapeDtypeStruct((R, N), BF16),
        compiler_params=_cparams("parallel"),
        name="memkv",
    )(mem2, gain, w_kv)


def _mem_attn_kernel(q_ref, gate_ref, mk_ref, mv_ref, o_ref):
    scale = MEM_DH ** -0.5 * LOG2E
    outs = []
    for h in range(MEM_HEADS):
        sl = slice(h * MEM_DH, (h + 1) * MEM_DH)
        s = _dot_nt(q_ref[:, sl], mk_ref[:, sl]) * scale
        m = jnp.max(s, axis=-1, keepdims=True)
        p = jnp.exp2(s - m)
        l = jnp.sum(p, axis=-1, keepdims=True)
        pv = jnp.dot(p.astype(BF16), mv_ref[:, sl], preferred_element_type=F32)
        outs.append(pv / l)
    o = jnp.concatenate(outs, axis=1)
    gt = gate_ref[...].astype(F32)
    o_ref[...] = (o * (gt * _sigmoid(gt))).astype(o_ref.dtype)


def _mem_attn(z_main, mkv, *, batch, seq, mem_len, tm):
    T = z_main.shape[0]
    nsb = seq // tm
    qblk = ZM_MQ // MEM_WIDTH
    gblk = ZM_MGATE // MEM_WIDTH
    return pl.pallas_call(
        _mem_attn_kernel,
        grid=(batch, nsb),
        in_specs=[
            pl.BlockSpec((tm, MEM_WIDTH), lambda b, s: (b * nsb + s, qblk)),
            pl.BlockSpec((tm, MEM_WIDTH), lambda b, s: (b * nsb + s, gblk)),
            pl.BlockSpec((mem_len, MEM_WIDTH), lambda b, s: (b, 0)),
            pl.BlockSpec((mem_len, MEM_WIDTH), lambda b, s: (b, 1)),
        ],
        out_specs=pl.BlockSpec((tm, MEM_WIDTH), lambda b, s: (b * nsb + s, 0)),
        out_shape=jax.ShapeDtypeStruct((T, MEM_WIDTH), BF16),
        compiler_params=_cparams("parallel", "parallel"),
        name="mem_attn",
    )(z_main, z_main, mkv, mkv)


def _merge_out_kernel(oa_ref, ob_ref, oc_ref, g0_ref, g1_ref, g2_ref, x_ref, gain_ref,
                      wbr_hbm, wout_hbm, o_ref, wbr_vmem, wout_vmem, sem, *, layer, final_norm):
    @pl.when(pl.program_id(0) == 0)
    def _():
        copies = [pltpu.make_async_copy(wbr_hbm.at[layer], wbr_vmem, sem.at[0]),
                  pltpu.make_async_copy(wout_hbm.at[layer], wout_vmem, sem.at[1])]
        for cp in copies:
            cp.start()
        for cp in copies:
            cp.wait()

    dot = functools.partial(jnp.dot, preferred_element_type=F32)
    r_b, r_c = GLA_WIDTH, GLA_WIDTH + FOX_WIDTH
    y = _sigmoid(g0_ref[...].astype(F32)) * dot(oa_ref[...], wbr_vmem[0:r_b, :])
    y = y + _sigmoid(g1_ref[...].astype(F32)) * dot(ob_ref[...], wbr_vmem[r_b:r_c, :])
    y = y + _sigmoid(g2_ref[...].astype(F32)) * dot(oc_ref[...], wbr_vmem[r_c:, :])
    x = x_ref[...] + dot(y.astype(BF16), wout_vmem[...])
    if final_norm:
        ms = jnp.mean(x * x, axis=-1, keepdims=True)
        x = x * lax.rsqrt(ms + EPS) * gain_ref[...]
    o_ref[...] = x


def _merge_out(o_a, o_b, o_c, z_main, x2, w_branch, w_out, final_gain, *, layer, tm, final_norm):
    T, D = x2.shape
    gblk = ZM_MERGE // D
    return pl.pallas_call(
        functools.partial(_merge_out_kernel, layer=layer, final_norm=final_norm),
        grid=(T // tm,),
        in_specs=[
            pl.BlockSpec((tm, GLA_WIDTH), lambda i: (i, 0)),
            pl.BlockSpec((tm, FOX_WIDTH), lambda i: (i, 0)),
            pl.BlockSpec((tm, MEM_WIDTH), lambda i: (i, 0)),
            pl.BlockSpec((tm, D), lambda i: (i, gblk)),
            pl.BlockSpec((tm, D), lambda i: (i, gblk + 1)),
            pl.BlockSpec((tm, D), lambda i: (i, gblk + 2)),
            pl.BlockSpec((tm, D), lambda i: (i, 0)),
            pl.BlockSpec((1, D), lambda i: (0, 0)),
            pl.BlockSpec(memory_space=pl.ANY),
            pl.BlockSpec(memory_space=pl.ANY),
        ],
        out_specs=pl.BlockSpec((tm, D), lambda i: (i, 0)),
        out_shape=jax.ShapeDtypeStruct((T, D), F32),
        scratch_shapes=[
            pltpu.VMEM(w_branch.shape[1:], BF16),
            pltpu.VMEM(w_out.shape[1:], BF16),
            pltpu.SemaphoreType.DMA((2,)),
        ],
        compiler_params=_cparams("arbitrary"),
        name="merge_out",
    )(o_a, o_b, o_c, z_main, z_main, z_main, x2, final_gain, w_branch, w_out)


W_GDOWN = 2 * GLA_KWIDTH + 2 * GLA_WIDTH
W_FQ = W_GDOWN + GK_RANK
W_FLOGIT = W_FQ + 3 * FOX_WIDTH
W_FGATE = W_FLOGIT + FOX_HEADS


def _pack_src_row(r):
    return (r + jnp.where(r >= W_GDOWN, GK_RANK, 0)
            + jnp.where(r + GK_RANK >= W_FLOGIT, FOX_HEADS, 0))


def _pack_copy(wt_hbm, buf, sem, step, *, tr, nblk):
    layer = step // nblk
    row = pl.multiple_of(_pack_src_row((step % nblk) * tr), 8)
    slot = step % 2
    return pltpu.make_async_copy(wt_hbm.at[layer, pl.ds(row, tr), :], buf.at[slot], sem.at[slot])


def _pack_kernel(wt_hbm, wm_ref, buf, sem, *, tr, nblk):
    s = pl.program_id(0)
    copy = functools.partial(_pack_copy, wt_hbm, buf, sem, tr=tr, nblk=nblk)

    @pl.when(s == 0)
    def _():
        copy(s).start()

    @pl.when(s + 1 < pl.num_programs(0))
    def _():
        copy(s + 1).start()

    copy(s).wait()
    wm_ref[...] = buf[s % 2].astype(BF16)


def _pack_w_in(w_in, *, tr):
    depth, d, in_cols = w_in.shape
    n_main = in_cols - GK_RANK - FOX_HEADS
    nblk = n_main // tr
    wt = jnp.swapaxes(w_in, 1, 2)
    w_main_t = pl.pallas_call(
        functools.partial(_pack_kernel, tr=tr, nblk=nblk),
        grid=(depth * nblk,),
        in_specs=[pl.BlockSpec(memory_space=pl.ANY)],
        out_specs=pl.BlockSpec((None, tr, d), lambda s: (s // nblk, s % nblk, 0)),
        out_shape=jax.ShapeDtypeStruct((depth, n_main, d), BF16),
        scratch_shapes=[pltpu.VMEM((2, tr, d), F32), pltpu.SemaphoreType.DMA((2,))],
        compiler_params=_cparams("arbitrary"),
        name="pack_w_in",
    )(wt)
    pad = jnp.zeros((depth, LANES - GD_COPIES * GK_RANK - FL_COPIES * FOX_HEADS, d), w_in.dtype)
    w_small_t = jnp.concatenate([wt[:, W_GDOWN:W_FQ, :]] * GD_COPIES
                                + [wt[:, W_FLOGIT:W_FGATE, :]] * FL_COPIES + [pad], axis=1).astype(BF16)
    return w_main_t, w_small_t


def _trunk(x, mem, norm_gain, w_in, w_gk_up, b_gk, gla_norm_gain, b_f, mem_norm_gain,
           w_mem_kv, w_branch, w_out, final_gain, *, tiles):
    B, S, D = x.shape
    M = mem.shape[1]
    depth = w_in.shape[0]
    x2 = x.reshape(B * S, D)
    mem2 = mem.reshape(B * M, D)
    w_main, w_small = _pack_w_in(w_in, tr=tiles["pack_tr"])
    w_branch_bf = w_branch.astype(BF16)
    w_out_bf = w_out.astype(BF16)
    for l in range(depth):
        z_main, z_small = _inproj(x2, norm_gain[l][None, :], w_main, w_small,
                                  layer=l, tm=tiles["in_tm"], tn=tiles["in_tn"])
        o_a = _gla(z_main, z_small, w_gk_up[l], b_gk[l][None, :], gla_norm_gain[l][None, :],
                   batch=B, seq=S, ts=tiles["gla_ts"], unroll=tiles["gla_unroll"])
        bf_row = jnp.zeros((1, LANES), F32).at[0, ZS_FLOGIT:ZS_FLOGIT + FL_COPIES * FOX_HEADS].set(
            jnp.tile(b_f[l], FL_COPIES))
        q_extra, k_extra, v_t = _fox_prep(z_main, z_small, bf_row, batch=B, seq=S, ts=tiles["prep_ts"])
        o_b = _fox_attn(z_main, q_extra, k_extra, v_t, batch=B, seq=S,
                        tq=tiles["fox_tq"], tk=tiles["fox_tk"])
        mkv = _memkv(mem2, mem_norm_gain[l][None, :], w_mem_kv[l].astype(BF16), tm=tiles["memkv_tm"])
        o_c = _mem_attn(z_main, mkv, batch=B, seq=S, mem_len=M, tm=tiles["mem_tm"])
        x2 = _merge_out(o_a, o_b, o_c, z_main, x2, w_branch_bf, w_out_bf, final_gain[None, :],
                        layer=l, tm=tiles["merge_tm"], final_norm=(l == depth - 1))
    return x2.reshape(B, S, D)


_TILES = dict(pack_tr=512, in_tm=1024, in_tn=2048, gla_ts=4096, gla_unroll=16, prep_ts=512,
              fox_tq=512, fox_tk=256,
              memkv_tm=256, mem_tm=2048, merge_tm=256)


def kernel(x, mem, norm_gain, w_in, w_gk_up, b_gk, gla_norm_gain, b_f, mem_norm_gain,
           w_mem_kv, w_branch, w_out, final_gain):
    return _trunk(x, mem, norm_gain, w_in, w_gk_up, b_gk, gla_norm_gain, b_f, mem_norm_gain,
                  w_mem_kv, w_branch, w_out, final_gain, tiles=_TILES)
```

```python
import functools

import jax
import jax.numpy as jnp
import numpy as np
from jax import lax
from jax.experimental import pallas as pl
from jax.experimental.pallas import tpu as pltpu

F32 = jnp.float32
BF16 = jnp.bfloat16

EPS = 1e-6

GLA_HEADS, GLA_DK, GLA_DV = 4, 128, 256
GLA_KWIDTH = GLA_HEADS * GLA_DK
GLA_WIDTH = GLA_HEADS * GLA_DV
GK_RANK = 16
GK_NORMALIZER = 16.0
FOX_HEADS, FOX_DH = 8, 64
FOX_WIDTH = FOX_HEADS * FOX_DH
MEM_HEADS, MEM_DH = 4, 128
MEM_WIDTH = MEM_HEADS * MEM_DH
N_BRANCH = 3

LANES = 128
VMEM_LIMIT_BYTES = 56 * 1024 * 1024

ZM_GQ = 0
ZM_GK = ZM_GQ + GLA_KWIDTH
ZM_GV = ZM_GK + GLA_KWIDTH
ZM_GGATE = ZM_GV + GLA_WIDTH
ZM_FQ = ZM_GGATE + GLA_WIDTH
ZM_FK = ZM_FQ + FOX_WIDTH
ZM_FV = ZM_FK + FOX_WIDTH
ZM_FGATE = ZM_FV + FOX_WIDTH
ZM_MQ = ZM_FGATE + FOX_WIDTH
ZM_MGATE = ZM_MQ + MEM_WIDTH
ZM_MERGE = ZM_MGATE + MEM_WIDTH
GD_COPIES = 6
FL_COPIES = 3
ZS_FLOGIT = GD_COPIES * GK_RANK

GLA_CHUNK = 128
GLA_SUB = 8
FOX_EXTRA = 6
FOX_VROWS = 80
PREP_CUMSUM_ROWS = 128
LOG2E = 1.4426950408889634


def _cparams(*sem):
    return pltpu.CompilerParams(dimension_semantics=sem, vmem_limit_bytes=VMEM_LIMIT_BYTES)


def _sigmoid(x):
    return 0.5 * jnp.tanh(0.5 * x) + 0.5


def _log_sigmoid(x):
    return jnp.minimum(x, 0.0) - jnp.log(1.0 + jnp.exp(-jnp.abs(x)))


def _split3(x):
    a = x.astype(BF16)
    r = x - a.astype(F32)
    b = r.astype(BF16)
    c = (r - b.astype(F32)).astype(BF16)
    return a, b, c


def _dot_nt(a, b):
    return lax.dot_general(a, b, (((1,), (1,)), ((), ())), preferred_element_type=F32)


def _tri_cumsum(tri_bf, x):
    a, b, c = _split3(x)
    dot = functools.partial(jnp.dot, preferred_element_type=F32)
    return dot(tri_bf, a) + dot(tri_bf, b) + dot(tri_bf, c)


def _inproj_kernel(x_ref, gain_ref, wm_ref, ws_ref, zm_ref, zs_ref, h_scr):
    @pl.when(pl.program_id(1) == 0)
    def _():
        x = x_ref[...]
        ms = jnp.mean(x * x, axis=-1, keepdims=True)
        h = (x * lax.rsqrt(ms + EPS) * gain_ref[...]).astype(BF16)
        h_scr[...] = h
        zs_ref[...] = _dot_nt(h, ws_ref[...])

    zm_ref[...] = _dot_nt(h_scr[...], wm_ref[...]).astype(BF16)


def _inproj(x2, gain, w_main_t, w_small_t, *, layer, tm, tn):
    T, D = x2.shape
    NM = w_main_t.shape[1]
    return pl.pallas_call(
        _inproj_kernel,
        grid=(T // tm, NM // tn),
        in_specs=[
            pl.BlockSpec((tm, D), lambda i, j: (i, 0)),
            pl.BlockSpec((1, D), lambda i, j: (0, 0)),
            pl.BlockSpec((None, tn, D), lambda i, j: (layer, j, 0)),
            pl.BlockSpec((None, LANES, D), lambda i, j: (layer, 0, 0)),
        ],
        out_specs=[
            pl.BlockSpec((tm, tn), lambda i, j: (i, j)),
            pl.BlockSpec((tm, LANES), lambda i, j: (i, 0)),
        ],
        out_shape=[
            jax.ShapeDtypeStruct((T, NM), BF16),
            jax.ShapeDtypeStruct((T, LANES), F32),
        ],
        scratch_shapes=[pltpu.VMEM((tm, D), BF16)],
        compiler_params=_cparams("parallel", "arbitrary"),
        name="inproj",
    )(x2, gain, w_main_t, w_small_t)


def _gla_kernel(q_ref, k_ref, v_ref, zs_ref, gate_ref, wup_ref, bgk_ref, gain_ref,
                o_ref, state_scr, b_scr, p_scr, *, nchunks, unroll):
    C, SUB = GLA_CHUNK, GLA_SUB
    NSUB = C // SUB
    NPAIR = SUB // 2
    DK, DV = GLA_DK, GLA_DV
    dot = functools.partial(jnp.dot, preferred_element_type=F32)

    @pl.when(pl.program_id(2) == 0)
    def _():
        state_scr[...] = jnp.zeros_like(state_scr)

    row = lax.broadcasted_iota(jnp.int32, (C, C), 0)
    col = lax.broadcasted_iota(jnp.int32, (C, C), 1)
    tri = (col <= row).astype(BF16)
    same_sub = (row // SUB) == (col // SUB)
    causal = col <= row

    z1, z2, z3 = _split3(zs_ref[...])
    zlane = lax.broadcasted_iota(jnp.int32, z1.shape, 1)
    lhs = jnp.where(zlane < 3 * GK_RANK, z1, jnp.where(zlane < 5 * GK_RANK, z2, z3))
    w1, w2, w3 = _split3(wup_ref[...])
    rhs = jnp.concatenate([w1, w2, w3, w1, w2, w1,
                           jnp.zeros((LANES - GD_COPIES * GK_RANK, DK), BF16)], axis=0)
    logits = dot(lhs, rhs) + bgk_ref[...]
    g = _log_sigmoid(logits) * (LOG2E / GK_NORMALIZER)
    for c in range(nchunks):
        b_scr[c * C:(c + 1) * C, :] = _tri_cumsum(tri, g[c * C:(c + 1) * C, :])

    cs_r = lax.broadcasted_iota(jnp.int32, (2 * DK, LANES), 0)
    cs_c = lax.broadcasted_iota(jnp.int32, (2 * DK, LANES), 1)
    chansum = ((cs_r // DK) == (cs_c % 2)).astype(BF16)
    pair_of_lane = (lax.broadcasted_iota(jnp.int32, (C, LANES), 1) % SUB) // 2
    scale = DK ** -0.5
    gain = gain_ref[...]

    def chunk(c, p_slot, state):
        r0 = pl.multiple_of(c * C, C)
        rows = pl.ds(r0, C)
        q = q_ref[rows, :].astype(F32) * scale
        k = k_ref[rows, :].astype(F32)
        v = v_ref[rows, :]
        b = b_scr[rows, :]
        b_last = b[C - 1:C, :]

        o = dot((q * jnp.exp2(b)).astype(BF16), state.astype(BF16))

        cross = [jnp.zeros((SUB, C), F32)]
        for i in range(1, NSUB):
            n = i * SUB
            bref = b[n - 1:n, :]
            q_t = (q[n:n + SUB, :] * jnp.exp2(b[n:n + SUB, :] - bref)).astype(BF16)
            k_t = (k[:n, :] * jnp.exp2(bref - b[:n, :])).astype(BF16)
            k_t = jnp.concatenate([k_t, jnp.zeros((C - n, DK), BF16)], axis=0)
            cross.append(lax.dot_general(q_t, k_t, (((1,), (1,)), ((), ())),
                                         preferred_element_type=F32))
        a_cross = jnp.concatenate(cross, axis=0)

        k3 = k.reshape(NSUB, SUB, DK)
        b3 = b.reshape(NSUB, SUB, DK)
        for j in range(SUB):
            kj = jnp.broadcast_to(k3[:, j:j + 1, :], (NSUB, SUB, DK)).reshape(C, DK)
            bj = jnp.broadcast_to(b3[:, j:j + 1, :], (NSUB, SUB, DK)).reshape(C, DK)
            p = q * kj * jnp.exp2(jnp.minimum(b - bj, 0.0))
            p_slot[(j // 2) * C:(j // 2 + 1) * C, (j % 2) * DK:(j % 2 + 1) * DK] = p.astype(BF16)
        d = dot(p_slot[...], chansum)
        a_diag = d[0:C, :]
        for t in range(1, NPAIR):
            a_diag = jnp.where(pair_of_lane == t, d[t * C:(t + 1) * C, :], a_diag)

        a = jnp.where(causal, jnp.where(same_sub, a_diag[:, :C], a_cross), 0.0)
        o = o + dot(a.astype(BF16), v)

        k_dec = (k * jnp.exp2(b_last - b)).astype(BF16)
        upd = lax.dot_general(k_dec, v, (((0,), (0,)), ((), ())),
                              preferred_element_type=F32)
        decay_col = jnp.transpose(jnp.broadcast_to(jnp.exp2(b_last), (DK, DK)))[:, 0:1]
        new_state = state * decay_col + upd

        ms = jnp.mean(o * o, axis=-1, keepdims=True)
        on = o * lax.rsqrt(ms + EPS) * gain
        gt = gate_ref[rows, :].astype(F32)
        o_ref[rows, :] = (on * (gt * _sigmoid(gt))).astype(o_ref.dtype)
        return new_state

    def group(gi, carry):
        state = state_scr[...]
        for u in range(unroll):
            state = chunk(gi * unroll + u, p_scr.at[u], state)
        state_scr[...] = state
        return carry

    lax.fori_loop(0, nchunks // unroll, group, 0)


def _gla(z_main, z_small, w_up, b_gk, gain, *, batch, seq, ts, unroll):
    T = z_main.shape[0]
    nsb = seq // ts
    nchunks = ts // GLA_CHUNK
    rowmap = lambda b, h_, s: b * nsb + s
    kblk = ZM_GK // GLA_DK
    vblk = ZM_GV // GLA_DV
    gblk = ZM_GGATE // GLA_DV
    return pl.pallas_call(
        functools.partial(_gla_kernel, nchunks=nchunks, unroll=unroll),
        grid=(batch, GLA_HEADS, nsb),
        in_specs=[
            pl.BlockSpec((ts, GLA_DK), lambda b, h_, s: (rowmap(b, h_, s), h_)),
            pl.BlockSpec((ts, GLA_DK), lambda b, h_, s: (rowmap(b, h_, s), kblk + h_)),
            pl.BlockSpec((ts, GLA_DV), lambda b, h_, s: (rowmap(b, h_, s), vblk + h_)),
            pl.BlockSpec((ts, LANES), lambda b, h_, s: (rowmap(b, h_, s), 0)),
            pl.BlockSpec((ts, GLA_DV), lambda b, h_, s: (rowmap(b, h_, s), gblk + h_)),
            pl.BlockSpec((GK_RANK, GLA_DK), lambda b, h_, s: (0, h_)),
            pl.BlockSpec((1, GLA_DK), lambda b, h_, s: (0, h_)),
            pl.BlockSpec((1, GLA_DV), lambda b, h_, s: (0, 0)),
        ],
        out_specs=pl.BlockSpec((ts, GLA_DV), lambda b, h_, s: (rowmap(b, h_, s), h_)),
        out_shape=jax.ShapeDtypeStruct((T, GLA_WIDTH), BF16),
        scratch_shapes=[
            pltpu.VMEM((GLA_DK, GLA_DV), F32),
            pltpu.VMEM((ts, GLA_DK), F32),
            pltpu.VMEM((unroll, GLA_SUB // 2 * GLA_CHUNK, 2 * GLA_DK), BF16),
        ],
        compiler_params=_cparams("parallel", "parallel", "arbitrary"),
        name="gla",
    )(z_main, z_main, z_main, z_small, z_main, w_up, b_gk, gain)


def _fox_extra_tables():
    npair = FOX_HEADS // 2
    pq = np.zeros((LANES, npair * LANES), np.float32)
    pk = np.zeros((LANES, npair * LANES), np.float32)
    cq = np.zeros((1, npair * LANES), np.float32)
    ck = np.zeros((1, npair * LANES), np.float32)
    for h in range(FOX_HEADS):
        base = (h // 2) * LANES + (h % 2) * FOX_EXTRA
        for t in range(FL_COPIES):
            pq[ZS_FLOGIT + FOX_HEADS * t + h, base + t] = 1.0
            cq[0, base + FL_COPIES + t] = 1.0
            ck[0, base + t] = 1.0
            pk[ZS_FLOGIT + FOX_HEADS * t + h, base + FL_COPIES + t] = -1.0
    return jnp.asarray(pq, BF16), jnp.asarray(pk, BF16), jnp.asarray(cq), jnp.asarray(ck)


def _fox_prep_kernel(v_ref, zs_ref, bf_ref, pq_ref, pk_ref, cq_ref, ck_ref,
                     qx_ref, kx_ref, vt_ref, carry_scr, *, ts):
    @pl.when(pl.program_id(1) == 0)
    def _():
        carry_scr[...] = jnp.zeros_like(carry_scr)

    dot = functools.partial(jnp.dot, preferred_element_type=F32)
    blk = PREP_CUMSUM_ROWS
    row = lax.broadcasted_iota(jnp.int32, (blk, blk), 0)
    col = lax.broadcasted_iota(jnp.int32, (blk, blk), 1)
    tri = (col <= row).astype(BF16)
    log_f = _log_sigmoid(zs_ref[...] + bf_ref[...])
    carry = carry_scr[...]
    parts = []
    for r in range(0, ts, blk):
        part = _tri_cumsum(tri, log_f[r:r + blk, :]) + carry
        carry = part[blk - 1:blk, :]
        parts.append(part)
    carry_scr[...] = carry
    c_all = jnp.concatenate(parts, axis=0)

    c1, c2, c3 = _split3(c_all * LOG2E)
    lane = lax.broadcasted_iota(jnp.int32, (ts, LANES), 1)
    cs = jnp.where(lane < ZS_FLOGIT + FOX_HEADS, c1, jnp.where(lane < ZS_FLOGIT + 2 * FOX_HEADS, c2, c3))
    qx_ref[...] = (dot(cs, pq_ref[...]) + cq_ref[...]).astype(BF16)
    kx_ref[...] = (dot(cs, pk_ref[...]) + ck_ref[...]).astype(BF16)

    v_t = jnp.transpose(v_ref[...].astype(F32))
    pad_rows = FOX_VROWS - FOX_DH
    ones_blk = jnp.where(lax.broadcasted_iota(jnp.int32, (pad_rows, ts), 0) == 0, 1.0, 0.0)
    pieces = []
    for h in range(FOX_HEADS):
        pieces += [v_t[h * FOX_DH:(h + 1) * FOX_DH, :], ones_blk]
    vt_ref[...] = jnp.concatenate(pieces, axis=0).astype(BF16)


def _fox_prep(z_main, z_small, bf_row, *, batch, seq, ts):
    T = z_main.shape[0]
    nsb = seq // ts
    vblk = ZM_FV // FOX_WIDTH
    npair = FOX_HEADS // 2
    extra = jax.ShapeDtypeStruct((T, npair * LANES), BF16)
    pq, pk, cq, ck = _fox_extra_tables()
    const = lambda b, s: (0, 0)
    return pl.pallas_call(
        functools.partial(_fox_prep_kernel, ts=ts),
        grid=(batch, nsb),
        in_specs=[
            pl.BlockSpec((ts, FOX_WIDTH), lambda b, s: (b * nsb + s, vblk)),
            pl.BlockSpec((ts, LANES), lambda b, s: (b * nsb + s, 0)),
            pl.BlockSpec((1, LANES), const),
            pl.BlockSpec(pq.shape, const),
            pl.BlockSpec(pk.shape, const),
            pl.BlockSpec(cq.shape, const),
            pl.BlockSpec(ck.shape, const),
        ],
        out_specs=[
            pl.BlockSpec((ts, npair * LANES), lambda b, s: (b * nsb + s, 0)),
            pl.BlockSpec((ts, npair * LANES), lambda b, s: (b * nsb + s, 0)),
            pl.BlockSpec((FOX_HEADS * FOX_VROWS, ts), lambda b, s: (b, s)),
        ],
        out_shape=[extra, extra,
                   jax.ShapeDtypeStruct((batch * FOX_HEADS * FOX_VROWS, seq), BF16)],
        scratch_shapes=[pltpu.VMEM((1, LANES), F32)],
        compiler_params=_cparams("parallel", "arbitrary"),
        name="fox_prep",
    )(z_main, z_small, bf_row, pq, pk, cq, ck)


def _fox_attn_kernel(q_ref, qx_ref, k_ref, kx_ref, vt_ref, gate_ref, o_ref, m_scr, acc_scr, *, tq, tk):
    qi = pl.program_id(2)
    m_scr[...] = jnp.full_like(m_scr, -jnp.inf)
    acc_scr[...] = jnp.zeros_like(acc_scr)

    q_scaled = (q_ref[...].astype(F32) * (FOX_DH ** -0.5 * LOG2E)).astype(BF16)
    q2 = jnp.concatenate([q_scaled, qx_ref[...]], axis=1)
    lane = lax.broadcasted_iota(jnp.int32, (1, 2 * LANES), 1)
    head_lanes = [((lane >= hh * FOX_DH) & (lane < (hh + 1) * FOX_DH))
                  | ((lane >= LANES + hh * FOX_EXTRA) & (lane < LANES + (hh + 1) * FOX_EXTRA))
                  for hh in range(2)]

    def scores(j, hh):
        k0 = pl.multiple_of(j * tk, tk)
        k2 = jnp.concatenate([k_ref[pl.ds(k0, tk), :], kx_ref[pl.ds(k0, tk), :]], axis=1)
        k_h = jnp.where(head_lanes[hh], k2, jnp.zeros_like(k2))
        return _dot_nt(k_h, q2)

    def absorb(j, hh, st, stats, key_offset):
        m_prev, acc = stats
        if key_offset is not None:
            k_pos = key_offset + lax.broadcasted_iota(jnp.int32, (tk, tq), 0)
            q_pos = lax.broadcasted_iota(jnp.int32, (tk, tq), 1)
            st = jnp.where(k_pos <= q_pos, st, -jnp.inf)
        m_new = jnp.maximum(m_prev, jnp.max(st, axis=0, keepdims=True))
        alpha = jnp.exp2(m_prev - m_new)
        p = jnp.exp2(st - m_new).astype(BF16)
        k0 = pl.multiple_of(j * tk, tk)
        vt = vt_ref[hh * FOX_VROWS:(hh + 1) * FOX_VROWS, pl.ds(k0, tk)]
        pv = jnp.dot(vt, p, preferred_element_type=F32)
        return m_new, acc * alpha + pv

    def run(chunks, key_offsets):
        sts = [[scores(j, hh) for hh in range(2)] for j in chunks]
        stats = [(m_scr[hh], acc_scr[hh]) for hh in range(2)]
        for ci, j in enumerate(chunks):
            for hh in range(2):
                stats[hh] = absorb(j, hh, sts[ci][hh], stats[hh], key_offsets[ci])
        for hh in range(2):
            m_scr[hh], acc_scr[hh] = stats[hh]

    sub = tq // tk
    span = lambda s: [sub * s + c for c in range(sub)]
    own_offsets = [c * tk for c in range(sub)]

    def pair(sp, carry):
        run(span(2 * sp) + span(2 * sp + 1), [None] * (2 * sub))
        return carry

    lax.fori_loop(0, qi // 2, pair, 0)

    @pl.when(qi % 2 == 1)
    def _():
        run(span(qi - 1) + span(qi), [None] * sub + own_offsets)

    @pl.when(qi % 2 == 0)
    def _():
        run(span(qi), own_offsets)

    outs = []
    for hh in range(2):
        acc = acc_scr[hh]
        outs.append(acc[:FOX_DH, :] / acc[FOX_DH:FOX_DH + 1, :])
    ot = jnp.concatenate(outs, axis=0)
    gt = gate_ref[...].astype(F32)
    o_ref[...] = (jnp.transpose(ot) * (gt * _sigmoid(gt))).astype(o_ref.dtype)


def _fox_attn(z_main, q_extra, k_extra, v_t, *, batch, seq, tq, tk):
    T = z_main.shape[0]
    nq = seq // tq
    npair = FOX_HEADS // 2
    qblk = ZM_FQ // LANES
    kblk = ZM_FK // LANES
    gblk = ZM_FGATE // LANES
    return pl.pallas_call(
        functools.partial(_fox_attn_kernel, tq=tq, tk=tk),
        grid=(batch, npair, nq),
        in_specs=[
            pl.BlockSpec((tq, LANES), lambda b, p, i: (b * nq + i, qblk + p)),
            pl.BlockSpec((tq, LANES), lambda b, p, i: (b * nq + i, p)),
            pl.BlockSpec((seq, LANES), lambda b, p, i: (b, kblk + p)),
            pl.BlockSpec((seq, LANES), lambda b, p, i: (b, p)),
            pl.BlockSpec((2 * FOX_VROWS, seq), lambda b, p, i: (b * npair + p, 0)),
            pl.BlockSpec((tq, LANES), lambda b, p, i: (b * nq + i, gblk + p)),
        ],
        out_specs=pl.BlockSpec((tq, LANES), lambda b, p, i: (b * nq + i, p)),
        out_shape=jax.ShapeDtypeStruct((T, FOX_WIDTH), BF16),
        scratch_shapes=[
            pltpu.VMEM((2, 1, tq), F32),
            pltpu.VMEM((2, FOX_VROWS, tq), F32),
        ],
        compiler_params=_cparams("parallel", "parallel", "arbitrary"),
        name="fox_attn",
    )(z_main, q_extra, z_main, k_extra, v_t, z_main)


def _memkv_kernel(mem_ref, gain_ref, w_ref, o_ref):
    x = mem_ref[...]
    ms = jnp.mean(x * x, axis=-1, keepdims=True)
    h = (x * lax.rsqrt(ms + EPS) * gain_ref[...]).astype(BF16)
    o_ref[...] = jnp.dot(h, w_ref[...], preferred_element_type=F32).astype(o_ref.dtype)


def _memkv(mem2, gain, w_kv, *, tm):
    R, D = mem2.shape
    N = w_kv.shape[1]
    return pl.pallas_call(
        _memkv_kernel,
        grid=(R // tm,),
        in_specs=[
            pl.BlockSpec((tm, D), lambda i: (i, 0)),
            pl.BlockSpec((1, D), lambda i: (0, 0)),
            pl.BlockSpec((D, N), lambda i: (0, 0)),
        ],
        out_specs=pl.BlockSpec((tm, N), lambda i: (i, 0)),
        out_shape=jax.ShapeDtypeStruct((R, N), BF16),
        compiler_params=_cparams("parallel"),
        name="memkv",
    )(mem2, gain, w_kv)


def _mem_attn_kernel(q_ref, gate_ref, mk_ref, mv_ref, o_ref):
    scale = MEM_DH ** -0.5 * LOG2E
    outs = []
    for h in range(MEM_HEADS):
        sl = slice(h * MEM_DH, (h + 1) * MEM_DH)
        s = _dot_nt(q_ref[:, sl], mk_ref[:, sl]) * scale
        m = jnp.max(s, axis=-1, keepdims=True)
        p = jnp.exp2(s - m)
        l = jnp.sum(p, axis=-1, keepdims=True)
        pv = jnp.dot(p.astype(BF16), mv_ref[:, sl], preferred_element_type=F32)
        outs.append(pv / l)
    o = jnp.concatenate(outs, axis=1)
    gt = gate_ref[...].astype(F32)
    o_ref[...] = (o * (gt * _sigmoid(gt))).astype(o_ref.dtype)


def _mem_attn(z_main, mkv, *, batch, seq, mem_len, tm):
    T = z_main.shape[0]
    nsb = seq // tm
    qblk = ZM_MQ // MEM_WIDTH
    gblk = ZM_MGATE // MEM_WIDTH
    return pl.pallas_call(
        _mem_attn_kernel,
        grid=(batch, nsb),
        in_specs=[
            pl.BlockSpec((tm, MEM_WIDTH), lambda b, s: (b * nsb + s, qblk)),
            pl.BlockSpec((tm, MEM_WIDTH), lambda b, s: (b * nsb + s, gblk)),
            pl.BlockSpec((mem_len, MEM_WIDTH), lambda b, s: (b, 0)),
            pl.BlockSpec((mem_len, MEM_WIDTH), lambda b, s: (b, 1)),
        ],
        out_specs=pl.BlockSpec((tm, MEM_WIDTH), lambda b, s: (b * nsb + s, 0)),
        out_shape=jax.ShapeDtypeStruct((T, MEM_WIDTH), BF16),
        compiler_params=_cparams("parallel", "parallel"),
        name="mem_attn",
    )(z_main, z_main, mkv, mkv)


def _merge_out_kernel(oa_ref, ob_ref, oc_ref, g0_ref, g1_ref, g2_ref, x_ref, gain_ref,
                      wbr_hbm, wout_hbm, o_ref, wbr_vmem, wout_vmem, sem, *, layer, final_norm):
    @pl.when(pl.program_id(0) == 0)
    def _():
        copies = [pltpu.make_async_copy(wbr_hbm.at[layer], wbr_vmem, sem.at[0]),
                  pltpu.make_async_copy(wout_hbm.at[layer], wout_vmem, sem.at[1])]
        for cp in copies:
            cp.start()
        for cp in copies:
            cp.wait()

    dot = functools.partial(jnp.dot, preferred_element_type=F32)
    r_b, r_c = GLA_WIDTH, GLA_WIDTH + FOX_WIDTH
    y = _sigmoid(g0_ref[...].astype(F32)) * dot(oa_ref[...], wbr_vmem[0:r_b, :])
    y = y + _sigmoid(g1_ref[...].astype(F32)) * dot(ob_ref[...], wbr_vmem[r_b:r_c, :])
    y = y + _sigmoid(g2_ref[...].astype(F32)) * dot(oc_ref[...], wbr_vmem[r_c:, :])
    x = x_ref[...] + dot(y.astype(BF16), wout_vmem[...])
    if final_norm:
        ms = jnp.mean(x * x, axis=-1, keepdims=True)
        x = x * lax.rsqrt(ms + EPS) * gain_ref[...]
    o_ref[...] = x


def _merge_out(o_a, o_b, o_c, z_main, x2, w_branch, w_out, final_gain, *, layer, tm, final_norm):
    T, D = x2.shape
    gblk = ZM_MERGE // D
    return pl.pallas_call(
        functools.partial(_merge_out_kernel, layer=layer, final_norm=final_norm),
        grid=(T // tm,),
        in_specs=[
            pl.BlockSpec((tm, GLA_WIDTH), lambda i: (i, 0)),
            pl.BlockSpec((tm, FOX_WIDTH), lambda i: (i, 0)),
            pl.BlockSpec((tm, MEM_WIDTH), lambda i: (i, 0)),
            pl.BlockSpec((tm, D), lambda i: (i, gblk)),
            pl.BlockSpec((tm, D), lambda i: (i, gblk + 1)),
            pl.BlockSpec((tm, D), lambda i: (i, gblk + 2)),
            pl.BlockSpec((tm, D), lambda i: (i, 0)),
            pl.BlockSpec((1, D), lambda i: (0, 0)),
            pl.BlockSpec(memory_space=pl.ANY),
            pl.BlockSpec(memory_space=pl.ANY),
        ],
        out_specs=pl.BlockSpec((tm, D), lambda i: (i, 0)),
        out_shape=jax.ShapeDtypeStruct((T, D), F32),
        scratch_shapes=[
            pltpu.VMEM(w_branch.shape[1:], BF16),
            pltpu.VMEM(w_out.shape[1:], BF16),
            pltpu.SemaphoreType.DMA((2,)),
        ],
        compiler_params=_cparams("arbitrary"),
        name="merge_out",
    )(o_a, o_b, o_c, z_main, z_main, z_main, x2, final_gain, w_branch, w_out)


W_GDOWN = 2 * GLA_KWIDTH + 2 * GLA_WIDTH
W_FQ = W_GDOWN + GK_RANK
W_FLOGIT = W_FQ + 3 * FOX_WIDTH
W_FGATE = W_FLOGIT + FOX_HEADS


def _pack_src_row(r):
    return (r + jnp.where(r >= W_GDOWN, GK_RANK, 0)
            + jnp.where(r + GK_RANK >= W_FLOGIT, FOX_HEADS, 0))


def _pack_copy(wt_hbm, buf, sem, step, *, tr, nblk):
    layer = step // nblk
    row = pl.multiple_of(_pack_src_row((step % nblk) * tr), 8)
    slot = step % 2
    return pltpu.make_async_copy(wt_hbm.at[layer, pl.ds(row, tr), :], buf.at[slot], sem.at[slot])


def _pack_kernel(wt_hbm, wm_ref, buf, sem, *, tr, nblk):
    s = pl.program_id(0)
    copy = functools.partial(_pack_copy, wt_hbm, buf, sem, tr=tr, nblk=nblk)

    @pl.when(s == 0)
    def _():
        copy(s).start()

    @pl.when(s + 1 < pl.num_programs(0))
    def _():
        copy(s + 1).start()

    copy(s).wait()
    wm_ref[...] = buf[s % 2].astype(BF16)


def _pack_w_in(w_in, *, tr):
    depth, d, in_cols = w_in.shape
    n_main = in_cols - GK_RANK - FOX_HEADS
    nblk = n_main // tr
    wt = jnp.swapaxes(w_in, 1, 2)
    w_main_t = pl.pallas_call(
        functools.partial(_pack_kernel, tr=tr, nblk=nblk),
        grid=(depth * nblk,),
        in_specs=[pl.BlockSpec(memory_space=pl.ANY)],
        out_specs=pl.BlockSpec((None, tr, d), lambda s: (s // nblk, s % nblk, 0)),
        out_shape=jax.ShapeDtypeStruct((depth, n_main, d), BF16),
        scratch_shapes=[pltpu.VMEM((2, tr, d), F32), pltpu.SemaphoreType.DMA((2,))],
        compiler_params=_cparams("arbitrary"),
        name="pack_w_in",
    )(wt)
    pad = jnp.zeros((depth, LANES - GD_COPIES * GK_RANK - FL_COPIES * FOX_HEADS, d), w_in.dtype)
    w_small_t = jnp.concatenate([wt[:, W_GDOWN:W_FQ, :]] * GD_COPIES
                                + [wt[:, W_FLOGIT:W_FGATE, :]] * FL_COPIES + [pad], axis=1).astype(BF16)
    return w_main_t, w_small_t


def _trunk(x, mem, norm_gain, w_in, w_gk_up, b_gk, gla_norm_gain, b_f, mem_norm_gain,
           w_mem_kv, w_branch, w_out, final_gain, *, tiles):
    B, S, D = x.shape
    M = mem.shape[1]
    depth = w_in.shape[0]
    x2 = x.reshape(B * S, D)
    mem2 = mem.reshape(B * M, D)
    w_main, w_small = _pack_w_in(w_in, tr=tiles["pack_tr"])
    w_branch_bf = w_branch.astype(BF16)
    w_out_bf = w_out.astype(BF16)
    for l in range(depth):
        z_main, z_small = _inproj(x2, norm_gain[l][None, :], w_main, w_small,
                                  layer=l, tm=tiles["in_tm"], tn=tiles["in_tn"])
        o_a = _gla(z_main, z_small, w_gk_up[l], b_gk[l][None, :], gla_norm_gain[l][None, :],
                   batch=B, seq=S, ts=tiles["gla_ts"], unroll=tiles["gla_unroll"])
        bf_row = jnp.zeros((1, LANES), F32).at[0, ZS_FLOGIT:ZS_FLOGIT + FL_COPIES * FOX_HEADS].set(
            jnp.tile(b_f[l], FL_COPIES))
        q_extra, k_extra, v_t = _fox_prep(z_main, z_small, bf_row, batch=B, seq=S, ts=tiles["prep_ts"])
        o_b = _fox_attn(z_main, q_extra, k_extra, v_t, batch=B, seq=S,
                        tq=tiles["fox_tq"], tk=tiles["fox_tk"])
        mkv = _memkv(mem2, mem_norm_gain[l][None, :], w_mem_kv[l].astype(BF16), tm=tiles["memkv_tm"])
        o_c = _mem_attn(z_main, mkv, batch=B, seq=S, mem_len=M, tm=tiles["mem_tm"])
        x2 = _merge_out(o_a, o_b, o_c, z_main, x2, w_branch_bf, w_out_bf, final_gain[None, :],
                        layer=l, tm=tiles["merge_tm"], final_norm=(l == depth - 1))
    return x2.reshape(B, S, D)


_TILES = dict(pack_tr=512, in_tm=1024, in_tn=2048, gla_ts=4096, gla_unroll=16, prep_ts=1024,
              fox_tq=512, fox_tk=256,
              memkv_tm=256, mem_tm=2048, merge_tm=256)


def kernel(x, mem, norm_gain, w_in, w_gk_up, b_gk, gla_norm_gain, b_f, mem_norm_gain,
           w_mem_kv, w_branch, w_out, final_gain):
    return _trunk(x, mem, norm_gain, w_in, w_gk_up, b_gk, gla_norm_gain, b_f, mem_norm_gain,
                  w_mem_kv, w_branch, w_out, final_gain, tiles=_TILES)
```
